```python
import math
import jax, jax.numpy as jnp
from jax import lax
import numpy as np

D_MODEL = 1024
BATCH = 32
SEQ = 2048
DEPTH = 1
DEC_BATCH = 32
DEC_SEQ = 16
PAST_LEN = 2048

CHUNK = 64
S5_WIDTH = 512
S5_GROUP = 16
S5_GROUPS = S5_WIDTH // S5_GROUP
S5_STATE = 64
RET_HEADS = 4
RET_DK = 128
RET_DV = 256
RET_QK = RET_HEADS * RET_DK
RET_V = RET_HEADS * RET_DV
ROPE_BASE = 10000.0
MOE_GROUPS = 4
MOE_EXPERTS = 8
MOE_TOPK = 2
MOE_FF = 256
EPS = 1e-6
IN_SPLITS = (S5_WIDTH, RET_QK, RET_QK, RET_V, RET_V, D_MODEL, D_MODEL)
IN_WIDTH = S5_WIDTH + 2 * RET_QK + 2 * RET_V + 2 * D_MODEL

kernel_name = 'hybrid_s5_retention_hmoe_stream_step'


def rmsnorm(x, g):
    xf = x.astype(jnp.float32)
    y = xf * lax.rsqrt(jnp.mean(xf * xf, axis=-1, keepdims=True) + EPS)
    return (y * g.astype(jnp.float32)).astype(x.dtype)


def modulate(n, shift, scale):
    return n * (1.0 + scale[:, None, :]) + shift[:, None, :]


def s5_discretise(a_re, a_im, log_dt, b_re, b_im):
    f32 = jnp.float32
    a_re = a_re.astype(f32)
    a_im = a_im.astype(f32)
    dt = jnp.exp(log_dt.astype(f32))[:, None]
    mag = jnp.exp(a_re * dt)
    ang = a_im * dt
    ab_re = mag * jnp.cos(ang)
    ab_im = mag * jnp.sin(ang)
    den = a_re * a_re + a_im * a_im
    nr = ab_re - 1.0
    ni = ab_im
    f_re = (nr * a_re + ni * a_im) / den
    f_im = (ni * a_re - nr * a_im) / den
    b_re = b_re.astype(f32)
    b_im = b_im.astype(f32)
    bb_re = f_re[..., None] * b_re - f_im[..., None] * b_im
    bb_im = f_re[..., None] * b_im + f_im[..., None] * b_re
    return ab_re, ab_im, bb_re, bb_im


def _ssm_combine(e1, e2):
    ar1, ai1, br1, bi1 = e1
    ar2, ai2, br2, bi2 = e2
    ar = ar1 * ar2 - ai1 * ai2
    ai = ar1 * ai2 + ai1 * ar2
    br = ar2 * br1 - ai2 * bi1 + br2
    bi = ar2 * bi1 + ai2 * br1 + bi2
    return (ar, ai, br, bi)


def s5_mixer(u, h0_re, h0_im, a_re, a_im, log_dt, b_re, b_im, c_re, c_im, d_skip, w_glu, b_glu):
    f32 = jnp.float32
    bsz, seq_len, _ = u.shape
    uf = u.astype(f32).reshape(bsz, seq_len, S5_GROUPS, S5_GROUP)
    ab_re, ab_im, bb_re, bb_im = s5_discretise(a_re, a_im, log_dt, b_re, b_im)
    bu_re = jnp.einsum('blgj,gpj->blgp', uf, bb_re)
    bu_im = jnp.einsum('blgj,gpj->blgp', uf, bb_im)
    shape = bu_re.shape
    elems = (jnp.broadcast_to(ab_re, shape), jnp.broadcast_to(ab_im, shape), bu_re, bu_im)
    ap_re, ap_im, x_re, x_im = lax.associative_scan(_ssm_combine, elems, axis=1)
    if h0_re is not None:
        h_re = h0_re.astype(f32)[:, None]
        h_im = h0_im.astype(f32)[:, None]
        x_re, x_im = (x_re + ap_re * h_re - ap_im * h_im,
                      x_im + ap_re * h_im + ap_im * h_re)
    y = (jnp.einsum('blgp,gjp->blgj', x_re, c_re.astype(f32))
         - jnp.einsum('blgp,gjp->blgj', x_im, c_im.astype(f32))
         + d_skip.astype(f32).reshape(S5_GROUPS, S5_GROUP) * uf)
    y = y.reshape(bsz, seq_len, S5_WIDTH)
    z = jax.nn.gelu(y)
    out = z * jax.nn.sigmoid(z @ w_glu.astype(f32) + b_glu.astype(f32))
    return out, x_re[:, -1], x_im[:, -1]


def rotary(x, pos):
    half = RET_DK // 2
    theta = 1.0 / (ROPE_BASE ** jnp.linspace(0.0, 1.0, half, dtype=jnp.float32))
    ang = pos.astype(jnp.float32)[:, None] * theta[None, :]
    cos = jnp.cos(ang)[None, :, None, :]
    sin = jnp.sin(ang)[None, :, None, :]
    x1 = x[..., :half]
    x2 = x[..., half:]
    return jnp.concatenate([x1 * cos - x2 * sin, x1 * sin + x2 * cos], axis=-1)


def retention_mixer(q, k, v, s0, offset):
    f32 = jnp.float32
    bsz, seq_len, _ = q.shape
    cl = min(CHUNK, seq_len)
    nc = seq_len // cl
    pos = offset + jnp.arange(seq_len)
    qh = rotary(q.astype(f32).reshape(bsz, seq_len, RET_HEADS, RET_DK), pos)
    kh = rotary(k.astype(f32).reshape(bsz, seq_len, RET_HEADS, RET_DK), pos) * (RET_DK ** -0.5)
    vh = v.astype(f32).reshape(bsz, seq_len, RET_HEADS, RET_DV)
    log_g = jnp.log(1.0 - 2.0 ** (-5.0 - jnp.arange(RET_HEADS, dtype=f32)))
    idx = jnp.arange(cl, dtype=f32)
    inner_decay = jnp.exp(log_g[:, None, None] * jnp.abs(idx[:, None] - idx[None, :]))
    qc = qh.reshape(bsz, nc, cl, RET_HEADS, RET_DK)
    kc = kh.reshape(bsz, nc, cl, RET_HEADS, RET_DK)
    vc = vh.reshape(bsz, nc, cl, RET_HEADS, RET_DV)
    scores = jnp.einsum('bnchd,bnmhd->bnhcm', qc, kc) * inner_decay
    inner = jnp.einsum('bnhcm,bnmhe->bnche', scores, vc)
    zeta = jnp.exp(log_g[:, None] * (cl - 1.0 - idx)[None, :])
    kv = jnp.einsum('bnmhd,bnmhe,hm->nbhde', kc, vc, zeta)
    chunk_decay = jnp.exp(log_g * cl)[None, :, None, None]
    init = jnp.zeros((bsz, RET_HEADS, RET_DK, RET_DV), f32) if s0 is None else s0.astype(f32)

    def step(s, kv_n):
        return chunk_decay * s + kv_n, s

    s_last, s_prev = lax.scan(step, init, kv)
    xi = jnp.exp(log_g[:, None] * (idx + 1.0)[None, :])
    cross = jnp.einsum('bnchd,nbhde,hc->bnche', qc, s_prev, xi)
    o = (inner + cross).reshape(bsz, seq_len, RET_HEADS, RET_DV)
    o = o * lax.rsqrt(jnp.mean(o * o, axis=-1, keepdims=True) + EPS)
    return o.reshape(bsz, seq_len, RET_V), s_last


def hmoe(x, w_rg, b_rg, w_re, b_re, w1, w3, w2):
    f32 = jnp.float32
    bsz, seq_len, dm = x.shape
    t = x.reshape(bsz * seq_len, dm)
    tf = t.astype(f32)
    g_logits = tf @ w_rg.astype(f32) + b_rg.astype(f32)
    g_prob = jax.nn.softmax(g_logits, axis=-1)
    _, g_idx = lax.top_k(g_logits, 1)
    g_w = jnp.take_along_axis(g_prob, g_idx, axis=-1)
    g_onehot = jax.nn.one_hot(g_idx[:, 0], MOE_GROUPS, dtype=f32)
    e_all = jnp.einsum('td,gde->tge', tf, w_re.astype(f32)) + b_re.astype(f32)
    e_logits = jnp.einsum('tge,tg->te', e_all, g_onehot)
    e_top, e_idx = lax.top_k(e_logits, MOE_TOPK)
    e_w = jax.nn.softmax(e_top, axis=-1)
    e_dense = jnp.einsum('tk,tke->te', e_w, jax.nn.one_hot(e_idx, MOE_EXPERTS, dtype=f32))
    gate = (g_w * g_onehot)[:, :, None] * e_dense[:, None, :]
    out = jnp.zeros((bsz * seq_len, dm), f32)
    for gi in range(MOE_GROUPS):
        h = jax.nn.silu(jnp.einsum('td,edf->tef', t, w1[gi])) * jnp.einsum('td,edf->tef', t, w3[gi])
        out = out + jnp.einsum('tef,efd->td', h * gate[:, gi, :, None].astype(h.dtype), w2[gi])
    return out.reshape(bsz, seq_len, dm)


def trunk_layer(x, c, s5_re0, s5_im0, ret0, offset, p):
    mod = jax.nn.silu(c) @ p['w_ada'] + p['b_ada']
    sh1, sc1, gt1, sh2, sc2, gt2 = jnp.split(mod, 6, axis=-1)
    n = modulate(rmsnorm(x, p['norm1']), sh1, sc1)
    proj = n @ p['w_in']
    offsets = np.cumsum(IN_SPLITS)[:-1].tolist()
    u, q, k, v, g, gate_a, gate_b = jnp.split(proj, offsets, axis=-1)
    y_a, s5_re, s5_im = s5_mixer(u, s5_re0, s5_im0, p['s5_a_re'], p['s5_a_im'], p['s5_log_dt'],
                                 p['s5_b_re'], p['s5_b_im'], p['s5_c_re'], p['s5_c_im'],
                                 p['s5_d'], p['s5_w_glu'], p['s5_b_glu'])
    y_b, ret_s = retention_mixer(q, k, v, ret0, offset)
    y_a = y_a @ p['w_s5_out']
    y_b = (y_b * jax.nn.silu(g)) @ p['w_ret_out']
    merged = jax.nn.sigmoid(gate_a) * y_a + jax.nn.sigmoid(gate_b) * y_b
    x = x + gt1[:, None, :] * (merged @ p['w_out'])
    n2 = modulate(rmsnorm(x, p['norm2']), sh2, sc2)
    x = x + gt2[:, None, :] * hmoe(n2, p['w_rg'], p['b_rg'], p['w_re'], p['b_re'],
                                   p['w1'], p['w3'], p['w2'])
    return x, s5_re, s5_im, ret_s


def setup_inputs(seed: int = 0) -> dict:
    key = jax.random.key(seed)
    ks = jax.random.split(key, 40)
    f32 = jnp.float32

    def nrm(i, shape, scale):
        return scale * jax.random.normal(ks[i], shape, f32)

    G, P, J = S5_GROUPS, S5_STATE, S5_GROUP
    a_im_base = jnp.pi * jnp.arange(P, dtype=f32)
    return {
        'x_prompt': nrm(0, (BATCH, SEQ, D_MODEL), 1.0),
        'x_sample': nrm(1, (DEC_BATCH, DEC_SEQ, D_MODEL), 1.0),
        'state_s5_re': nrm(2, (DEPTH, DEC_BATCH, G, P), 0.3),
        'state_s5_im': nrm(3, (DEPTH, DEC_BATCH, G, P), 0.3),
        'state_ret': nrm(4, (DEPTH, DEC_BATCH, RET_HEADS, RET_DK, RET_DV), 1.0),
        'c_prompt': nrm(5, (BATCH, D_MODEL), 1.0),
        'c_sample': nrm(6, (DEC_BATCH, D_MODEL), 1.0),
        'w_ada': nrm(7, (DEPTH, D_MODEL, 6 * D_MODEL), 0.5 * D_MODEL ** -0.5),
        'b_ada': nrm(8, (DEPTH, 6 * D_MODEL), 0.02),
        'norm1': 1.0 + nrm(9, (DEPTH, D_MODEL), 0.02),
        'norm2': 1.0 + nrm(10, (DEPTH, D_MODEL), 0.02),
        'w_in': nrm(11, (DEPTH, D_MODEL, IN_WIDTH), D_MODEL ** -0.5),
        's5_a_re': -0.5 + nrm(12, (DEPTH, G, P), 0.01),
        's5_a_im': a_im_base + nrm(13, (DEPTH, G, P), 0.01),
        's5_log_dt': jax.random.uniform(ks[14], (DEPTH, G), f32, math.log(0.001), math.log(0.1)),
        's5_b_re': nrm(15, (DEPTH, G, P, J), (2.0 * J) ** -0.5),
        's5_b_im': nrm(16, (DEPTH, G, P, J), (2.0 * J) ** -0.5),
        's5_c_re': nrm(17, (DEPTH, G, J, P), P ** -0.5),
        's5_c_im': nrm(18, (DEPTH, G, J, P), P ** -0.5),
        's5_d': nrm(19, (DEPTH, S5_WIDTH), 0.5),
        's5_w_glu': nrm(20, (DEPTH, S5_WIDTH, S5_WIDTH), S5_WIDTH ** -0.5),
        's5_b_glu': nrm(21, (DEPTH, S5_WIDTH), 0.02),
        'w_s5_out': nrm(22, (DEPTH, S5_WIDTH, D_MODEL), S5_WIDTH ** -0.5),
        'w_ret_out': nrm(23, (DEPTH, RET_V, D_MODEL), RET_V ** -0.5),
        'w_out': nrm(24, (DEPTH, D_MODEL, D_MODEL), D_MODEL ** -0.5),
        'w_rg': nrm(25, (DEPTH, D_MODEL, MOE_GROUPS), D_MODEL ** -0.5),
        'b_rg': nrm(26, (DEPTH, MOE_GROUPS), 0.01),
        'w_re': nrm(27, (DEPTH, MOE_GROUPS, D_MODEL, MOE_EXPERTS), D_MODEL ** -0.5),
        'b_re': nrm(28, (DEPTH, MOE_GROUPS, MOE_EXPERTS), 0.01),
        'w1': nrm(29, (DEPTH, MOE_GROUPS, MOE_EXPERTS, D_MODEL, MOE_FF), D_MODEL ** -0.5),
        'w3': nrm(30, (DEPTH, MOE_GROUPS, MOE_EXPERTS, D_MODEL, MOE_FF), D_MODEL ** -0.5),
        'w2': nrm(31, (DEPTH, MOE_GROUPS, MOE_EXPERTS, MOE_FF, D_MODEL), MOE_FF ** -0.5),
        'final_norm': 1.0 + nrm(32, (D_MODEL,), 0.02),
    }


def reference(x_prompt, x_sample, state_s5_re, state_s5_im, state_ret, c_prompt, c_sample,
              w_ada, b_ada, norm1, norm2, w_in, s5_a_re, s5_a_im, s5_log_dt, s5_b_re, s5_b_im,
              s5_c_re, s5_c_im, s5_d, s5_w_glu, s5_b_glu, w_s5_out, w_ret_out, w_out,
              w_rg, b_rg, w_re, b_re, w1, w3, w2, final_norm):
    hp = x_prompt
    hs = x_sample
    p_re, p_im, p_ret = [], [], []
    s_re, s_im, s_ret = [], [], []
    for l in range(DEPTH):
        p = dict(w_ada=w_ada[l], b_ada=b_ada[l], norm1=norm1[l], norm2=norm2[l], w_in=w_in[l],
                 s5_a_re=s5_a_re[l], s5_a_im=s5_a_im[l], s5_log_dt=s5_log_dt[l],
                 s5_b_re=s5_b_re[l], s5_b_im=s5_b_im[l], s5_c_re=s5_c_re[l], s5_c_im=s5_c_im[l],
                 s5_d=s5_d[l], s5_w_glu=s5_w_glu[l], s5_b_glu=s5_b_glu[l],
                 w_s5_out=w_s5_out[l], w_ret_out=w_ret_out[l], w_out=w_out[l],
                 w_rg=w_rg[l], b_rg=b_rg[l], w_re=w_re[l], b_re=b_re[l],
                 w1=w1[l], w3=w3[l], w2=w2[l])
        hp, r_re, r_im, r_ret = trunk_layer(hp, c_prompt, None, None, None, 0, p)
        hs, q_re, q_im, q_ret = trunk_layer(hs, c_sample, state_s5_re[l], state_s5_im[l],
                                            state_ret[l], PAST_LEN, p)
        p_re.append(r_re)
        p_im.append(r_im)
        p_ret.append(r_ret)
        s_re.append(q_re)
        s_im.append(q_im)
        s_ret.append(q_ret)
    y_prompt = rmsnorm(hp, final_norm)
    y_sample = rmsnorm(hs, final_norm)
    return (y_prompt, y_sample, jnp.stack(p_re), jnp.stack(p_im), jnp.stack(p_ret),
            jnp.stack(s_re), jnp.stack(s_im), jnp.stack(s_ret))
```

```python
import functools
import math

import jax
import jax.numpy as jnp
from jax import lax
from jax.experimental import pallas as pl
from jax.experimental.pallas import tpu as pltpu

F32 = jnp.float32
BF16 = jnp.bfloat16

D_MODEL = 1024
PAST_LEN = 2048
CHUNK = 64
S5_WIDTH = 512
S5_GROUP = 16
S5_GROUPS = 32
S5_STATE = 64
S5_LANES = 2 * S5_GROUPS * S5_STATE
S5_CHUNKS = 4
RET_HEADS = 4
RET_DK = 128
RET_DV = 256
RET_QK = RET_HEADS * RET_DK
RET_V = RET_HEADS * RET_DV
ROPE_BASE = 10000.0
MOE_GROUPS = 4
MOE_EXPERTS = 8
N_EXPERTS = MOE_GROUPS * MOE_EXPERTS
MOE_FF = 256
EPS = 1e-6
IN_WIDTH = S5_WIDTH + 2 * RET_QK + 2 * RET_V + 2 * D_MODEL
ROUTE_ROWS = 8 * (1 + MOE_GROUPS)

BATCH_GROUP = 8
TOKEN_TILE = 512
S5_TIME_TILE = 64
RET_BLOCK = 256
EXPERT_TILE = 512
COMBINE_TILE = 256
VMEM_LIMIT = 56 * 1024 * 1024
LANE = 128


def _cparams(*sem):
    return pltpu.CompilerParams(dimension_semantics=sem, vmem_limit_bytes=VMEM_LIMIT)


def _bdot(a, b):
    return jnp.dot(a, b, preferred_element_type=F32)


def _sigmoid(x):
    return 1.0 / (1.0 + jnp.exp(-x))


def _mod_kernel(c_ref, w_ref, b_ref, o_ref):
    c = c_ref[...]
    a = (c * _sigmoid(c)).astype(BF16)
    o_ref[...] = _bdot(a, w_ref[...].astype(BF16)) + b_ref[...]


def _mod(c, w_ada, b_ada):
    n = c.shape[0]
    return pl.pallas_call(
        _mod_kernel,
        grid=(6,),
        in_specs=[pl.BlockSpec((n, D_MODEL), lambda j: (0, 0)),
                  pl.BlockSpec((D_MODEL, D_MODEL), lambda j: (0, j)),
                  pl.BlockSpec((1, D_MODEL), lambda j: (0, j))],
        out_specs=pl.BlockSpec((n, D_MODEL), lambda j: (0, j)),
        out_shape=jax.ShapeDtypeStruct((n, 6 * D_MODEL), F32),
        compiler_params=_cparams("parallel"),
        name="mod",
    )(c, w_ada, b_ada.reshape(1, -1))


def _rope(x, cos2, sin2):
    return x * cos2 + pltpu.roll(x, RET_DK // 2, 1) * sin2


def _inproj_kernel(x_ref, sh_ref, sc_ref, g_ref, w_ref, cos_ref, sin_ref,
                   u_ref, q_ref, k_ref, v_ref, gs_ref, ga_ref, gb_ref):
    x = x_ref[...]
    n = x * lax.rsqrt(jnp.mean(x * x, axis=-1, keepdims=True) + EPS) * g_ref[...]
    nb = (n * (1.0 + sc_ref[...]) + sh_ref[...]).astype(BF16)
    cos2 = cos_ref[...]
    sin2 = sin_ref[...]
    o = 0
    u_ref[...] = _bdot(nb, w_ref[:, o:o + S5_WIDTH]).astype(BF16)
    o += S5_WIDTH
    for h in range(RET_HEADS):
        qh = _bdot(nb, w_ref[:, o + h * RET_DK:o + (h + 1) * RET_DK])
        q_ref[:, h * RET_DK:(h + 1) * RET_DK] = _rope(qh, cos2, sin2).astype(BF16)
    o += RET_QK
    for h in range(RET_HEADS):
        kh = _bdot(nb, w_ref[:, o + h * RET_DK:o + (h + 1) * RET_DK])
        k_ref[:, h * RET_DK:(h + 1) * RET_DK] = (_rope(kh, cos2, sin2) * (RET_DK ** -0.5)).astype(BF16)
    o += RET_QK
    for ref in (v_ref, gs_ref, ga_ref, gb_ref):
        ref[...] = _bdot(nb, w_ref[:, o:o + D_MODEL]).astype(BF16)
        o += D_MODEL


def _inproj(x, shift, scale, g1, w_in_b, cos2, sin2):
    bsz, seq, _ = x.shape
    tl = min(TOKEN_TILE, seq)
    row = lambda w: pl.BlockSpec((None, tl, w), lambda b, t: (b, t, 0))
    vec = pl.BlockSpec((None, 1, D_MODEL), lambda b, t: (b, 0, 0))
    shapes = [S5_WIDTH, RET_QK, RET_QK, RET_V, RET_V, D_MODEL, D_MODEL]
    return pl.pallas_call(
        _inproj_kernel,
        grid=(bsz, seq // tl),
        in_specs=[row(D_MODEL), vec, vec,
                  pl.BlockSpec((1, D_MODEL), lambda b, t: (0, 0)),
                  pl.BlockSpec((D_MODEL, IN_WIDTH), lambda b, t: (0, 0), pipeline_mode=pl.Buffered(1)),
                  pl.BlockSpec((tl, RET_DK), lambda b, t: (t, 0)),
                  pl.BlockSpec((tl, RET_DK), lambda b, t: (t, 0))],
        out_specs=[row(w) for w in shapes],
        out_shape=[jax.ShapeDtypeStruct((bsz, seq, w), BF16) for w in shapes],
        compiler_params=_cparams("parallel", "parallel"),
        name="inproj",
    )(x, shift, scale, g1, w_in_b, cos2, sin2)


def _gelu_tanh(y):
    return 0.5 * y * (1.0 + jnp.tanh(math.sqrt(2.0 / math.pi) * (y + 0.044715 * (y * y * y))))


def _s5_kernel(u_ref, h0_ref, a_ref, bm_ref, cm_ref, d_ref, wg_ref, bg_ref, wo_ref,
               ya_ref, ht_ref, bu_ref, hs_ref, *, tt):
    ti = pl.program_id(1)
    rows = BATCH_GROUP * tt
    half = S5_LANES // (2 * S5_CHUNKS)

    @pl.when(ti == 0)
    def _():
        hs_ref[...] = h0_ref[...]

    u2 = u_ref[...].reshape(rows, S5_WIDTH)
    kc = S5_WIDTH // S5_CHUNKS
    vpc = 2 * half // LANE
    for c in range(S5_CHUNKS):
        bu = _bdot(u2[:, c * kc:(c + 1) * kc], bm_ref[c])
        for j in range(vpc):
            bu_ref[c * vpc + j] = bu[:, j * LANE:(j + 1) * LANE]

    nv = vpc // 2
    for c in range(S5_CHUNKS):
        lre = pl.ds(c * 2 * half, half)
        lim = pl.ds(c * 2 * half + half, half)
        are = a_ref[:, lre]
        aim = a_ref[:, lim]

        def step(t, carry, c=c, are=are, aim=aim):
            hre, him = carry
            rsel = pl.ds(t, BATCH_GROUP, stride=tt)
            bre = jnp.concatenate([bu_ref[c * vpc + j, rsel, :] for j in range(nv)], axis=1)
            bim = jnp.concatenate([bu_ref[c * vpc + nv + j, rsel, :] for j in range(nv)], axis=1)
            nre = are * hre - aim * him + bre
            nim = are * him + aim * hre + bim
            for j in range(nv):
                bu_ref[c * vpc + j, rsel, :] = nre[:, j * LANE:(j + 1) * LANE]
                bu_ref[c * vpc + nv + j, rsel, :] = nim[:, j * LANE:(j + 1) * LANE]
            return nre, nim

        hre, him = lax.fori_loop(0, tt, step, (hs_ref[:, lre], hs_ref[:, lim]))
        hs_ref[:, lre] = hre
        hs_ref[:, lim] = him

    ys = []
    for c in range(S5_CHUNKS):
        xs = jnp.concatenate([bu_ref[c * vpc + j].astype(BF16) for j in range(vpc)], axis=1)
        ys.append(_bdot(xs, cm_ref[c]))
    y = jnp.concatenate(ys, axis=1) + d_ref[...] * u2.astype(F32)
    z = _gelu_tanh(y)
    gl = _bdot(z.astype(BF16), wg_ref[...]) + bg_ref[...]
    o = (z * _sigmoid(gl)).astype(BF16)
    ya_ref[...] = _bdot(o, wo_ref[...]).reshape(BATCH_GROUP, tt, D_MODEL).astype(BF16)

    @pl.when(ti == pl.num_programs(1) - 1)
    def _():
        ht_ref[...] = hs_ref[...]


def _s5(u, h0, a_lanes, bm, cm, d, wg, bg, wo):
    bsz, seq, _ = u.shape
    tt = min(S5_TIME_TILE, seq)
    const = lambda shape: pl.BlockSpec(shape, lambda b, t: (0,) * len(shape))
    return pl.pallas_call(
        functools.partial(_s5_kernel, tt=tt),
        grid=(bsz // BATCH_GROUP, seq // tt),
        in_specs=[pl.BlockSpec((BATCH_GROUP, tt, S5_WIDTH), lambda b, t: (b, t, 0)),
                  pl.BlockSpec((BATCH_GROUP, S5_LANES), lambda b, t: (b, 0)),
                  const((BATCH_GROUP, S5_LANES)),
                  const(bm.shape), const(cm.shape), const((1, S5_WIDTH)),
                  const((S5_WIDTH, S5_WIDTH)), const((1, S5_WIDTH)), const((S5_WIDTH, D_MODEL))],
        out_specs=[pl.BlockSpec((BATCH_GROUP, tt, D_MODEL), lambda b, t: (b, t, 0)),
                   pl.BlockSpec((BATCH_GROUP, S5_LANES), lambda b, t: (b, 0))],
        out_shape=[jax.ShapeDtypeStruct((bsz, seq, D_MODEL), BF16),
                   jax.ShapeDtypeStruct((bsz, S5_LANES), F32)],
        scratch_shapes=[pltpu.VMEM((S5_LANES // LANE, BATCH_GROUP * tt, LANE), F32),
                        pltpu.VMEM((BATCH_GROUP, S5_LANES), F32)],
        compiler_params=_cparams("parallel", "arbitrary"),
        name="s5",
    )(u, h0, a_lanes, bm, cm, d, wg, bg, wo)


def _s5_params(a_re, a_im, log_dt, b_re, b_im, c_re, c_im):
    a_re = a_re.astype(F32)
    a_im = a_im.astype(F32)
    dt = jnp.exp(log_dt.astype(F32))[:, None]
    mag = jnp.exp(a_re * dt)
    ang = a_im * dt
    ab_re = mag * jnp.cos(ang)
    ab_im = mag * jnp.sin(ang)
    den = a_re * a_re + a_im * a_im
    nr = ab_re - 1.0
    ni = ab_im
    f_re = (nr * a_re + ni * a_im) / den
    f_im = (ni * a_re - nr * a_im) / den
    b_re = b_re.astype(F32)
    b_im = b_im.astype(F32)
    bb_re = f_re[..., None] * b_re - f_im[..., None] * b_im
    bb_im = f_re[..., None] * b_im + f_im[..., None] * b_re
    gpc = S5_GROUPS // S5_CHUNKS
    eye = jnp.eye(gpc, dtype=F32)

    def lanes(x):
        return x.reshape(S5_CHUNKS, gpc * S5_STATE)

    a_lanes = jnp.concatenate([lanes(ab_re), lanes(ab_im)], axis=1).reshape(1, S5_LANES)
    a_lanes = jnp.broadcast_to(a_lanes, (BATCH_GROUP, S5_LANES))

    def in_blocks(bb):
        bb = bb.reshape(S5_CHUNKS, gpc, S5_STATE, S5_GROUP)
        return jnp.einsum('cgpj,gh->cgjhp', bb, eye).reshape(S5_CHUNKS, gpc * S5_GROUP, gpc * S5_STATE)

    bm = jnp.concatenate([in_blocks(bb_re), in_blocks(bb_im)], axis=2).astype(BF16)

    def out_blocks(cc):
        cc = cc.astype(F32).reshape(S5_CHUNKS, gpc, S5_GROUP, S5_STATE)
        return jnp.einsum('cgjp,gh->cgphj', cc, eye).reshape(S5_CHUNKS, gpc * S5_STATE, gpc * S5_GROUP)

    cm = jnp.concatenate([out_blocks(c_re), -out_blocks(c_im)], axis=1).astype(BF16)
    return a_lanes, bm, cm


def _s5_state_to_lanes(h_re, h_im):
    bsz = h_re.shape[0]
    re = h_re.astype(F32).reshape(bsz, S5_CHUNKS, -1)
    im = h_im.astype(F32).reshape(bsz, S5_CHUNKS, -1)
    return jnp.concatenate([re, im], axis=2).reshape(bsz, S5_LANES)


def _s5_state_from_lanes(h):
    bsz = h.shape[0]
    h = h.reshape(bsz, S5_CHUNKS, 2, S5_GROUPS // S5_CHUNKS, S5_STATE)
    return (h[:, :, 0].reshape(bsz, S5_GROUPS, S5_STATE), h[:, :, 1].reshape(bsz, S5_GROUPS, S5_STATE))


def _ret_kernel(q_ref, k_ref, v_ref, g_ref, s0_ref, dm_ref, xi_ref, zeta_ref, wo_ref,
                yb_ref, st_ref, s_ref, *, block_decay):
    si = pl.program_id(1)

    @pl.when(si == 0)
    def _():
        s_ref[...] = s0_ref[...]

    gated = []
    for h in range(RET_HEADS):
        qh = q_ref[:, h * RET_DK:(h + 1) * RET_DK]
        kh = k_ref[:, h * RET_DK:(h + 1) * RET_DK]
        vh = v_ref[:, h * RET_DV:(h + 1) * RET_DV]
        scores = lax.dot_general(qh, kh, (((1,), (1,)), ((), ())), preferred_element_type=F32) * dm_ref[h]
        state = s_ref[h]
        o = _bdot(scores.astype(BF16), vh) + _bdot(qh, state.astype(BF16)) * xi_ref[h]
        o = o * lax.rsqrt(jnp.mean(o * o, axis=-1, keepdims=True) + EPS)
        gh = g_ref[:, h * RET_DV:(h + 1) * RET_DV].astype(F32)
        gated.append((o * (gh * _sigmoid(gh))).astype(BF16))
        kz = (kh.astype(F32) * zeta_ref[h]).astype(BF16)
        kv = lax.dot_general(kz, vh, (((0,), (0,)), ((), ())), preferred_element_type=F32)
        s_ref[h] = block_decay[h] * state + kv
    yb_ref[...] = _bdot(jnp.concatenate(gated, axis=1), wo_ref[...]).astype(BF16)

    @pl.when(si == pl.num_programs(1) - 1)
    def _():
        st_ref[...] = s_ref[...]


def _ret_tables(seq):
    cl = min(CHUNK, seq)
    blk = min(RET_BLOCK, seq)
    log_g = jnp.log(1.0 - 2.0 ** (-5.0 - jnp.arange(RET_HEADS, dtype=F32)))
    idx = jnp.arange(blk, dtype=F32)
    diff = idx[:, None] - idx[None, :]
    cn = jnp.arange(blk)[:, None] // cl
    cm = jnp.arange(blk)[None, :] // cl
    expo = jnp.where(cm == cn, jnp.abs(diff), diff)
    dm = jnp.where(cm <= cn, jnp.exp(log_g[:, None, None] * expo[None]), 0.0)
    xi = jnp.exp(log_g[:, None] * (idx + 1.0)[None, :])[..., None]
    zeta = jnp.exp(log_g[:, None] * (blk - 1.0 - idx)[None, :])[..., None]
    block_decay = tuple(math.exp(math.log(1.0 - 2.0 ** (-5.0 - h)) * blk) for h in range(RET_HEADS))
    return blk, dm, xi, zeta, block_decay


def _retention(q, k, v, g, s0, w_ret_out_b):
    bsz, seq, _ = q.shape
    blk, dm, xi, zeta, block_decay = _ret_tables(seq)
    row = lambda w: pl.BlockSpec((None, blk, w), lambda b, s: (b, s, 0))
    const = lambda shape: pl.BlockSpec(shape, lambda b, s: (0,) * len(shape))
    state = pl.BlockSpec((None, RET_HEADS, RET_DK, RET_DV), lambda b, s: (b, 0, 0, 0))
    return pl.pallas_call(
        functools.partial(_ret_kernel, block_decay=block_decay),
        grid=(bsz, seq // blk),
        in_specs=[row(RET_QK), row(RET_QK), row(RET_V), row(RET_V), state,
                  const(dm.shape), const(xi.shape), const(zeta.shape), const((RET_V, D_MODEL))],
        out_specs=[row(D_MODEL), state],
        out_shape=[jax.ShapeDtypeStruct((bsz, seq, D_MODEL), BF16),
                   jax.ShapeDtypeStruct((bsz, RET_HEADS, RET_DK, RET_DV), F32)],
        scratch_shapes=[pltpu.VMEM((RET_HEADS, RET_DK, RET_DV), F32)],
        compiler_params=_cparams("parallel", "arbitrary"),
        name="retention",
    )(q, k, v, g, s0, dm, xi, zeta, w_ret_out_b)


def _merge_kernel(x_ref, ya_ref, yb_ref, ga_ref, gb_ref, gt_ref, sh_ref, sc_ref, g2_ref, wo_ref,
                  wr_ref, br_ref, x1_ref, n2_ref, re_ref, rw_ref):
    merged = (_sigmoid(ga_ref[...].astype(F32)) * ya_ref[...].astype(F32)
              + _sigmoid(gb_ref[...].astype(F32)) * yb_ref[...].astype(F32))
    x1 = x_ref[...] + gt_ref[...] * _bdot(merged.astype(BF16), wo_ref[...])
    x1_ref[...] = x1
    n2 = x1 * lax.rsqrt(jnp.mean(x1 * x1, axis=-1, keepdims=True) + EPS) * g2_ref[...]
    n2 = n2 * (1.0 + sc_ref[...]) + sh_ref[...]
    n2_ref[...] = n2

    lt = lax.dot_general(wr_ref[...], n2, (((1,), (1,)), ((), ())),
                         precision=lax.Precision.HIGHEST, preferred_element_type=F32) + br_ref[...]
    tl = lt.shape[1]
    iota = lax.broadcasted_iota(jnp.int32, (8, tl), 0)
    gl = lt[0:8]
    gmax = jnp.max(gl, axis=0, keepdims=True)
    gi = jnp.min(jnp.where(gl == gmax, iota, 8), axis=0, keepdims=True)
    gw = 1.0 / jnp.sum(jnp.exp(gl - gmax), axis=0, keepdims=True)
    el = jnp.zeros((8, tl), F32)
    for g in range(MOE_GROUPS):
        el = jnp.where(gi == g, lt[8 * (g + 1):8 * (g + 2)], el)
    m1 = jnp.max(el, axis=0, keepdims=True)
    i1 = jnp.min(jnp.where(el == m1, iota, 8), axis=0, keepdims=True)
    el2 = jnp.where(iota == i1, -jnp.inf, el)
    m2 = jnp.max(el2, axis=0, keepdims=True)
    i2 = jnp.min(jnp.where(el2 == m2, iota, 8), axis=0, keepdims=True)
    e21 = jnp.exp(m2 - m1)
    w1 = gw / (1.0 + e21)
    w2 = w1 * e21
    re_ref[...] = jnp.where(iota == 0, gi * MOE_EXPERTS + i1, jnp.where(iota == 1, gi * MOE_EXPERTS + i2, 0))
    rw_ref[...] = jnp.where(iota == 0, w1, jnp.where(iota == 1, w2, 0.0))


def _merge(x, ya, yb, ga, gb, gate1, shift2, scale2, g2, w_out_b, wr, br):
    bsz, seq, _ = x.shape
    tl = min(TOKEN_TILE, seq)
    row = pl.BlockSpec((None, tl, D_MODEL), lambda b, t: (b, t, 0))
    vec = pl.BlockSpec((None, 1, D_MODEL), lambda b, t: (b, 0, 0))
    const = lambda shape: pl.BlockSpec(shape, lambda b, t: (0,) * len(shape))
    route = pl.BlockSpec((None, 8, tl), lambda b, t: (b, 0, t))
    return pl.pallas_call(
        _merge_kernel,
        grid=(bsz, seq // tl),
        in_specs=[row, row, row, row, row, vec, vec, vec, const((1, D_MODEL)),
                  const((D_MODEL, D_MODEL)), const((ROUTE_ROWS, D_MODEL)), const((ROUTE_ROWS, 1))],
        out_specs=[row, row, route, route],
        out_shape=[jax.ShapeDtypeStruct((bsz, seq, D_MODEL), F32),
                   jax.ShapeDtypeStruct((bsz, seq, D_MODEL), F32),
                   jax.ShapeDtypeStruct((bsz, 8, seq), jnp.int32),
                   jax.ShapeDtypeStruct((bsz, 8, seq), F32)],
        compiler_params=_cparams("parallel", "parallel"),
        name="merge",
    )(x, ya, yb, ga, gb, gate1, shift2, scale2, g2, w_out_b, wr, br)


def _router_weights(w_rg, b_rg, w_re, b_re):
    wr = jnp.zeros((ROUTE_ROWS, D_MODEL), F32)
    wr = wr.at[0:MOE_GROUPS].set(w_rg.astype(F32).T)
    wr = wr.at[8:].set(jnp.transpose(w_re.astype(F32), (0, 2, 1)).reshape(N_EXPERTS, D_MODEL))
    br = jnp.full((ROUTE_ROWS,), -1e30, F32)
    br = br.at[0:MOE_GROUPS].set(b_rg.astype(F32))
    br = br.at[8:].set(b_re.astype(F32).reshape(N_EXPERTS))
    return wr, br.reshape(ROUTE_ROWS, 1)


def _row_gather(idx_ref, src_hbm, buf, sem, slot, n):
    def body(r, carry):
        pltpu.make_async_copy(src_hbm.at[pl.ds(idx_ref[0, r], 1), :],
                              buf.at[slot, pl.ds(r, 1), :], sem.at[slot]).start()
        return carry
    lax.fori_loop(0, n, body, 0)


def _row_gather_wait(src_hbm, buf, sem, slot, n):
    pltpu.make_async_copy(src_hbm.at[pl.ds(0, n), :], buf.at[slot], sem.at[slot]).wait()


def _expert_kernel(te_ref, nu_ref, cur_ref, nxt_ref, x_hbm, gate_ref, w1_ref, w3_ref, w2_ref,
                   y_ref, xbuf, sem, *, tm):
    i = pl.program_id(0)
    used = nu_ref[0]
    slot = i % 2

    @pl.when(jnp.logical_and(i == 0, used > 0))
    def _():
        _row_gather(cur_ref, x_hbm, xbuf, sem, 0, tm)

    @pl.when(i + 1 < used)
    def _():
        _row_gather(nxt_ref, x_hbm, xbuf, sem, 1 - slot, tm)

    @pl.when(i < used)
    def _():
        _row_gather_wait(x_hbm, xbuf, sem, slot, tm)
        xb = xbuf[slot].astype(BF16)
        a = _bdot(xb, w1_ref[...])
        h = a * _sigmoid(a) * _bdot(xb, w3_ref[...]) * gate_ref[...]
        y_ref[...] = _bdot(h.astype(BF16), w2_ref[...])

    @pl.when(i >= used)
    def _():
        y_ref[...] = jnp.zeros_like(y_ref)


def _experts(n2_flat, row_token, row_gate, tile_expert, n_used, w1b, w3b, w2b, tm):
    n_tiles = row_token.shape[0] // tm
    idx3 = row_token.reshape(n_tiles, 1, tm)
    last = n_tiles - 1
    grid_spec = pltpu.PrefetchScalarGridSpec(
        num_scalar_prefetch=2,
        grid=(n_tiles,),
        in_specs=[pl.BlockSpec((None, 1, tm), lambda i, te, nu: (i, 0, 0), memory_space=pltpu.SMEM),
                  pl.BlockSpec((None, 1, tm), lambda i, te, nu: (jnp.minimum(i + 1, last), 0, 0),
                               memory_space=pltpu.SMEM),
                  pl.BlockSpec(memory_space=pl.ANY),
                  pl.BlockSpec((tm, 1), lambda i, te, nu: (i, 0)),
                  pl.BlockSpec((None, D_MODEL, MOE_FF), lambda i, te, nu: (te[i], 0, 0)),
                  pl.BlockSpec((None, D_MODEL, MOE_FF), lambda i, te, nu: (te[i], 0, 0)),
                  pl.BlockSpec((None, MOE_FF, D_MODEL), lambda i, te, nu: (te[i], 0, 0))],
        out_specs=pl.BlockSpec((tm, D_MODEL), lambda i, te, nu: (i, 0)),
        scratch_shapes=[pltpu.VMEM((2, tm, D_MODEL), F32), pltpu.SemaphoreType.DMA((2,))],
    )
    return pl.pallas_call(
        functools.partial(_expert_kernel, tm=tm),
        grid_spec=grid_spec,
        out_shape=jax.ShapeDtypeStruct((n_tiles * tm, D_MODEL), F32),
        compiler_params=_cparams("arbitrary"),
        name="experts",
    )(tile_expert, n_used, idx3, idx3, n2_flat, row_gate, w1b, w3b, w2b)


def _dispatch_plan(route_e, route_w, tm):
    bsz, _, seq = route_e.shape
    n_tok = bsz * seq
    e_flat = jnp.transpose(route_e[:, 0:2, :], (0, 2, 1)).reshape(2 * n_tok)
    w_flat = jnp.transpose(route_w[:, 0:2, :], (0, 2, 1)).reshape(2 * n_tok)
    n_rows = 2 * n_tok + N_EXPERTS * tm
    order = jnp.argsort(e_flat, stable=True).astype(jnp.int32)
    e_sorted = e_flat[order]
    counts = jnp.sum((e_flat[:, None] == jnp.arange(N_EXPERTS)[None, :]).astype(jnp.int32), axis=0)
    padded = ((counts + tm - 1) // tm) * tm
    pad_end = jnp.cumsum(padded)
    start = jnp.cumsum(counts) - counts
    dest = (pad_end - padded)[e_sorted] + (jnp.arange(2 * n_tok, dtype=jnp.int32) - start[e_sorted])
    row_token = jnp.zeros((n_rows,), jnp.int32).at[dest].set(order // 2)
    row_gate = jnp.zeros((n_rows,), F32).at[dest].set(w_flat[order])
    pos = jnp.zeros((2 * n_tok,), jnp.int32).at[order].set(dest).reshape(n_tok, 2)
    tile_start = jnp.arange(n_rows // tm, dtype=jnp.int32) * tm
    tile_expert = jnp.minimum(jnp.searchsorted(pad_end, tile_start, side='right'), N_EXPERTS - 1).astype(jnp.int32)
    n_used = (pad_end[-1] // tm).astype(jnp.int32).reshape(1)
    return row_token, row_gate.reshape(n_rows, 1), pos, tile_expert, n_used


def _combine_kernel(cur_ref, nxt_ref, x1_ref, gt_ref, fn_ref, y_hbm, o_ref, ybuf, sem, *, tl):
    i = pl.program_id(0)
    slot = i % 2

    @pl.when(i == 0)
    def _():
        _row_gather(cur_ref, y_hbm, ybuf, sem, 0, 2 * tl)

    @pl.when(i + 1 < pl.num_programs(0))
    def _():
        _row_gather(nxt_ref, y_hbm, ybuf, sem, 1 - slot, 2 * tl)

    _row_gather_wait(y_hbm, ybuf, sem, slot, 2 * tl)
    moe = ybuf[slot, 0:tl, :] + ybuf[slot, tl:2 * tl, :]
    x2 = x1_ref[...] + gt_ref[...] * moe
    o_ref[...] = x2 * lax.rsqrt(jnp.mean(x2 * x2, axis=-1, keepdims=True) + EPS) * fn_ref[...]


def _combine(x1_flat, pos, y_rows, gate2, final_norm, seq):
    n_tok = x1_flat.shape[0]
    tl = min(COMBINE_TILE, seq)
    n_tiles = n_tok // tl
    idx3 = jnp.transpose(pos.reshape(n_tiles, tl, 2), (0, 2, 1)).reshape(n_tiles, 1, 2 * tl)
    last = n_tiles - 1
    return pl.pallas_call(
        functools.partial(_combine_kernel, tl=tl),
        grid=(n_tiles,),
        in_specs=[pl.BlockSpec((None, 1, 2 * tl), lambda i: (i, 0, 0), memory_space=pltpu.SMEM),
                  pl.BlockSpec((None, 1, 2 * tl), lambda i: (jnp.minimum(i + 1, last), 0, 0),
                               memory_space=pltpu.SMEM),
                  pl.BlockSpec((tl, D_MODEL), lambda i: (i, 0)),
                  pl.BlockSpec((None, 1, D_MODEL), lambda i: ((i * tl) // seq, 0, 0)),
                  pl.BlockSpec((1, D_MODEL), lambda i: (0, 0)),
                  pl.BlockSpec(memory_space=pl.ANY)],
        out_specs=pl.BlockSpec((tl, D_MODEL), lambda i: (i, 0)),
        out_shape=jax.ShapeDtypeStruct((n_tok, D_MODEL), F32),
        scratch_shapes=[pltpu.VMEM((2, 2 * tl, D_MODEL), F32), pltpu.SemaphoreType.DMA((2,))],
        compiler_params=_cparams("arbitrary"),
        name="combine",
    )(idx3, idx3, x1_flat, gate2, final_norm, y_rows)


def _rope_tables(seq, offset):
    half = RET_DK // 2
    theta = 1.0 / (ROPE_BASE ** jnp.linspace(0.0, 1.0, half, dtype=F32))
    pos = offset + jnp.arange(seq)
    ang = pos.astype(F32)[:, None] * theta[None, :]
    cos = jnp.cos(ang)
    sin = jnp.sin(ang)
    return jnp.concatenate([cos, cos], axis=1), jnp.concatenate([-sin, sin], axis=1)


def _layer(x, mod, h0, s0, offset, p, final_norm, expert_tile):
    bsz, seq, _ = x.shape
    sh1, sc1, gt1, sh2, sc2, gt2 = [m.reshape(bsz, 1, D_MODEL) for m in jnp.split(mod, 6, axis=-1)]
    cos2, sin2 = _rope_tables(seq, offset)
    u, q, k, v, g, ga, gb = _inproj(x, sh1, sc1, p['norm1'], p['w_in'], cos2, sin2)
    ya, h_t = _s5(u, h0, p['a_lanes'], p['bm'], p['cm'], p['d'], p['w_glu'], p['b_glu'], p['w_s5_out'])
    yb, s_t = _retention(q, k, v, g, s0, p['w_ret_out'])
    x1, n2, route_e, route_w = _merge(x, ya, yb, ga, gb, gt1, sh2, sc2, p['norm2'], p['w_out'],
                                      p['wr'], p['br'])
    row_token, row_gate, pos, tile_expert, n_used = _dispatch_plan(route_e, route_w, expert_tile)
    y_rows = _experts(n2.reshape(bsz * seq, D_MODEL), row_token, row_gate, tile_expert, n_used,
                      p['w1'], p['w3'], p['w2'], expert_tile)
    y = _combine(x1.reshape(bsz * seq, D_MODEL), pos, y_rows, gt2, final_norm, seq)
    return y.reshape(bsz, seq, D_MODEL), h_t, s_t


def kernel(x_prompt, x_sample, state_s5_re, state_s5_im, state_ret, c_prompt, c_sample, w_ada, b_ada, norm1, norm2, w_in, s5_a_re, s5_a_im, s5_log_dt, s5_b_re, s5_b_im, s5_c_re, s5_c_im, s5_d, s5_w_glu, s5_b_glu, w_s5_out, w_ret_out, w_out, w_rg, b_rg, w_re, b_re, w1, w3, w2, final_norm):
    depth = w_ada.shape[0]
    assert depth == 1
    bp = x_prompt.shape[0]
    bs, seq_s, _ = x_sample.shape
    l = 0
    a_lanes, bm, cm = _s5_params(s5_a_re[l], s5_a_im[l], s5_log_dt[l], s5_b_re[l], s5_b_im[l],
                                 s5_c_re[l], s5_c_im[l])
    wr, br = _router_weights(w_rg[l], b_rg[l], w_re[l], b_re[l])
    p = dict(
        norm1=norm1[l].astype(F32).reshape(1, D_MODEL), norm2=norm2[l].astype(F32).reshape(1, D_MODEL),
        w_in=w_in[l].astype(BF16), a_lanes=a_lanes, bm=bm, cm=cm,
        d=s5_d[l].astype(F32).reshape(1, S5_WIDTH), w_glu=s5_w_glu[l].astype(BF16),
        b_glu=s5_b_glu[l].astype(F32).reshape(1, S5_WIDTH), w_s5_out=w_s5_out[l].astype(BF16),
        w_ret_out=w_ret_out[l].astype(BF16), w_out=w_out[l].astype(BF16), wr=wr, br=br,
        w1=w1[l].astype(BF16).reshape(N_EXPERTS, D_MODEL, MOE_FF),
        w3=w3[l].astype(BF16).reshape(N_EXPERTS, D_MODEL, MOE_FF),
        w2=w2[l].astype(BF16).reshape(N_EXPERTS, MOE_FF, D_MODEL))
    fn = final_norm.astype(F32).reshape(1, D_MODEL)
    mod = _mod(jnp.concatenate([c_prompt, c_sample], axis=0).astype(F32), w_ada[l], b_ada[l])

    h0_p = jnp.zeros((bp, S5_LANES), F32)
    s0_p = jnp.zeros((bp, RET_HEADS, RET_DK, RET_DV), F32)
    y_p, h_p, s_p = _layer(x_prompt, mod[:bp], h0_p, s0_p, 0, p, fn, EXPERT_TILE)
    h0_s = _s5_state_to_lanes(state_s5_re[l], state_s5_im[l])
    y_s, h_s, s_s = _layer(x_sample, mod[bp:], h0_s, state_ret[l].astype(F32), PAST_LEN, p, fn,
                           min(EXPERT_TILE, 128))
    p_re, p_im = _s5_state_from_lanes(h_p)
    s_re, s_im = _s5_state_from_lanes(h_s)
    return (y_p, y_s, p_re[None], p_im[None], s_p[None], s_re[None], s_im[None], s_s[None])
```

```python
import functools
import math

import jax
import jax.numpy as jnp
from jax import lax
from jax.experimental import pallas as pl
from jax.experimental.pallas import tpu as pltpu

F32 = jnp.float32
BF16 = jnp.bfloat16

D_MODEL = 1024
PAST_LEN = 2048
CHUNK = 64
S5_WIDTH = 512
S5_GROUP = 16
S5_GROUPS = 32
S5_STATE = 64
S5_LANES = 2 * S5_GROUPS * S5_STATE
S5_CHUNKS = 4
RET_HEADS = 4
RET_DK = 128
RET_DV = 256
RET_QK = RET_HEADS * RET_DK
RET_V = RET_HEADS * RET_DV
ROPE_BASE = 10000.0
MOE_GROUPS = 4
MOE_EXPERTS = 8
N_EXPERTS = MOE_GROUPS * MOE_EXPERTS
MOE_FF = 256
EPS = 1e-6
IN_WIDTH = S5_WIDTH + 2 * RET_QK + 2 * RET_V + 2 * D_MODEL
ROUTE_ROWS = 8 * (1 + MOE_GROUPS)

BATCH_GROUP = 8
TOKEN_TILE = 512
S5_TIME_TILE = 64
RET_BLOCK = 256
EXPERT_TILE = 512
COMBINE_TILE = 256
VMEM_LIMIT = 56 * 1024 * 1024
LANE = 128
SUBLANE = 8
ROW_TILES = D_MODEL // LANE
SCAN_UNROLL = 8
DMA_UNROLL = 8


def _cparams(*sem):
    return pltpu.CompilerParams(dimension_semantics=sem, vmem_limit_bytes=VMEM_LIMIT)


def _bdot(a, b):
    return jnp.dot(a, b, preferred_element_type=F32)


def _sigmoid(x):
    return 1.0 / (1.0 + jnp.exp(-x))


def _mod_kernel(c_ref, w_ref, b_ref, o_ref):
    c = c_ref[...]
    a = (c * _sigmoid(c)).astype(BF16)
    o_ref[...] = _bdot(a, w_ref[...].astype(BF16)) + b_ref[...]


def _mod(c, w_ada, b_ada):
    n = c.shape[0]
    return pl.pallas_call(
        _mod_kernel,
        grid=(6,),
        in_specs=[pl.BlockSpec((n, D_MODEL), lambda j: (0, 0)),
                  pl.BlockSpec((D_MODEL, D_MODEL), lambda j: (0, j)),
                  pl.BlockSpec((1, D_MODEL), lambda j: (0, j))],
        out_specs=pl.BlockSpec((n, D_MODEL), lambda j: (0, j)),
        out_shape=jax.ShapeDtypeStruct((n, 6 * D_MODEL), F32),
        compiler_params=_cparams("parallel"),
        name="mod",
    )(c, w_ada, b_ada.reshape(1, -1))


def _rope(x, cos2, sin2):
    return x * cos2 + pltpu.roll(x, RET_DK // 2, 1) * sin2


def _inproj_kernel(x_ref, sh_ref, sc_ref, g_ref, w_ref, cos_ref, sin_ref,
                   u_ref, q_ref, k_ref, v_ref, gs_ref, ga_ref, gb_ref):
    x = x_ref[...]
    n = x * lax.rsqrt(jnp.mean(x * x, axis=-1, keepdims=True) + EPS) * g_ref[...]
    nb = (n * (1.0 + sc_ref[...]) + sh_ref[...]).astype(BF16)
    cos2 = cos_ref[...]
    sin2 = sin_ref[...]
    o = 0
    u_ref[...] = _bdot(nb, w_ref[:, o:o + S5_WIDTH]).astype(BF16)
    o += S5_WIDTH
    for h in range(RET_HEADS):
        qh = _bdot(nb, w_ref[:, o + h * RET_DK:o + (h + 1) * RET_DK])
        q_ref[:, h * RET_DK:(h + 1) * RET_DK] = _rope(qh, cos2, sin2).astype(BF16)
    o += RET_QK
    for h in range(RET_HEADS):
        kh = _bdot(nb, w_ref[:, o + h * RET_DK:o + (h + 1) * RET_DK])
        k_ref[:, h * RET_DK:(h + 1) * RET_DK] = (_rope(kh, cos2, sin2) * (RET_DK ** -0.5)).astype(BF16)
    o += RET_QK
    for ref in (v_ref, gs_ref, ga_ref, gb_ref):
        ref[...] = _bdot(nb, w_ref[:, o:o + D_MODEL]).astype(BF16)
        o += D_MODEL


def _inproj(x, shift, scale, g1, w_in_b, cos2, sin2):
    bsz, seq, _ = x.shape
    tl = min(TOKEN_TILE, seq)
    row = lambda w: pl.BlockSpec((None, tl, w), lambda b, t: (b, t, 0))
    vec = pl.BlockSpec((None, 1, D_MODEL), lambda b, t: (b, 0, 0))
    shapes = [S5_WIDTH, RET_QK, RET_QK, RET_V, RET_V, D_MODEL, D_MODEL]
    return pl.pallas_call(
        _inproj_kernel,
        grid=(bsz, seq // tl),
        in_specs=[row(D_MODEL), vec, vec,
                  pl.BlockSpec((1, D_MODEL), lambda b, t: (0, 0)),
                  pl.BlockSpec((D_MODEL, IN_WIDTH), lambda b, t: (0, 0), pipeline_mode=pl.Buffered(1)),
                  pl.BlockSpec((tl, RET_DK), lambda b, t: (t, 0)),
                  pl.BlockSpec((tl, RET_DK), lambda b, t: (t, 0))],
        out_specs=[row(w) for w in shapes],
        out_shape=[jax.ShapeDtypeStruct((bsz, seq, w), BF16) for w in shapes],
        compiler_params=_cparams("parallel", "parallel"),
        name="inproj",
    )(x, shift, scale, g1, w_in_b, cos2, sin2)


def _gelu_tanh(y):
    return 0.5 * y * (1.0 + jnp.tanh(math.sqrt(2.0 / math.pi) * (y + 0.044715 * (y * y * y))))


def _s5_kernel(u_ref, h0_ref, a_ref, pm_ref, pt_ref, bm_ref, cm_ref, d_ref, wg_ref, bg_ref, wo_ref,
               ya_ref, ht_ref, bu_ref, hs_ref, *, tt):
    ti = pl.program_id(1)
    rows = BATCH_GROUP * tt
    half = S5_LANES // (2 * S5_CHUNKS)

    @pl.when(ti == 0)
    def _():
        hs_ref[...] = h0_ref[...]

    u2 = _bdot(pm_ref[...], u_ref[...].reshape(rows, S5_WIDTH)).astype(BF16)
    kc = S5_WIDTH // S5_CHUNKS
    for c in range(S5_CHUNKS):
        bu_ref[:, c * 2 * half:(c + 1) * 2 * half] = _bdot(u2[:, c * kc:(c + 1) * kc], bm_ref[c])

    for c in range(S5_CHUNKS):
        lre = pl.ds(c * 2 * half, half)
        lim = pl.ds(c * 2 * half + half, half)
        are = a_ref[:, lre]
        aim = a_ref[:, lim]

        def step(t, carry, lre=lre, lim=lim, are=are, aim=aim):
            hre, him = carry
            rsel = pl.ds(pl.multiple_of(t * BATCH_GROUP, BATCH_GROUP), BATCH_GROUP)
            nre = are * hre - aim * him + bu_ref[rsel, lre]
            nim = are * him + aim * hre + bu_ref[rsel, lim]
            bu_ref[rsel, lre] = nre
            bu_ref[rsel, lim] = nim
            return nre, nim

        hre, him = lax.fori_loop(0, tt, step, (hs_ref[:, lre], hs_ref[:, lim]), unroll=SCAN_UNROLL)
        hs_ref[:, lre] = hre
        hs_ref[:, lim] = him

    ys = [_bdot(bu_ref[:, c * 2 * half:(c + 1) * 2 * half].astype(BF16), cm_ref[c])
          for c in range(S5_CHUNKS)]
    y = jnp.concatenate(ys, axis=1) + d_ref[...] * u2.astype(F32)
    z = _gelu_tanh(y)
    gl = _bdot(z.astype(BF16), wg_ref[...]) + bg_ref[...]
    o = (z * _sigmoid(gl)).astype(BF16)
    ob = _bdot(pt_ref[...], o).astype(BF16)
    ya_ref[...] = _bdot(ob, wo_ref[...]).reshape(BATCH_GROUP, tt, D_MODEL).astype(BF16)

    @pl.when(ti == pl.num_programs(1) - 1)
    def _():
        ht_ref[...] = hs_ref[...]


def _s5(u, h0, a_lanes, bm, cm, d, wg, bg, wo):
    bsz, seq, _ = u.shape
    tt = min(S5_TIME_TILE, seq)
    rows = BATCH_GROUP * tt
    const = lambda shape: pl.BlockSpec(shape, lambda b, t: (0,) * len(shape))
    r = jnp.arange(rows)
    perm = ((r[:, None] % BATCH_GROUP) * tt + r[:, None] // BATCH_GROUP == r[None, :]).astype(BF16)
    return pl.pallas_call(
        functools.partial(_s5_kernel, tt=tt),
        grid=(bsz // BATCH_GROUP, seq // tt),
        in_specs=[pl.BlockSpec((BATCH_GROUP, tt, S5_WIDTH), lambda b, t: (b, t, 0)),
                  pl.BlockSpec((BATCH_GROUP, S5_LANES), lambda b, t: (b, 0)),
                  const((BATCH_GROUP, S5_LANES)), const((rows, rows)), const((rows, rows)),
                  const(bm.shape), const(cm.shape), const((1, S5_WIDTH)),
                  const((S5_WIDTH, S5_WIDTH)), const((1, S5_WIDTH)), const((S5_WIDTH, D_MODEL))],
        out_specs=[pl.BlockSpec((BATCH_GROUP, tt, D_MODEL), lambda b, t: (b, t, 0)),
                   pl.BlockSpec((BATCH_GROUP, S5_LANES), lambda b, t: (b, 0))],
        out_shape=[jax.ShapeDtypeStruct((bsz, seq, D_MODEL), BF16),
                   jax.ShapeDtypeStruct((bsz, S5_LANES), F32)],
        scratch_shapes=[pltpu.VMEM((rows, S5_LANES), F32),
                        pltpu.VMEM((BATCH_GROUP, S5_LANES), F32)],
        compiler_params=_cparams("parallel", "arbitrary"),
        name="s5",
    )(u, h0, a_lanes, perm, perm.T, bm, cm, d, wg, bg, wo)


def _s5_params(a_re, a_im, log_dt, b_re, b_im, c_re, c_im):
    a_re = a_re.astype(F32)
    a_im = a_im.astype(F32)
    dt = jnp.exp(log_dt.astype(F32))[:, None]
    mag = jnp.exp(a_re * dt)
    ang = a_im * dt
    ab_re = mag * jnp.cos(ang)
    ab_im = mag * jnp.sin(ang)
    den = a_re * a_re + a_im * a_im
    nr = ab_re - 1.0
    ni = ab_im
    f_re = (nr * a_re + ni * a_im) / den
    f_im = (ni * a_re - nr * a_im) / den
    b_re = b_re.astype(F32)
    b_im = b_im.astype(F32)
    bb_re = f_re[..., None] * b_re - f_im[..., None] * b_im
    bb_im = f_re[..., None] * b_im + f_im[..., None] * b_re
    gpc = S5_GROUPS // S5_CHUNKS
    eye = jnp.eye(gpc, dtype=F32)

    def lanes(x):
        return x.reshape(S5_CHUNKS, gpc * S5_STATE)

    a_lanes = jnp.concatenate([lanes(ab_re), lanes(ab_im)], axis=1).reshape(1, S5_LANES)
    a_lanes = jnp.broadcast_to(a_lanes, (BATCH_GROUP, S5_LANES))

    def in_blocks(bb):
        bb = bb.reshape(S5_CHUNKS, gpc, S5_STATE, S5_GROUP)
        return jnp.einsum('cgpj,gh->cgjhp', bb, eye).reshape(S5_CHUNKS, gpc * S5_GROUP, gpc * S5_STATE)

    bm = jnp.concatenate([in_blocks(bb_re), in_blocks(bb_im)], axis=2).astype(BF16)

    def out_blocks(cc):
        cc = cc.astype(F32).reshape(S5_CHUNKS, gpc, S5_GROUP, S5_STATE)
        return jnp.einsum('cgjp,gh->cgphj', cc, eye).reshape(S5_CHUNKS, gpc * S5_STATE, gpc * S5_GROUP)

    cm = jnp.concatenate([out_blocks(c_re), -out_blocks(c_im)], axis=1).astype(BF16)
    return a_lanes, bm, cm


def _s5_state_to_lanes(h_re, h_im):
    bsz = h_re.shape[0]
    re = h_re.astype(F32).reshape(bsz, S5_CHUNKS, -1)
    im = h_im.astype(F32).reshape(bsz, S5_CHUNKS, -1)
    return jnp.concatenate([re, im], axis=2).reshape(bsz, S5_LANES)


def _s5_state_from_lanes(h):
    bsz = h.shape[0]
    h = h.reshape(bsz, S5_CHUNKS, 2, S5_GROUPS // S5_CHUNKS, S5_STATE)
    return (h[:, :, 0].reshape(bsz, S5_GROUPS, S5_STATE), h[:, :, 1].reshape(bsz, S5_GROUPS, S5_STATE))


def _ret_kernel(q_ref, k_ref, v_ref, g_ref, s0_ref, dm_ref, xi_ref, zeta_ref, wo_ref,
                yb_ref, st_ref, s_ref, *, block_decay):
    si = pl.program_id(1)

    @pl.when(si == 0)
    def _():
        s_ref[...] = s0_ref[...]

    gated = []
    for h in range(RET_HEADS):
        qh = q_ref[:, h * RET_DK:(h + 1) * RET_DK]
        kh = k_ref[:, h * RET_DK:(h + 1) * RET_DK]
        vh = v_ref[:, h * RET_DV:(h + 1) * RET_DV]
        scores = lax.dot_general(qh, kh, (((1,), (1,)), ((), ())), preferred_element_type=F32) * dm_ref[h]
        state = s_ref[h]
        o = _bdot(scores.astype(BF16), vh) + _bdot(qh, state.astype(BF16)) * xi_ref[h]
        o = o * lax.rsqrt(jnp.mean(o * o, axis=-1, keepdims=True) + EPS)
        gh = g_ref[:, h * RET_DV:(h + 1) * RET_DV].astype(F32)
        gated.append((o * (gh * _sigmoid(gh))).astype(BF16))
        kz = (kh.astype(F32) * zeta_ref[h]).astype(BF16)
        kv = lax.dot_general(kz, vh, (((0,), (0,)), ((), ())), preferred_element_type=F32)
        s_ref[h] = block_decay[h] * state + kv
    yb_ref[...] = _bdot(jnp.concatenate(gated, axis=1), wo_ref[...]).astype(BF16)

    @pl.when(si == pl.num_programs(1) - 1)
    def _():
        st_ref[...] = s_ref[...]


def _ret_tables(seq):
    cl = min(CHUNK, seq)
    blk = min(RET_BLOCK, seq)
    log_g = jnp.log(1.0 - 2.0 ** (-5.0 - jnp.arange(RET_HEADS, dtype=F32)))
    idx = jnp.arange(blk, dtype=F32)
    diff = idx[:, None] - idx[None, :]
    cn = jnp.arange(blk)[:, None] // cl
    cm = jnp.arange(blk)[None, :] // cl
    expo = jnp.where(cm == cn, jnp.abs(diff), diff)
    dm = jnp.where(cm <= cn, jnp.exp(log_g[:, None, None] * expo[None]), 0.0)
    xi = jnp.exp(log_g[:, None] * (idx + 1.0)[None, :])[..., None]
    zeta = jnp.exp(log_g[:, None] * (blk - 1.0 - idx)[None, :])[..., None]
    block_decay = tuple(math.exp(math.log(1.0 - 2.0 ** (-5.0 - h)) * blk) for h in range(RET_HEADS))
    return blk, dm, xi, zeta, block_decay


def _retention(q, k, v, g, s0, w_ret_out_b):
    bsz, seq, _ = q.shape
    blk, dm, xi, zeta, block_decay = _ret_tables(seq)
    row = lambda w: pl.BlockSpec((None, blk, w), lambda b, s: (b, s, 0))
    const = lambda shape: pl.BlockSpec(shape, lambda b, s: (0,) * len(shape))
    state = pl.BlockSpec((None, RET_HEADS, RET_DK, RET_DV), lambda b, s: (b, 0, 0, 0))
    return pl.pallas_call(
        functools.partial(_ret_kernel, block_decay=block_decay),
        grid=(bsz, seq // blk),
        in_specs=[row(RET_QK), row(RET_QK), row(RET_V), row(RET_V), state,
                  const(dm.shape), const(xi.shape), const(zeta.shape), const((RET_V, D_MODEL))],
        out_specs=[row(D_MODEL), state],
        out_shape=[jax.ShapeDtypeStruct((bsz, seq, D_MODEL), BF16),
                   jax.ShapeDtypeStruct((bsz, RET_HEADS, RET_DK, RET_DV), F32)],
        scratch_shapes=[pltpu.VMEM((RET_HEADS, RET_DK, RET_DV), F32)],
        compiler_params=_cparams("parallel", "arbitrary"),
        name="retention",
    )(q, k, v, g, s0, dm, xi, zeta, w_ret_out_b)


def _store_row_tiles(ref, val):
    n = val.shape[0]
    for j in range(ROW_TILES):
        ref[pl.ds(j, n, stride=ROW_TILES), :] = val[:, j * LANE:(j + 1) * LANE]


def _load_row_tiles(ref, first_row, n):
    return jnp.concatenate([ref[pl.ds(first_row * ROW_TILES + j, n, stride=ROW_TILES), :]
                            for j in range(ROW_TILES)], axis=1)


def _merge_kernel(x_ref, ya_ref, yb_ref, ga_ref, gb_ref, gt_ref, sh_ref, sc_ref, g2_ref, wo_ref,
                  wr_ref, br_ref, x1_ref, n2_ref, re_ref, rw_ref):
    merged = (_sigmoid(ga_ref[...].astype(F32)) * ya_ref[...].astype(F32)
              + _sigmoid(gb_ref[...].astype(F32)) * yb_ref[...].astype(F32))
    x1 = x_ref[...] + gt_ref[...] * _bdot(merged.astype(BF16), wo_ref[...])
    x1_ref[...] = x1
    n2 = x1 * lax.rsqrt(jnp.mean(x1 * x1, axis=-1, keepdims=True) + EPS) * g2_ref[...]
    n2 = n2 * (1.0 + sc_ref[...]) + sh_ref[...]
    _store_row_tiles(n2_ref, n2)

    lt = lax.dot_general(wr_ref[...], n2, (((1,), (1,)), ((), ())),
                         precision=lax.Precision.HIGHEST, preferred_element_type=F32) + br_ref[...]
    tl = lt.shape[1]
    iota = lax.broadcasted_iota(jnp.int32, (8, tl), 0)
    gl = lt[0:8]
    gmax = jnp.max(gl, axis=0, keepdims=True)
    gi = jnp.min(jnp.where(gl == gmax, iota, 8), axis=0, keepdims=True)
    gw = 1.0 / jnp.sum(jnp.exp(gl - gmax), axis=0, keepdims=True)
    el = jnp.zeros((8, tl), F32)
    for g in range(MOE_GROUPS):
        el = jnp.where(gi == g, lt[8 * (g + 1):8 * (g + 2)], el)
    m1 = jnp.max(el, axis=0, keepdims=True)
    i1 = jnp.min(jnp.where(el == m1, iota, 8), axis=0, keepdims=True)
    el2 = jnp.where(iota == i1, -jnp.inf, el)
    m2 = jnp.max(el2, axis=0, keepdims=True)
    i2 = jnp.min(jnp.where(el2 == m2, iota, 8), axis=0, keepdims=True)
    e21 = jnp.exp(m2 - m1)
    w1 = gw / (1.0 + e21)
    w2 = w1 * e21
    re_ref[...] = jnp.where(iota == 0, gi * MOE_EXPERTS + i1, jnp.where(iota == 1, gi * MOE_EXPERTS + i2, 0))
    rw_ref[...] = jnp.where(iota == 0, w1, jnp.where(iota == 1, w2, 0.0))


def _merge(x, ya, yb, ga, gb, gate1, shift2, scale2, g2, w_out_b, wr, br):
    bsz, seq, _ = x.shape
    tl = min(TOKEN_TILE, seq)
    row = pl.BlockSpec((None, tl, D_MODEL), lambda b, t: (b, t, 0))
    vec = pl.BlockSpec((None, 1, D_MODEL), lambda b, t: (b, 0, 0))
    const = lambda shape: pl.BlockSpec(shape, lambda b, t: (0,) * len(shape))
    route = pl.BlockSpec((None, 8, tl), lambda b, t: (b, 0, t))
    nt = seq // tl
    tiles = pl.BlockSpec((tl * ROW_TILES, LANE), lambda b, t: (b * nt + t, 0))
    return pl.pallas_call(
        _merge_kernel,
        grid=(bsz, nt),
        in_specs=[row, row, row, row, row, vec, vec, vec, const((1, D_MODEL)),
                  const((D_MODEL, D_MODEL)), const((ROUTE_ROWS, D_MODEL)), const((ROUTE_ROWS, 1))],
        out_specs=[row, tiles, route, route],
        out_shape=[jax.ShapeDtypeStruct((bsz, seq, D_MODEL), F32),
                   jax.ShapeDtypeStruct((bsz * seq * ROW_TILES, LANE), F32),
                   jax.ShapeDtypeStruct((bsz, 8, seq), jnp.int32),
                   jax.ShapeDtypeStruct((bsz, 8, seq), F32)],
        compiler_params=_cparams("parallel", "parallel"),
        name="merge",
    )(x, ya, yb, ga, gb, gate1, shift2, scale2, g2, w_out_b, wr, br)


def _router_weights(w_rg, b_rg, w_re, b_re):
    wr = jnp.zeros((ROUTE_ROWS, D_MODEL), F32)
    wr = wr.at[0:MOE_GROUPS].set(w_rg.astype(F32).T)
    wr = wr.at[8:].set(jnp.transpose(w_re.astype(F32), (0, 2, 1)).reshape(N_EXPERTS, D_MODEL))
    br = jnp.full((ROUTE_ROWS,), -1e30, F32)
    br = br.at[0:MOE_GROUPS].set(b_rg.astype(F32))
    br = br.at[8:].set(b_re.astype(F32).reshape(N_EXPERTS))
    return wr, br.reshape(ROUTE_ROWS, 1)


def _row_gather(idx_ref, src_hbm, buf, sem, slot, n):
    def body(r, carry):
        src = pl.multiple_of(idx_ref[0, r] * ROW_TILES, ROW_TILES)
        dst = pl.multiple_of(r * ROW_TILES, ROW_TILES)
        pltpu.make_async_copy(src_hbm.at[pl.ds(src, ROW_TILES), :],
                              buf.at[slot, pl.ds(dst, ROW_TILES), :], sem.at[slot]).start()
        return carry
    lax.fori_loop(0, n, body, 0, unroll=DMA_UNROLL)


def _row_gather_wait(src_hbm, buf, sem, slot, n):
    pltpu.make_async_copy(src_hbm.at[pl.ds(0, n * ROW_TILES), :], buf.at[slot], sem.at[slot]).wait()


def _expert_kernel(te_ref, nu_ref, cur_ref, nxt_ref, x_hbm, gate_ref, w1_ref, w3_ref, w2_ref,
                   y_ref, xbuf, sem, *, tm):
    i = pl.program_id(0)
    used = nu_ref[0]
    slot = i % 2

    @pl.when(jnp.logical_and(i == 0, used > 0))
    def _():
        _row_gather(cur_ref, x_hbm, xbuf, sem, 0, tm)

    @pl.when(i + 1 < used)
    def _():
        _row_gather(nxt_ref, x_hbm, xbuf, sem, 1 - slot, tm)

    @pl.when(i < used)
    def _():
        _row_gather_wait(x_hbm, xbuf, sem, slot, tm)
        xb = _load_row_tiles(xbuf.at[slot], 0, tm).astype(BF16)
        a = _bdot(xb, w1_ref[...])
        h = a * _sigmoid(a) * _bdot(xb, w3_ref[...]) * gate_ref[...]
        _store_row_tiles(y_ref, _bdot(h.astype(BF16), w2_ref[...]))

    @pl.when(i >= used)
    def _():
        y_ref[...] = jnp.zeros_like(y_ref)


def _experts(n2_flat, row_token, row_gate, tile_expert, n_used, w1b, w3b, w2b, tm):
    n_tiles = row_token.shape[0] // tm
    idx3 = row_token.reshape(n_tiles, 1, tm)
    last = n_tiles - 1
    grid_spec = pltpu.PrefetchScalarGridSpec(
        num_scalar_prefetch=2,
        grid=(n_tiles,),
        in_specs=[pl.BlockSpec((None, 1, tm), lambda i, te, nu: (i, 0, 0), memory_space=pltpu.SMEM),
                  pl.BlockSpec((None, 1, tm), lambda i, te, nu: (jnp.minimum(i + 1, last), 0, 0),
                               memory_space=pltpu.SMEM),
                  pl.BlockSpec(memory_space=pl.ANY),
                  pl.BlockSpec((tm, 1), lambda i, te, nu: (i, 0)),
                  pl.BlockSpec((None, D_MODEL, MOE_FF), lambda i, te, nu: (te[i], 0, 0)),
                  pl.BlockSpec((None, D_MODEL, MOE_FF), lambda i, te, nu: (te[i], 0, 0)),
                  pl.BlockSpec((None, MOE_FF, D_MODEL), lambda i, te, nu: (te[i], 0, 0))],
        out_specs=pl.BlockSpec((tm * ROW_TILES, LANE), lambda i, te, nu: (i, 0)),
        scratch_shapes=[pltpu.VMEM((2, tm * ROW_TILES, LANE), F32), pltpu.SemaphoreType.DMA((2,))],
    )
    return pl.pallas_call(
        functools.partial(_expert_kernel, tm=tm),
        grid_spec=grid_spec,
        out_shape=jax.ShapeDtypeStruct((n_tiles * tm * ROW_TILES, LANE), F32),
        compiler_params=_cparams("arbitrary"),
        name="experts",
    )(tile_expert, n_used, idx3, idx3, n2_flat, row_gate, w1b, w3b, w2b)


def _dispatch_plan(route_e, route_w, tm):
    bsz, _, seq = route_e.shape
    n_tok = bsz * seq
    e_flat = jnp.transpose(route_e[:, 0:2, :], (0, 2, 1)).reshape(2 * n_tok)
    w_flat = jnp.transpose(route_w[:, 0:2, :], (0, 2, 1)).reshape(2 * n_tok)
    n_asg = 2 * n_tok
    n_tiles = n_asg // tm + N_EXPERTS
    n_rows = n_tiles * tm
    iota = jnp.arange(n_asg, dtype=jnp.int32)
    _, order = lax.sort((e_flat, iota), num_keys=1, is_stable=True)
    _, rank = lax.sort((order, iota), num_keys=1)
    counts = jnp.sum((e_flat[:, None] == jnp.arange(N_EXPERTS)[None, :]).astype(jnp.int32), axis=0)
    padded = ((counts + tm - 1) // tm) * tm
    pad_end = jnp.cumsum(padded)
    pad_start = pad_end - padded
    start = jnp.cumsum(counts) - counts
    pos = ((pad_start - start)[e_flat] + rank).reshape(n_tok, 2)
    tile_start = jnp.arange(n_tiles, dtype=jnp.int32) * tm
    tile_expert = jnp.sum((pad_end[None, :] <= tile_start[:, None]).astype(jnp.int32), axis=1)
    tile_expert = jnp.minimum(tile_expert, N_EXPERTS - 1)
    row_expert = jnp.repeat(tile_expert, tm)
    within = jnp.arange(n_rows, dtype=jnp.int32) - pad_start[row_expert]
    valid = within < counts[row_expert]
    row_asg = order[jnp.clip(start[row_expert] + within, 0, n_asg - 1)]
    row_token = jnp.where(valid, row_asg // 2, 0)
    row_gate = jnp.where(valid, w_flat[row_asg], 0.0)
    n_used = (pad_end[-1] // tm).astype(jnp.int32).reshape(1)
    return row_token, row_gate.reshape(n_rows, 1), pos, tile_expert, n_used


def _combine_kernel(cur_ref, nxt_ref, x1_ref, gt_ref, fn_ref, y_hbm, o_ref, ybuf, sem, *, tl):
    i = pl.program_id(0)
    slot = i % 2

    @pl.when(i == 0)
    def _():
        _row_gather(cur_ref, y_hbm, ybuf, sem, 0, 2 * tl)

    @pl.when(i + 1 < pl.num_programs(0))
    def _():
        _row_gather(nxt_ref, y_hbm, ybuf, sem, 1 - slot, 2 * tl)

    _row_gather_wait(y_hbm, ybuf, sem, slot, 2 * tl)
    rows = ybuf.at[slot]
    moe = _load_row_tiles(rows, 0, tl) + _load_row_tiles(rows, tl, tl)
    x2 = x1_ref[...] + gt_ref[...] * moe
    o_ref[...] = x2 * lax.rsqrt(jnp.mean(x2 * x2, axis=-1, keepdims=True) + EPS) * fn_ref[...]


def _combine(x1_flat, pos, y_rows, gate2, final_norm, seq):
    n_tok = x1_flat.shape[0]
    tl = min(COMBINE_TILE, seq)
    n_tiles = n_tok // tl
    idx3 = jnp.transpose(pos.reshape(n_tiles, tl, 2), (0, 2, 1)).reshape(n_tiles, 1, 2 * tl)
    last = n_tiles - 1
    return pl.pallas_call(
        functools.partial(_combine_kernel, tl=tl),
        grid=(n_tiles,),
        in_specs=[pl.BlockSpec((None, 1, 2 * tl), lambda i: (i, 0, 0), memory_space=pltpu.SMEM),
                  pl.BlockSpec((None, 1, 2 * tl), lambda i: (jnp.minimum(i + 1, last), 0, 0),
                               memory_space=pltpu.SMEM),
                  pl.BlockSpec((tl, D_MODEL), lambda i: (i, 0)),
                  pl.BlockSpec((None, 1, D_MODEL), lambda i: ((i * tl) // seq, 0, 0)),
                  pl.BlockSpec((1, D_MODEL), lambda i: (0, 0)),
                  pl.BlockSpec(memory_space=pl.ANY)],
        out_specs=pl.BlockSpec((tl, D_MODEL), lambda i: (i, 0)),
        out_shape=jax.ShapeDtypeStruct((n_tok, D_MODEL), F32),
        scratch_shapes=[pltpu.VMEM((2, 2 * tl * ROW_TILES, LANE), F32), pltpu.SemaphoreType.DMA((2,))],
        compiler_params=_cparams("arbitrary"),
        name="combine",
    )(idx3, idx3, x1_flat, gate2, final_norm, y_rows)


def _rope_tables(seq, offset):
    half = RET_DK // 2
    theta = 1.0 / (ROPE_BASE ** jnp.linspace(0.0, 1.0, half, dtype=F32))
    pos = offset + jnp.arange(seq)
    ang = pos.astype(F32)[:, None] * theta[None, :]
    cos = jnp.cos(ang)
    sin = jnp.sin(ang)
    return jnp.concatenate([cos, cos], axis=1), jnp.concatenate([-sin, sin], axis=1)


def _layer(x, mod, h0, s0, offset, p, final_norm, expert_tile):
    bsz, seq, _ = x.shape
    sh1, sc1, gt1, sh2, sc2, gt2 = [m.reshape(bsz, 1, D_MODEL) for m in jnp.split(mod, 6, axis=-1)]
    cos2, sin2 = _rope_tables(seq, offset)
    u, q, k, v, g, ga, gb = _inproj(x, sh1, sc1, p['norm1'], p['w_in'], cos2, sin2)
    ya, h_t = _s5(u, h0, p['a_lanes'], p['bm'], p['cm'], p['d'], p['w_glu'], p['b_glu'], p['w_s5_out'])
    yb, s_t = _retention(q, k, v, g, s0, p['w_ret_out'])
    x1, n2, route_e, route_w = _merge(x, ya, yb, ga, gb, gt1, sh2, sc2, p['norm2'], p['w_out'],
                                      p['wr'], p['br'])
    row_token, row_gate, pos, tile_expert, n_used = _dispatch_plan(route_e, route_w, expert_tile)
    y_rows = _experts(n2, row_token, row_gate, tile_expert, n_used, p['w1'], p['w3'], p['w2'], expert_tile)
    y = _combine(x1.reshape(bsz * seq, D_MODEL), pos, y_rows, gt2, final_norm, seq)
    return y.reshape(bsz, seq, D_MODEL), h_t, s_t


def kernel(x_prompt, x_sample, state_s5_re, state_s5_im, state_ret, c_prompt, c_sample, w_ada, b_ada, norm1, norm2, w_in, s5_a_re, s5_a_im, s5_log_dt, s5_b_re, s5_b_im, s5_c_re, s5_c_im, s5_d, s5_w_glu, s5_b_glu, w_s5_out, w_ret_out, w_out, w_rg, b_rg, w_re, b_re, w1, w3, w2, final_norm):
    depth = w_ada.shape[0]
    assert depth == 1
    bp = x_prompt.shape[0]
    bs, seq_s, _ = x_sample.shape
    l = 0
    a_lanes, bm, cm = _s5_params(s5_a_re[l], s5_a_im[l], s5_log_dt[l], s5_b_re[l], s5_b_im[l],
                                 s5_c_re[l], s5_c_im[l])
    wr, br = _router_weights(w_rg[l], b_rg[l], w_re[l], b_re[l])
    p = dict(
        norm1=norm1[l].astype(F32).reshape(1, D_MODEL), norm2=norm2[l].astype(F32).reshape(1, D_MODEL),
        w_in=w_in[l].astype(BF16), a_lanes=a_lanes, bm=bm, cm=cm,
        d=s5_d[l].astype(F32).reshape(1, S5_WIDTH), w_glu=s5_w_glu[l].astype(BF16),
        b_glu=s5_b_glu[l].astype(F32).reshape(1, S5_WIDTH), w_s5_out=w_s5_out[l].astype(BF16),
        w_ret_out=w_ret_out[l].astype(BF16), w_out=w_out[l].astype(BF16), wr=wr, br=br,
        w1=w1[l].astype(BF16).reshape(N_EXPERTS, D_MODEL, MOE_FF),
        w3=w3[l].astype(BF16).reshape(N_EXPERTS, D_MODEL, MOE_FF),
        w2=w2[l].astype(BF16).reshape(N_EXPERTS, MOE_FF, D_MODEL))
    fn = final_norm.astype(F32).reshape(1, D_MODEL)
    mod = _mod(jnp.concatenate([c_prompt, c_sample], axis=0).astype(F32), w_ada[l], b_ada[l])

    h0_p = jnp.zeros((bp, S5_LANES), F32)
    s0_p = jnp.zeros((bp, RET_HEADS, RET_DK, RET_DV), F32)
    y_p, h_p, s_p = _layer(x_prompt, mod[:bp], h0_p, s0_p, 0, p, fn, EXPERT_TILE)
    h0_s = _s5_state_to_lanes(state_s5_re[l], state_s5_im[l])
    y_s, h_s, s_s = _layer(x_sample, mod[bp:], h0_s, state_ret[l].astype(F32), PAST_LEN, p, fn,
                           min(EXPERT_TILE, 128))
    p_re, p_im = _s5_state_from_lanes(h_p)
    s_re, s_im = _s5_state_from_lanes(h_s)
    return (y_p, y_s, p_re[None], p_im[None], s_p[None], s_re[None], s_im[None], s_s[None])
```

```python
import functools
import math

import jax
import jax.numpy as jnp
from jax import lax
from jax.experimental import pallas as pl
from jax.experimental.pallas import tpu as pltpu

F32 = jnp.float32
BF16 = jnp.bfloat16

D_MODEL = 1024
PAST_LEN = 2048
CHUNK = 64
S5_WIDTH = 512
S5_GROUP = 16
S5_GROUPS = 32
S5_STATE = 64
S5_LANES = 2 * S5_GROUPS * S5_STATE
S5_CHUNKS = 4
RET_HEADS = 4
RET_DK = 128
RET_DV = 256
RET_QK = RET_HEADS * RET_DK
RET_V = RET_HEADS * RET_DV
ROPE_BASE = 10000.0
MOE_GROUPS = 4
MOE_EXPERTS = 8
N_EXPERTS = MOE_GROUPS * MOE_EXPERTS
MOE_FF = 256
EPS = 1e-6
IN_WIDTH = S5_WIDTH + 2 * RET_QK + 2 * RET_V + 2 * D_MODEL
ROUTE_ROWS = 8 * (1 + MOE_GROUPS)

BATCH_GROUP = 8
TOKEN_TILE = 512
S5_TIME_TILE = 64
RET_BLOCK = 256
EXPERT_TILE = 512
ROUTE_TILE = 256
VMEM_LIMIT = 56 * 1024 * 1024
LANE = 128
SUBLANE = 8
ROW_TILES = D_MODEL // LANE
SCAN_UNROLL = 8
DMA_UNROLL = 8


def _cparams(*sem):
    return pltpu.CompilerParams(dimension_semantics=sem, vmem_limit_bytes=VMEM_LIMIT)


def _bdot(a, b):
    return jnp.dot(a, b, preferred_element_type=F32)


def _sigmoid(x):
    return 1.0 / (1.0 + jnp.exp(-x))


def _mod_kernel(c_ref, w_ref, b_ref, o_ref):
    c = c_ref[...]
    a = (c * _sigmoid(c)).astype(BF16)
    o_ref[...] = _bdot(a, w_ref[...].astype(BF16)) + b_ref[...]


def _mod(c, w_ada, b_ada):
    n = c.shape[0]
    return pl.pallas_call(
        _mod_kernel,
        grid=(6,),
        in_specs=[pl.BlockSpec((n, D_MODEL), lambda j: (0, 0)),
                  pl.BlockSpec((D_MODEL, D_MODEL), lambda j: (0, j)),
                  pl.BlockSpec((1, D_MODEL), lambda j: (0, j))],
        out_specs=pl.BlockSpec((n, D_MODEL), lambda j: (0, j)),
        out_shape=jax.ShapeDtypeStruct((n, 6 * D_MODEL), F32),
        compiler_params=_cparams("parallel"),
        name="mod",
    )(c, w_ada, b_ada.reshape(1, -1))


def _rope(x, cos2, sin2):
    return x * cos2 + pltpu.roll(x, RET_DK // 2, 1) * sin2


def _inproj_kernel(x_ref, sh_ref, sc_ref, g_ref, w_ref, cos_ref, sin_ref,
                   u_ref, q_ref, k_ref, v_ref, gs_ref, ga_ref, gb_ref):
    x = x_ref[...]
    n = x * lax.rsqrt(jnp.mean(x * x, axis=-1, keepdims=True) + EPS) * g_ref[...]
    nb = (n * (1.0 + sc_ref[...]) + sh_ref[...]).astype(BF16)
    cos2 = cos_ref[...]
    sin2 = sin_ref[...]
    o = 0
    u_ref[...] = _bdot(nb, w_ref[:, o:o + S5_WIDTH]).astype(BF16)
    o += S5_WIDTH
    for h in range(RET_HEADS):
        qh = _bdot(nb, w_ref[:, o + h * RET_DK:o + (h + 1) * RET_DK])
        q_ref[:, h * RET_DK:(h + 1) * RET_DK] = _rope(qh, cos2, sin2).astype(BF16)
    o += RET_QK
    for h in range(RET_HEADS):
        kh = _bdot(nb, w_ref[:, o + h * RET_DK:o + (h + 1) * RET_DK])
        k_ref[:, h * RET_DK:(h + 1) * RET_DK] = (_rope(kh, cos2, sin2) * (RET_DK ** -0.5)).astype(BF16)
    o += RET_QK
    for ref in (v_ref, gs_ref, ga_ref, gb_ref):
        ref[...] = _bdot(nb, w_ref[:, o:o + D_MODEL]).astype(BF16)
        o += D_MODEL


def _inproj(x, shift, scale, g1, w_in_b, cos2, sin2):
    bsz, seq, _ = x.shape
    tl = min(TOKEN_TILE, seq)
    row = lambda w: pl.BlockSpec((None, tl, w), lambda b, t: (b, t, 0))
    vec = pl.BlockSpec((None, 1, D_MODEL), lambda b, t: (b, 0, 0))
    shapes = [S5_WIDTH, RET_QK, RET_QK, RET_V, RET_V, D_MODEL, D_MODEL]
    return pl.pallas_call(
        _inproj_kernel,
        grid=(bsz, seq // tl),
        in_specs=[row(D_MODEL), vec, vec,
                  pl.BlockSpec((1, D_MODEL), lambda b, t: (0, 0)),
                  pl.BlockSpec((D_MODEL, IN_WIDTH), lambda b, t: (0, 0), pipeline_mode=pl.Buffered(1)),
                  pl.BlockSpec((tl, RET_DK), lambda b, t: (t, 0)),
                  pl.BlockSpec((tl, RET_DK), lambda b, t: (t, 0))],
        out_specs=[row(w) for w in shapes],
        out_shape=[jax.ShapeDtypeStruct((bsz, seq, w), BF16) for w in shapes],
        compiler_params=_cparams("parallel", "parallel"),
        name="inproj",
    )(x, shift, scale, g1, w_in_b, cos2, sin2)


def _gelu_tanh(y):
    return 0.5 * y * (1.0 + jnp.tanh(math.sqrt(2.0 / math.pi) * (y + 0.044715 * (y * y * y))))


def _s5_kernel(u_ref, h0_ref, a_ref, pm_ref, pt_ref, bm_ref, cm_ref, d_ref, wg_ref, bg_ref, wo_ref,
               ya_ref, ht_ref, bu_ref, hs_ref, *, tt):
    ti = pl.program_id(1)
    rows = BATCH_GROUP * tt
    half = S5_LANES // (2 * S5_CHUNKS)

    @pl.when(ti == 0)
    def _():
        hs_ref[...] = h0_ref[...]

    u2 = _bdot(pm_ref[...], u_ref[...].reshape(rows, S5_WIDTH)).astype(BF16)
    kc = S5_WIDTH // S5_CHUNKS
    for c in range(S5_CHUNKS):
        bu_ref[:, c * 2 * half:(c + 1) * 2 * half] = _bdot(u2[:, c * kc:(c + 1) * kc], bm_ref[c])

    for c in range(S5_CHUNKS):
        lre = pl.ds(c * 2 * half, half)
        lim = pl.ds(c * 2 * half + half, half)
        are = a_ref[:, lre]
        aim = a_ref[:, lim]

        def step(t, carry, lre=lre, lim=lim, are=are, aim=aim):
            hre, him = carry
            rsel = pl.ds(pl.multiple_of(t * BATCH_GROUP, BATCH_GROUP), BATCH_GROUP)
            nre = are * hre - aim * him + bu_ref[rsel, lre]
            nim = are * him + aim * hre + bu_ref[rsel, lim]
            bu_ref[rsel, lre] = nre
            bu_ref[rsel, lim] = nim
            return nre, nim

        hre, him = lax.fori_loop(0, tt, step, (hs_ref[:, lre], hs_ref[:, lim]), unroll=SCAN_UNROLL)
        hs_ref[:, lre] = hre
        hs_ref[:, lim] = him

    ys = [_bdot(bu_ref[:, c * 2 * half:(c + 1) * 2 * half].astype(BF16), cm_ref[c])
          for c in range(S5_CHUNKS)]
    y = jnp.concatenate(ys, axis=1) + d_ref[...] * u2.astype(F32)
    z = _gelu_tanh(y)
    gl = _bdot(z.astype(BF16), wg_ref[...]) + bg_ref[...]
    o = (z * _sigmoid(gl)).astype(BF16)
    ob = _bdot(pt_ref[...], o).astype(BF16)
    ya_ref[...] = _bdot(ob, wo_ref[...]).reshape(BATCH_GROUP, tt, D_MODEL).astype(BF16)

    @pl.when(ti == pl.num_programs(1) - 1)
    def _():
        ht_ref[...] = hs_ref[...]


def _s5(u, h0, a_lanes, bm, cm, d, wg, bg, wo):
    bsz, seq, _ = u.shape
    tt = min(S5_TIME_TILE, seq)
    rows = BATCH_GROUP * tt
    const = lambda shape: pl.BlockSpec(shape, lambda b, t: (0,) * len(shape))
    r = jnp.arange(rows)
    perm = ((r[:, None] % BATCH_GROUP) * tt + r[:, None] // BATCH_GROUP == r[None, :]).astype(BF16)
    return pl.pallas_call(
        functools.partial(_s5_kernel, tt=tt),
        grid=(bsz // BATCH_GROUP, seq // tt),
        in_specs=[pl.BlockSpec((BATCH_GROUP, tt, S5_WIDTH), lambda b, t: (b, t, 0)),
                  pl.BlockSpec((BATCH_GROUP, S5_LANES), lambda b, t: (b, 0)),
                  const((BATCH_GROUP, S5_LANES)), const((rows, rows)), const((rows, rows)),
                  const(bm.shape), const(cm.shape), const((1, S5_WIDTH)),
                  const((S5_WIDTH, S5_WIDTH)), const((1, S5_WIDTH)), const((S5_WIDTH, D_MODEL))],
        out_specs=[pl.BlockSpec((BATCH_GROUP, tt, D_MODEL), lambda b, t: (b, t, 0)),
                   pl.BlockSpec((BATCH_GROUP, S5_LANES), lambda b, t: (b, 0))],
        out_shape=[jax.ShapeDtypeStruct((bsz, seq, D_MODEL), BF16),
                   jax.ShapeDtypeStruct((bsz, S5_LANES), F32)],
        scratch_shapes=[pltpu.VMEM((rows, S5_LANES), F32),
                        pltpu.VMEM((BATCH_GROUP, S5_LANES), F32)],
        compiler_params=_cparams("parallel", "arbitrary"),
        name="s5",
    )(u, h0, a_lanes, perm, perm.T, bm, cm, d, wg, bg, wo)


def _s5_params(a_re, a_im, log_dt, b_re, b_im, c_re, c_im):
    a_re = a_re.astype(F32)
    a_im = a_im.astype(F32)
    dt = jnp.exp(log_dt.astype(F32))[:, None]
    mag = jnp.exp(a_re * dt)
    ang = a_im * dt
    ab_re = mag * jnp.cos(ang)
    ab_im = mag * jnp.sin(ang)
    den = a_re * a_re + a_im * a_im
    nr = ab_re - 1.0
    ni = ab_im
    f_re = (nr * a_re + ni * a_im) / den
    f_im = (ni * a_re - nr * a_im) / den
    b_re = b_re.astype(F32)
    b_im = b_im.astype(F32)
    bb_re = f_re[..., None] * b_re - f_im[..., None] * b_im
    bb_im = f_re[..., None] * b_im + f_im[..., None] * b_re
    gpc = S5_GROUPS // S5_CHUNKS
    eye = jnp.eye(gpc, dtype=F32)

    def lanes(x):
        return x.reshape(S5_CHUNKS, gpc * S5_STATE)

    a_lanes = jnp.concatenate([lanes(ab_re), lanes(ab_im)], axis=1).reshape(1, S5_LANES)
    a_lanes = jnp.broadcast_to(a_lanes, (BATCH_GROUP, S5_LANES))

    def in_blocks(bb):
        bb = bb.reshape(S5_CHUNKS, gpc, S5_STATE, S5_GROUP)
        return jnp.einsum('cgpj,gh->cgjhp', bb, eye).reshape(S5_CHUNKS, gpc * S5_GROUP, gpc * S5_STATE)

    bm = jnp.concatenate([in_blocks(bb_re), in_blocks(bb_im)], axis=2).astype(BF16)

    def out_blocks(cc):
        cc = cc.astype(F32).reshape(S5_CHUNKS, gpc, S5_GROUP, S5_STATE)
        return jnp.einsum('cgjp,gh->cgphj', cc, eye).reshape(S5_CHUNKS, gpc * S5_STATE, gpc * S5_GROUP)

    cm = jnp.concatenate([out_blocks(c_re), -out_blocks(c_im)], axis=1).astype(BF16)
    return a_lanes, bm, cm


def _s5_state_to_lanes(h_re, h_im):
    bsz = h_re.shape[0]
    re = h_re.astype(F32).reshape(bsz, S5_CHUNKS, -1)
    im = h_im.astype(F32).reshape(bsz, S5_CHUNKS, -1)
    return jnp.concatenate([re, im], axis=2).reshape(bsz, S5_LANES)


def _s5_state_from_lanes(h):
    bsz = h.shape[0]
    h = h.reshape(bsz, S5_CHUNKS, 2, S5_GROUPS // S5_CHUNKS, S5_STATE)
    return (h[:, :, 0].reshape(bsz, S5_GROUPS, S5_STATE), h[:, :, 1].reshape(bsz, S5_GROUPS, S5_STATE))


def _ret_kernel(q_ref, k_ref, v_ref, g_ref, s0_ref, dm_ref, xi_ref, zeta_ref, wo_ref,
                yb_ref, st_ref, s_ref, *, block_decay):
    si = pl.program_id(1)

    @pl.when(si == 0)
    def _():
        s_ref[...] = s0_ref[...]

    gated = []
    for h in range(RET_HEADS):
        qh = q_ref[:, h * RET_DK:(h + 1) * RET_DK]
        kh = k_ref[:, h * RET_DK:(h + 1) * RET_DK]
        vh = v_ref[:, h * RET_DV:(h + 1) * RET_DV]
        scores = lax.dot_general(qh, kh, (((1,), (1,)), ((), ())), preferred_element_type=F32) * dm_ref[h]
        state = s_ref[h]
        o = _bdot(scores.astype(BF16), vh) + _bdot(qh, state.astype(BF16)) * xi_ref[h]
        o = o * lax.rsqrt(jnp.mean(o * o, axis=-1, keepdims=True) + EPS)
        gh = g_ref[:, h * RET_DV:(h + 1) * RET_DV].astype(F32)
        gated.append((o * (gh * _sigmoid(gh))).astype(BF16))
        kz = (kh.astype(F32) * zeta_ref[h]).astype(BF16)
        kv = lax.dot_general(kz, vh, (((0,), (0,)), ((), ())), preferred_element_type=F32)
        s_ref[h] = block_decay[h] * state + kv
    yb_ref[...] = _bdot(jnp.concatenate(gated, axis=1), wo_ref[...]).astype(BF16)

    @pl.when(si == pl.num_programs(1) - 1)
    def _():
        st_ref[...] = s_ref[...]


def _ret_tables(seq):
    cl = min(CHUNK, seq)
    blk = min(RET_BLOCK, seq)
    log_g = jnp.log(1.0 - 2.0 ** (-5.0 - jnp.arange(RET_HEADS, dtype=F32)))
    idx = jnp.arange(blk, dtype=F32)
    diff = idx[:, None] - idx[None, :]
    cn = jnp.arange(blk)[:, None] // cl
    cm = jnp.arange(blk)[None, :] // cl
    expo = jnp.where(cm == cn, jnp.abs(diff), diff)
    dm = jnp.where(cm <= cn, jnp.exp(log_g[:, None, None] * expo[None]), 0.0)
    xi = jnp.exp(log_g[:, None] * (idx + 1.0)[None, :])[..., None]
    zeta = jnp.exp(log_g[:, None] * (blk - 1.0 - idx)[None, :])[..., None]
    block_decay = tuple(math.exp(math.log(1.0 - 2.0 ** (-5.0 - h)) * blk) for h in range(RET_HEADS))
    return blk, dm, xi, zeta, block_decay


def _retention(q, k, v, g, s0, w_ret_out_b):
    bsz, seq, _ = q.shape
    blk, dm, xi, zeta, block_decay = _ret_tables(seq)
    row = lambda w: pl.BlockSpec((None, blk, w), lambda b, s: (b, s, 0))
    const = lambda shape: pl.BlockSpec(shape, lambda b, s: (0,) * len(shape))
    state = pl.BlockSpec((None, RET_HEADS, RET_DK, RET_DV), lambda b, s: (b, 0, 0, 0))
    return pl.pallas_call(
        functools.partial(_ret_kernel, block_decay=block_decay),
        grid=(bsz, seq // blk),
        in_specs=[row(RET_QK), row(RET_QK), row(RET_V), row(RET_V), state,
                  const(dm.shape), const(xi.shape), const(zeta.shape), const((RET_V, D_MODEL))],
        out_specs=[row(D_MODEL), state],
        out_shape=[jax.ShapeDtypeStruct((bsz, seq, D_MODEL), BF16),
                   jax.ShapeDtypeStruct((bsz, RET_HEADS, RET_DK, RET_DV), F32)],
        scratch_shapes=[pltpu.VMEM((RET_HEADS, RET_DK, RET_DV), F32)],
        compiler_params=_cparams("parallel", "arbitrary"),
        name="retention",
    )(q, k, v, g, s0, dm, xi, zeta, w_ret_out_b)


def _store_row_tiles(ref, val):
    n = val.shape[0]
    for j in range(ROW_TILES):
        ref[pl.ds(j, n, stride=ROW_TILES), :] = val[:, j * LANE:(j + 1) * LANE]


def _load_row_tiles(ref, first_row, n):
    return jnp.concatenate([ref[pl.ds(first_row * ROW_TILES + j, n, stride=ROW_TILES), :]
                            for j in range(ROW_TILES)], axis=1)


def _merge_kernel(x_ref, ya_ref, yb_ref, ga_ref, gb_ref, gt_ref, sh_ref, sc_ref, g2_ref, wo_ref,
                  wr_ref, br_ref, x1_ref, n2_ref, re_ref, wc_ref, cnt_ref):
    merged = (_sigmoid(ga_ref[...].astype(F32)) * ya_ref[...].astype(F32)
              + _sigmoid(gb_ref[...].astype(F32)) * yb_ref[...].astype(F32))
    x1 = x_ref[...] + gt_ref[...] * _bdot(merged.astype(BF16), wo_ref[...])
    x1_ref[...] = x1
    n2 = x1 * lax.rsqrt(jnp.mean(x1 * x1, axis=-1, keepdims=True) + EPS) * g2_ref[...]
    n2 = n2 * (1.0 + sc_ref[...]) + sh_ref[...]
    _store_row_tiles(n2_ref, n2)

    lt = lax.dot_general(wr_ref[...], n2, (((1,), (1,)), ((), ())),
                         precision=lax.Precision.HIGHEST, preferred_element_type=F32) + br_ref[...]
    tl = lt.shape[1]
    iota = lax.broadcasted_iota(jnp.int32, (8, tl), 0)
    gl = lt[0:8]
    gmax = jnp.max(gl, axis=0, keepdims=True)
    gi = jnp.min(jnp.where(gl == gmax, iota, 8), axis=0, keepdims=True)
    gw = 1.0 / jnp.sum(jnp.exp(gl - gmax), axis=0, keepdims=True)
    el = jnp.zeros((8, tl), F32)
    for g in range(MOE_GROUPS):
        el = jnp.where(gi == g, lt[8 * (g + 1):8 * (g + 2)], el)
    m1 = jnp.max(el, axis=0, keepdims=True)
    i1 = jnp.min(jnp.where(el == m1, iota, 8), axis=0, keepdims=True)
    el2 = jnp.where(iota == i1, -jnp.inf, el)
    m2 = jnp.max(el2, axis=0, keepdims=True)
    i2 = jnp.min(jnp.where(el2 == m2, iota, 8), axis=0, keepdims=True)
    e21 = jnp.exp(m2 - m1)
    w1 = gw / (1.0 + e21)
    w2 = w1 * e21
    e1 = gi * MOE_EXPERTS + i1
    e2 = gi * MOE_EXPERTS + i2
    re_ref[...] = jnp.where(iota == 0, e1, jnp.where(iota == 1, e2, 0))
    rw = jnp.where(iota == 0, w1, jnp.where(iota == 1, w2, 0.0))
    eye = (lax.broadcasted_iota(jnp.int32, (8, LANE), 0)
           == lax.broadcasted_iota(jnp.int32, (8, LANE), 1)).astype(F32)
    wc_ref[...] = lax.dot_general(rw, eye, (((0,), (0,)), ((), ())),
                                  precision=lax.Precision.HIGHEST, preferred_element_type=F32)
    ids = lax.broadcasted_iota(jnp.int32, (N_EXPERTS, tl), 0)
    hits = (ids == e1).astype(F32) + (ids == e2).astype(F32)
    cnt_ref[...] = jnp.broadcast_to(jnp.sum(hits, axis=1, keepdims=True), (N_EXPERTS, LANE)).astype(jnp.int32)


def _merge(x, ya, yb, ga, gb, gate1, shift2, scale2, g2, w_out_b, wr, br):
    bsz, seq, _ = x.shape
    tl = min(TOKEN_TILE, seq)
    row = pl.BlockSpec((None, tl, D_MODEL), lambda b, t: (b, t, 0))
    vec = pl.BlockSpec((None, 1, D_MODEL), lambda b, t: (b, 0, 0))
    const = lambda shape: pl.BlockSpec(shape, lambda b, t: (0,) * len(shape))
    route = pl.BlockSpec((None, 8, tl), lambda b, t: (b, 0, t))
    nt = seq // tl
    tiles = pl.BlockSpec((tl * ROW_TILES, LANE), lambda b, t: (b * nt + t, 0))
    return pl.pallas_call(
        _merge_kernel,
        grid=(bsz, nt),
        in_specs=[row, row, row, row, row, vec, vec, vec, const((1, D_MODEL)),
                  const((D_MODEL, D_MODEL)), const((ROUTE_ROWS, D_MODEL)), const((ROUTE_ROWS, 1))],
        out_specs=[row, tiles, route,
                   pl.BlockSpec((tl, LANE), lambda b, t: (b * nt + t, 0)),
                   pl.BlockSpec((None, N_EXPERTS, LANE), lambda b, t: (b * nt + t, 0, 0))],
        out_shape=[jax.ShapeDtypeStruct((bsz, seq, D_MODEL), F32),
                   jax.ShapeDtypeStruct((bsz * seq * ROW_TILES, LANE), F32),
                   jax.ShapeDtypeStruct((bsz, 8, seq), jnp.int32),
                   jax.ShapeDtypeStruct((bsz * seq, LANE), F32),
                   jax.ShapeDtypeStruct((bsz * nt, N_EXPERTS, LANE), jnp.int32)],
        compiler_params=_cparams("parallel", "parallel"),
        name="merge",
    )(x, ya, yb, ga, gb, gate1, shift2, scale2, g2, w_out_b, wr, br)


def _router_weights(w_rg, b_rg, w_re, b_re):
    wr = jnp.zeros((ROUTE_ROWS, D_MODEL), F32)
    wr = wr.at[0:MOE_GROUPS].set(w_rg.astype(F32).T)
    wr = wr.at[8:].set(jnp.transpose(w_re.astype(F32), (0, 2, 1)).reshape(N_EXPERTS, D_MODEL))
    br = jnp.full((ROUTE_ROWS,), -1e30, F32)
    br = br.at[0:MOE_GROUPS].set(b_rg.astype(F32))
    br = br.at[8:].set(b_re.astype(F32).reshape(N_EXPERTS))
    return wr, br.reshape(ROUTE_ROWS, 1)


def _row_gather(idx_ref, src_hbm, buf, sem, slot, n):
    def body(r, carry):
        src = pl.multiple_of(idx_ref[0, r] * ROW_TILES, ROW_TILES)
        dst = pl.multiple_of(r * ROW_TILES, ROW_TILES)
        pltpu.make_async_copy(src_hbm.at[pl.ds(src, ROW_TILES), :],
                              buf.at[slot, pl.ds(dst, ROW_TILES), :], sem.at[slot]).start()
        return carry
    lax.fori_loop(0, n, body, 0, unroll=DMA_UNROLL)


def _row_gather_wait(src_hbm, buf, sem, slot, n):
    pltpu.make_async_copy(src_hbm.at[pl.ds(0, n * ROW_TILES), :], buf.at[slot], sem.at[slot]).wait()


def _plan_kernel(e_ref, base_ref, c0_ref, tri_ref, pos_ref, carry_ref):
    @pl.when(pl.program_id(0) == 0)
    def _():
        carry_ref[...] = c0_ref[...]

    tb = e_ref.shape[1]
    ids = lax.broadcasted_iota(jnp.int32, (N_EXPERTS, tb), 0)
    carry = carry_ref[...]
    base = base_ref[...]
    out = []
    for k in range(2):
        hit = ids == e_ref[k:k + 1, :]
        incl = _bdot(jnp.where(hit, 1.0, 0.0).astype(BF16), tri_ref[...])
        out.append(jnp.sum(jnp.where(hit, incl - 1.0 + carry + base, 0.0), axis=0, keepdims=True))
        carry = carry + incl[:, tb - 1:tb]
    carry_ref[...] = carry
    pos_ref[...] = jnp.concatenate(out, axis=1).astype(jnp.int32)


def _plan(route_e, pad_start, placed):
    bsz, _, seq = route_e.shape
    tb = min(ROUTE_TILE, seq)
    nb = seq // tb
    r = jnp.arange(tb)
    tri = (r[:, None] <= r[None, :]).astype(BF16)
    col = pl.BlockSpec((N_EXPERTS, 1), lambda i: (0, 0))
    return pl.pallas_call(
        _plan_kernel,
        grid=(bsz * nb,),
        in_specs=[pl.BlockSpec((None, 8, tb), lambda i: (i // nb, 0, i % nb)), col, col,
                  pl.BlockSpec((tb, tb), lambda i: (0, 0))],
        out_specs=pl.BlockSpec((None, 1, 2 * tb), lambda i: (i, 0, 0)),
        out_shape=jax.ShapeDtypeStruct((bsz * nb, 1, 2 * tb), jnp.int32),
        scratch_shapes=[pltpu.VMEM((N_EXPERTS, 1), F32)],
        compiler_params=_cparams("arbitrary"),
        name="plan",
    )(route_e, pad_start.astype(F32).reshape(N_EXPERTS, 1), placed.astype(F32).reshape(N_EXPERTS, 1), tri)


ZERO_CHUNKS = tuple(2 ** b for b in range(8, -1, -1))


def _dispatch_kernel(lo_ref, hi_ref, nu_ref, pos_ref, x_ref, xs_hbm, zero_ref, sem, *, tb, tm, n_tiles):
    def rows_of(k):
        def body(r, carry):
            src = pl.multiple_of(r * ROW_TILES, ROW_TILES)
            dst = pl.multiple_of(pos_ref[0, k * tb + r] * ROW_TILES, ROW_TILES)
            pltpu.make_async_copy(x_ref.at[pl.ds(src, ROW_TILES), :],
                                  xs_hbm.at[pl.ds(dst, ROW_TILES), :], sem.at[0]).start()
            return carry
        lax.fori_loop(0, tb, body, 0, unroll=DMA_UNROLL)

    rows_of(0)
    rows_of(1)

    def zero_copy(row, size):
        dst = pl.multiple_of(row * ROW_TILES, ROW_TILES)
        return pltpu.make_async_copy(zero_ref.at[pl.ds(0, size * ROW_TILES), :],
                                     xs_hbm.at[pl.ds(dst, size * ROW_TILES), :], sem.at[1])

    @pl.when(pl.program_id(0) == pl.num_programs(0) - 1)
    def _():
        zero_ref[...] = jnp.zeros_like(zero_ref)

        def per_expert(e, carry):
            lo = lo_ref[e]
            n = hi_ref[e] - lo
            for wait in (False, True):
                row = lo
                for size in ZERO_CHUNKS:
                    @pl.when((n & size) != 0)
                    def _(row=row, size=size, wait=wait):
                        zero_copy(row, size).wait() if wait else zero_copy(row, size).start()
                    row = row + (n & size)
            return carry
        lax.fori_loop(0, N_EXPERTS, per_expert, 0)

        chunk = min(ZERO_CHUNKS[0], tm)
        for wait in (False, True):
            def per_tile(t, carry, wait=wait):
                for c in range(tm // chunk):
                    cp = zero_copy(t * tm + c * chunk, chunk)
                    cp.wait() if wait else cp.start()
                return carry
            lax.fori_loop(nu_ref[0], n_tiles, per_tile, 0)

    for _ in range(2):
        pltpu.make_async_copy(x_ref, xs_hbm.at[pl.ds(0, tb * ROW_TILES), :], sem.at[0]).wait()


def _dispatch(pos, n2_tiles, pad_lo, pad_hi, n_used, n_rows, tm):
    n_blocks, _, two_tb = pos.shape
    tb = two_tb // 2
    grid_spec = pltpu.PrefetchScalarGridSpec(
        num_scalar_prefetch=3,
        grid=(n_blocks,),
        in_specs=[pl.BlockSpec((None, 1, two_tb), lambda i, lo, hi, nu: (i, 0, 0), memory_space=pltpu.SMEM),
                  pl.BlockSpec((tb * ROW_TILES, LANE), lambda i, lo, hi, nu: (i, 0))],
        out_specs=pl.BlockSpec(memory_space=pl.ANY),
        scratch_shapes=[pltpu.VMEM((ZERO_CHUNKS[0] * ROW_TILES, LANE), F32), pltpu.SemaphoreType.DMA((2,))],
    )
    return pl.pallas_call(
        functools.partial(_dispatch_kernel, tb=tb, tm=tm, n_tiles=n_rows // tm),
        grid_spec=grid_spec,
        out_shape=jax.ShapeDtypeStruct((n_rows * ROW_TILES, LANE), F32),
        compiler_params=_cparams("arbitrary"),
        name="dispatch",
    )(pad_lo, pad_hi, n_used, pos, n2_tiles)


def _expert_kernel(te_ref, nu_ref, x_ref, w1_ref, w3_ref, w2_ref, y_ref, *, tm):
    @pl.when(pl.program_id(0) < nu_ref[0])
    def _():
        xb = _load_row_tiles(x_ref, 0, tm).astype(BF16)
        a = _bdot(xb, w1_ref[...])
        h = a * _sigmoid(a) * _bdot(xb, w3_ref[...])
        _store_row_tiles(y_ref, _bdot(h.astype(BF16), w2_ref[...]))

    @pl.when(pl.program_id(0) >= nu_ref[0])
    def _():
        y_ref[...] = jnp.zeros_like(y_ref)


def _experts(xs, tile_expert, n_used, w1b, w3b, w2b, tm):
    n_tiles = xs.shape[0] // (tm * ROW_TILES)
    tile = lambda i, te, nu: (jnp.minimum(i, nu[0] - 1), 0)
    grid_spec = pltpu.PrefetchScalarGridSpec(
        num_scalar_prefetch=2,
        grid=(n_tiles,),
        in_specs=[pl.BlockSpec((tm * ROW_TILES, LANE), tile),
                  pl.BlockSpec((None, D_MODEL, MOE_FF), lambda i, te, nu: (te[i], 0, 0)),
                  pl.BlockSpec((None, D_MODEL, MOE_FF), lambda i, te, nu: (te[i], 0, 0)),
                  pl.BlockSpec((None, MOE_FF, D_MODEL), lambda i, te, nu: (te[i], 0, 0))],
        out_specs=pl.BlockSpec((tm * ROW_TILES, LANE), lambda i, te, nu: (i, 0)),
    )
    return pl.pallas_call(
        functools.partial(_expert_kernel, tm=tm),
        grid_spec=grid_spec,
        out_shape=jax.ShapeDtypeStruct(xs.shape, F32),
        compiler_params=_cparams("arbitrary"),
        name="experts",
    )(tile_expert, n_used, xs, w1b, w3b, w2b)


def _expert_layout(counts, tm, n_tok):
    n_tiles = 2 * n_tok // tm + N_EXPERTS
    padded = ((counts + tm - 1) // tm) * tm
    pad_end = jnp.cumsum(padded)
    pad_start = pad_end - padded
    n_used = pad_end[-1] // tm
    tile_start = jnp.minimum(jnp.arange(n_tiles, dtype=jnp.int32), n_used - 1) * tm
    tile_expert = jnp.sum((pad_end[None, :] <= tile_start[:, None]).astype(jnp.int32), axis=1)
    return (pad_start, pad_start + counts, pad_end, tile_expert.astype(jnp.int32),
            n_used.astype(jnp.int32).reshape(1), n_tiles * tm)


def _combine_kernel(cur_ref, nxt_ref, x1_ref, wc_ref, gt_ref, fn_ref, y_hbm, o_ref, ybuf, sem, *, tl):
    i = pl.program_id(0)
    slot = i % 2

    @pl.when(i == 0)
    def _():
        _row_gather(cur_ref, y_hbm, ybuf, sem, 0, 2 * tl)

    @pl.when(i + 1 < pl.num_programs(0))
    def _():
        _row_gather(nxt_ref, y_hbm, ybuf, sem, 1 - slot, 2 * tl)

    _row_gather_wait(y_hbm, ybuf, sem, slot, 2 * tl)
    rows = ybuf.at[slot]
    moe = wc_ref[:, 0:1] * _load_row_tiles(rows, 0, tl) + wc_ref[:, 1:2] * _load_row_tiles(rows, tl, tl)
    x2 = x1_ref[...] + gt_ref[...] * moe
    o_ref[...] = x2 * lax.rsqrt(jnp.mean(x2 * x2, axis=-1, keepdims=True) + EPS) * fn_ref[...]


def _combine(x1_flat, pos, y_rows, wcol, gate2, final_norm, seq):
    n_tok = x1_flat.shape[0]
    n_tiles, _, two_tl = pos.shape
    tl = two_tl // 2
    last = n_tiles - 1
    return pl.pallas_call(
        functools.partial(_combine_kernel, tl=tl),
        grid=(n_tiles,),
        in_specs=[pl.BlockSpec((None, 1, 2 * tl), lambda i: (i, 0, 0), memory_space=pltpu.SMEM),
                  pl.BlockSpec((None, 1, 2 * tl), lambda i: (jnp.minimum(i + 1, last), 0, 0),
                               memory_space=pltpu.SMEM),
                  pl.BlockSpec((tl, D_MODEL), lambda i: (i, 0)),
                  pl.BlockSpec((tl, LANE), lambda i: (i, 0)),
                  pl.BlockSpec((None, 1, D_MODEL), lambda i: ((i * tl) // seq, 0, 0)),
                  pl.BlockSpec((1, D_MODEL), lambda i: (0, 0)),
                  pl.BlockSpec(memory_space=pl.ANY)],
        out_specs=pl.BlockSpec((tl, D_MODEL), lambda i: (i, 0)),
        out_shape=jax.ShapeDtypeStruct((n_tok, D_MODEL), F32),
        scratch_shapes=[pltpu.VMEM((2, 2 * tl * ROW_TILES, LANE), F32), pltpu.SemaphoreType.DMA((2,))],
        compiler_params=_cparams("arbitrary"),
        name="combine",
    )(pos, pos, x1_flat, wcol, gate2, final_norm, y_rows)


def _rope_tables(seq, offset):
    half = RET_DK // 2
    theta = 1.0 / (ROPE_BASE ** jnp.linspace(0.0, 1.0, half, dtype=F32))
    pos = offset + jnp.arange(seq)
    ang = pos.astype(F32)[:, None] * theta[None, :]
    cos = jnp.cos(ang)
    sin = jnp.sin(ang)
    return jnp.concatenate([cos, cos], axis=1), jnp.concatenate([-sin, sin], axis=1)


def _layer(x, mod, h0, s0, offset, p, final_norm, expert_tile):
    bsz, seq, _ = x.shape
    sh1, sc1, gt1, sh2, sc2, gt2 = [m.reshape(bsz, 1, D_MODEL) for m in jnp.split(mod, 6, axis=-1)]
    cos2, sin2 = _rope_tables(seq, offset)
    u, q, k, v, g, ga, gb = _inproj(x, sh1, sc1, p['norm1'], p['w_in'], cos2, sin2)
    ya, h_t = _s5(u, h0, p['a_lanes'], p['bm'], p['cm'], p['d'], p['w_glu'], p['b_glu'], p['w_s5_out'])
    yb, s_t = _retention(q, k, v, g, s0, p['w_ret_out'])
    x1, n2, route_e, wcol, cnt = _merge(x, ya, yb, ga, gb, gt1, sh2, sc2, p['norm2'], p['w_out'],
                                        p['wr'], p['br'])
    counts = jnp.sum(cnt[:, :, 0], axis=0)
    pad_start, pad_lo, pad_hi, tile_expert, n_used, n_rows = _expert_layout(counts, expert_tile, bsz * seq)
    pos = _plan(route_e, pad_start, jnp.zeros_like(counts))
    xs = _dispatch(pos, n2, pad_lo, pad_hi, n_used, n_rows, expert_tile)
    y_rows = _experts(xs, tile_expert, n_used, p['w1'], p['w3'], p['w2'], expert_tile)
    y = _combine(x1.reshape(bsz * seq, D_MODEL), pos, y_rows, wcol, gt2, final_norm, seq)
    return y.reshape(bsz, seq, D_MODEL), h_t, s_t


def kernel(x_prompt, x_sample, state_s5_re, state_s5_im, state_ret, c_prompt, c_sample, w_ada, b_ada, norm1, norm2, w_in, s5_a_re, s5_a_im, s5_log_dt, s5_b_re, s5_b_im, s5_c_re, s5_c_im, s5_d, s5_w_glu, s5_b_glu, w_s5_out, w_ret_out, w_out, w_rg, b_rg, w_re, b_re, w1, w3, w2, final_norm):
    depth = w_ada.shape[0]
    assert depth == 1
    bp = x_prompt.shape[0]
    bs, seq_s, _ = x_sample.shape
    l = 0
    a_lanes, bm, cm = _s5_params(s5_a_re[l], s5_a_im[l], s5_log_dt[l], s5_b_re[l], s5_b_im[l],
                                 s5_c_re[l], s5_c_im[l])
    wr, br = _router_weights(w_rg[l], b_rg[l], w_re[l], b_re[l])
    p = dict(
        norm1=norm1[l].astype(F32).reshape(1, D_MODEL), norm2=norm2[l].astype(F32).reshape(1, D_MODEL),
        w_in=w_in[l].astype(BF16), a_lanes=a_lanes, bm=bm, cm=cm,
        d=s5_d[l].astype(F32).reshape(1, S5_WIDTH), w_glu=s5_w_glu[l].astype(BF16),
        b_glu=s5_b_glu[l].astype(F32).reshape(1, S5_WIDTH), w_s5_out=w_s5_out[l].astype(BF16),
        w_ret_out=w_ret_out[l].astype(BF16), w_out=w_out[l].astype(BF16), wr=wr, br=br,
        w1=w1[l].astype(BF16).reshape(N_EXPERTS, D_MODEL, MOE_FF),
        w3=w3[l].astype(BF16).reshape(N_EXPERTS, D_MODEL, MOE_FF),
        w2=w2[l].astype(BF16).reshape(N_EXPERTS, MOE_FF, D_MODEL))
    fn = final_norm.astype(F32).reshape(1, D_MODEL)
    mod = _mod(jnp.concatenate([c_prompt, c_sample], axis=0).astype(F32), w_ada[l], b_ada[l])

    h0_p = jnp.zeros((bp, S5_LANES), F32)
    s0_p = jnp.zeros((bp, RET_HEADS, RET_DK, RET_DV), F32)
    y_p, h_p, s_p = _layer(x_prompt, mod[:bp], h0_p, s0_p, 0, p, fn, EXPERT_TILE)
    h0_s = _s5_state_to_lanes(state_s5_re[l], state_s5_im[l])
    y_s, h_s, s_s = _layer(x_sample, mod[bp:], h0_s, state_ret[l].astype(F32), PAST_LEN, p, fn,
                           min(EXPERT_TILE, 128))
    p_re, p_im = _s5_state_from_lanes(h_p)
    s_re, s_im = _s5_state_from_lanes(h_s)
    return (y_p, y_s, p_re[None], p_im[None], s_p[None], s_re[None], s_im[None], s_s[None])
```

```python
import functools
import math

import jax
import jax.numpy as jnp
from jax import lax
from jax.experimental import pallas as pl
from jax.experimental.pallas import tpu as pltpu

F32 = jnp.float32
BF16 = jnp.bfloat16

D_MODEL = 1024
PAST_LEN = 2048
CHUNK = 64
S5_WIDTH = 512
S5_GROUP = 16
S5_GROUPS = 32
S5_STATE = 64
S5_LANES = 2 * S5_GROUPS * S5_STATE
S5_CHUNKS = 4
RET_HEADS = 4
RET_DK = 128
RET_DV = 256
RET_QK = RET_HEADS * RET_DK
RET_V = RET_HEADS * RET_DV
ROPE_BASE = 10000.0
MOE_GROUPS = 4
MOE_EXPERTS = 8
N_EXPERTS = MOE_GROUPS * MOE_EXPERTS
MOE_FF = 256
EPS = 1e-6
IN_WIDTH = S5_WIDTH + 2 * RET_QK + 2 * RET_V + 2 * D_MODEL
ROUTE_ROWS = 8 * (1 + MOE_GROUPS)

BATCH_GROUP = 8
TOKEN_TILE = 512
S5_TIME_TILE = 64
RET_BLOCK = 256
EXPERT_TILE = 512
ROUTE_TILE = 256
VMEM_LIMIT = 56 * 1024 * 1024
LANE = 128
SUBLANE = 8
ROW_TILES = D_MODEL // LANE
DMA_UNROLL = 8
DMA_QUEUES = 2


def _cparams(*sem):
    return pltpu.CompilerParams(dimension_semantics=sem, vmem_limit_bytes=VMEM_LIMIT)


def _bdot(a, b):
    return jnp.dot(a, b, preferred_element_type=F32)


def _sigmoid(x):
    return 0.5 * jnp.tanh(0.5 * x) + 0.5


def _mod_kernel(c_ref, w_ref, b_ref, o_ref):
    c = c_ref[...]
    a = (c * _sigmoid(c)).astype(BF16)
    o_ref[...] = _bdot(a, w_ref[...].astype(BF16)) + b_ref[...]


def _mod(c, w_ada, b_ada):
    n = c.shape[0]
    return pl.pallas_call(
        _mod_kernel,
        grid=(6,),
        in_specs=[pl.BlockSpec((n, D_MODEL), lambda j: (0, 0)),
                  pl.BlockSpec((D_MODEL, D_MODEL), lambda j: (0, j)),
                  pl.BlockSpec((1, D_MODEL), lambda j: (0, j))],
        out_specs=pl.BlockSpec((n, D_MODEL), lambda j: (0, j)),
        out_shape=jax.ShapeDtypeStruct((n, 6 * D_MODEL), F32),
        compiler_params=_cparams("parallel"),
        name="mod",
    )(c, w_ada, b_ada.reshape(1, -1))


def _rope(x, cos2, sin2):
    return x * cos2 + pltpu.roll(x, RET_DK // 2, 1) * sin2


def _inproj_kernel(x_ref, sh_ref, sc_ref, g_ref, w_ref, cos_ref, sin_ref,
                   u_ref, q_ref, k_ref, v_ref, gs_ref, ga_ref, gb_ref):
    x = x_ref[...]
    n = x * lax.rsqrt(jnp.mean(x * x, axis=-1, keepdims=True) + EPS) * g_ref[...]
    nb = (n * (1.0 + sc_ref[...]) + sh_ref[...]).astype(BF16)
    cos2 = cos_ref[...]
    sin2 = sin_ref[...]
    o = 0
    u_ref[...] = _bdot(nb, w_ref[:, o:o + S5_WIDTH]).astype(BF16)
    o += S5_WIDTH
    for h in range(RET_HEADS):
        qh = _bdot(nb, w_ref[:, o + h * RET_DK:o + (h + 1) * RET_DK])
        q_ref[:, h * RET_DK:(h + 1) * RET_DK] = _rope(qh, cos2, sin2).astype(BF16)
    o += RET_QK
    for h in range(RET_HEADS):
        kh = _bdot(nb, w_ref[:, o + h * RET_DK:o + (h + 1) * RET_DK])
        k_ref[:, h * RET_DK:(h + 1) * RET_DK] = (_rope(kh, cos2, sin2) * (RET_DK ** -0.5)).astype(BF16)
    o += RET_QK
    for ref in (v_ref, gs_ref, ga_ref, gb_ref):
        ref[...] = _bdot(nb, w_ref[:, o:o + D_MODEL]).astype(BF16)
        o += D_MODEL


def _inproj(x, shift, scale, g1, w_in_b, cos2, sin2):
    bsz, seq, _ = x.shape
    tl = min(TOKEN_TILE, seq)
    row = lambda w: pl.BlockSpec((None, tl, w), lambda b, t: (b, t, 0))
    vec = pl.BlockSpec((None, 1, D_MODEL), lambda b, t: (b, 0, 0))
    shapes = [S5_WIDTH, RET_QK, RET_QK, RET_V, RET_V, D_MODEL, D_MODEL]
    return pl.pallas_call(
        _inproj_kernel,
        grid=(bsz, seq // tl),
        in_specs=[row(D_MODEL), vec, vec,
                  pl.BlockSpec((1, D_MODEL), lambda b, t: (0, 0)),
                  pl.BlockSpec((D_MODEL, IN_WIDTH), lambda b, t: (0, 0), pipeline_mode=pl.Buffered(1)),
                  pl.BlockSpec((tl, RET_DK), lambda b, t: (t, 0)),
                  pl.BlockSpec((tl, RET_DK), lambda b, t: (t, 0))],
        out_specs=[row(w) for w in shapes],
        out_shape=[jax.ShapeDtypeStruct((bsz, seq, w), BF16) for w in shapes],
        compiler_params=_cparams("parallel", "parallel"),
        name="inproj",
    )(x, shift, scale, g1, w_in_b, cos2, sin2)


def _gelu_tanh(y):
    return 0.5 * y * (1.0 + jnp.tanh(math.sqrt(2.0 / math.pi) * (y + 0.044715 * (y * y * y))))


def _s5_kernel(u_ref, h0_ref, a_ref, pm_ref, pt_ref, bm_ref, cm_ref, d_ref, wg_ref, bg_ref, wo_ref,
               ya_ref, ht_ref, bu_ref, hs_ref, *, tt):
    ti = pl.program_id(1)
    rows = BATCH_GROUP * tt
    half = S5_LANES // (2 * S5_CHUNKS)

    @pl.when(ti == 0)
    def _():
        hs_ref[...] = h0_ref[...]

    u2 = _bdot(pm_ref[...], u_ref[...].reshape(rows, S5_WIDTH)).astype(BF16)
    kc = S5_WIDTH // S5_CHUNKS
    ys = []
    for c in range(S5_CHUNKS):
        lanes = slice(c * 2 * half, (c + 1) * 2 * half)
        lre = slice(c * 2 * half, c * 2 * half + half)
        lim = slice(c * 2 * half + half, (c + 1) * 2 * half)
        bu_ref[:, lanes] = _bdot(u2[:, c * kc:(c + 1) * kc], bm_ref[c])
        are = a_ref[:, lre]
        aim = a_ref[:, lim]
        hre = hs_ref[:, lre]
        him = hs_ref[:, lim]
        for t in range(tt):
            rsel = slice(t * BATCH_GROUP, (t + 1) * BATCH_GROUP)
            hre, him = (are * hre - aim * him + bu_ref[rsel, lre],
                        are * him + aim * hre + bu_ref[rsel, lim])
            bu_ref[rsel, lre] = hre
            bu_ref[rsel, lim] = him
        hs_ref[:, lre] = hre
        hs_ref[:, lim] = him
        ys.append(_bdot(bu_ref[:, lanes].astype(BF16), cm_ref[c]))
    y = jnp.concatenate(ys, axis=1) + d_ref[...] * u2.astype(F32)
    z = _gelu_tanh(y)
    gl = _bdot(z.astype(BF16), wg_ref[...]) + bg_ref[...]
    o = (z * _sigmoid(gl)).astype(BF16)
    ob = _bdot(pt_ref[...], o).astype(BF16)
    ya_ref[...] = _bdot(ob, wo_ref[...]).reshape(BATCH_GROUP, tt, D_MODEL).astype(BF16)

    @pl.when(ti == pl.num_programs(1) - 1)
    def _():
        ht_ref[...] = hs_ref[...]


def _s5(u, h0, a_lanes, bm, cm, d, wg, bg, wo):
    bsz, seq, _ = u.shape
    tt = min(S5_TIME_TILE, seq)
    rows = BATCH_GROUP * tt
    const = lambda shape: pl.BlockSpec(shape, lambda b, t: (0,) * len(shape))
    r = jnp.arange(rows)
    perm = ((r[:, None] % BATCH_GROUP) * tt + r[:, None] // BATCH_GROUP == r[None, :]).astype(BF16)
    return pl.pallas_call(
        functools.partial(_s5_kernel, tt=tt),
        grid=(bsz // BATCH_GROUP, seq // tt),
        in_specs=[pl.BlockSpec((BATCH_GROUP, tt, S5_WIDTH), lambda b, t: (b, t, 0)),
                  pl.BlockSpec((BATCH_GROUP, S5_LANES), lambda b, t: (b, 0)),
                  const((BATCH_GROUP, S5_LANES)), const((rows, rows)), const((rows, rows)),
                  const(bm.shape), const(cm.shape), const((1, S5_WIDTH)),
                  const((S5_WIDTH, S5_WIDTH)), const((1, S5_WIDTH)), const((S5_WIDTH, D_MODEL))],
        out_specs=[pl.BlockSpec((BATCH_GROUP, tt, D_MODEL), lambda b, t: (b, t, 0)),
                   pl.BlockSpec((BATCH_GROUP, S5_LANES), lambda b, t: (b, 0))],
        out_shape=[jax.ShapeDtypeStruct((bsz, seq, D_MODEL), BF16),
                   jax.ShapeDtypeStruct((bsz, S5_LANES), F32)],
        scratch_shapes=[pltpu.VMEM((rows, S5_LANES), F32),
                        pltpu.VMEM((BATCH_GROUP, S5_LANES), F32)],
        compiler_params=_cparams("parallel", "arbitrary"),
        name="s5",
    )(u, h0, a_lanes, perm, perm.T, bm, cm, d, wg, bg, wo)


def _s5_params(a_re, a_im, log_dt, b_re, b_im, c_re, c_im):
    a_re = a_re.astype(F32)
    a_im = a_im.astype(F32)
    dt = jnp.exp(log_dt.astype(F32))[:, None]
    mag = jnp.exp(a_re * dt)
    ang = a_im * dt
    ab_re = mag * jnp.cos(ang)
    ab_im = mag * jnp.sin(ang)
    den = a_re * a_re + a_im * a_im
    nr = ab_re - 1.0
    ni = ab_im
    f_re = (nr * a_re + ni * a_im) / den
    f_im = (ni * a_re - nr * a_im) / den
    b_re = b_re.astype(F32)
    b_im = b_im.astype(F32)
    bb_re = f_re[..., None] * b_re - f_im[..., None] * b_im
    bb_im = f_re[..., None] * b_im + f_im[..., None] * b_re
    gpc = S5_GROUPS // S5_CHUNKS
    eye = jnp.eye(gpc, dtype=F32)

    def lanes(x):
        return x.reshape(S5_CHUNKS, gpc * S5_STATE)

    a_lanes = jnp.concatenate([lanes(ab_re), lanes(ab_im)], axis=1).reshape(1, S5_LANES)
    a_lanes = jnp.broadcast_to(a_lanes, (BATCH_GROUP, S5_LANES))

    def in_blocks(bb):
        bb = bb.reshape(S5_CHUNKS, gpc, S5_STATE, S5_GROUP)
        return jnp.einsum('cgpj,gh->cgjhp', bb, eye).reshape(S5_CHUNKS, gpc * S5_GROUP, gpc * S5_STATE)

    bm = jnp.concatenate([in_blocks(bb_re), in_blocks(bb_im)], axis=2).astype(BF16)

    def out_blocks(cc):
        cc = cc.astype(F32).reshape(S5_CHUNKS, gpc, S5_GROUP, S5_STATE)
        return jnp.einsum('cgjp,gh->cgphj', cc, eye).reshape(S5_CHUNKS, gpc * S5_STATE, gpc * S5_GROUP)

    cm = jnp.concatenate([out_blocks(c_re), -out_blocks(c_im)], axis=1).astype(BF16)
    return a_lanes, bm, cm


def _s5_state_to_lanes(h_re, h_im):
    bsz = h_re.shape[0]
    re = h_re.astype(F32).reshape(bsz, S5_CHUNKS, -1)
    im = h_im.astype(F32).reshape(bsz, S5_CHUNKS, -1)
    return jnp.concatenate([re, im], axis=2).reshape(bsz, S5_LANES)


def _s5_state_from_lanes(h):
    bsz = h.shape[0]
    h = h.reshape(bsz, S5_CHUNKS, 2, S5_GROUPS // S5_CHUNKS, S5_STATE)
    return (h[:, :, 0].reshape(bsz, S5_GROUPS, S5_STATE), h[:, :, 1].reshape(bsz, S5_GROUPS, S5_STATE))


def _ret_kernel(q_ref, k_ref, v_ref, g_ref, s0_ref, dm_ref, xi_ref, zeta_ref, wo_ref,
                yb_ref, st_ref, s_ref, *, block_decay):
    si = pl.program_id(1)

    @pl.when(si == 0)
    def _():
        s_ref[...] = s0_ref[...]

    gated = []
    for h in range(RET_HEADS):
        qh = q_ref[:, h * RET_DK:(h + 1) * RET_DK]
        kh = k_ref[:, h * RET_DK:(h + 1) * RET_DK]
        vh = v_ref[:, h * RET_DV:(h + 1) * RET_DV]
        scores = lax.dot_general(qh, kh, (((1,), (1,)), ((), ())), preferred_element_type=F32) * dm_ref[h]
        state = s_ref[h]
        o = _bdot(scores.astype(BF16), vh) + _bdot(qh, state.astype(BF16)) * xi_ref[h]
        o = o * lax.rsqrt(jnp.mean(o * o, axis=-1, keepdims=True) + EPS)
        gh = g_ref[:, h * RET_DV:(h + 1) * RET_DV].astype(F32)
        gated.append((o * (gh * _sigmoid(gh))).astype(BF16))
        kz = (kh.astype(F32) * zeta_ref[h]).astype(BF16)
        kv = lax.dot_general(kz, vh, (((0,), (0,)), ((), ())), preferred_element_type=F32)
        s_ref[h] = block_decay[h] * state + kv
    yb_ref[...] = _bdot(jnp.concatenate(gated, axis=1), wo_ref[...]).astype(BF16)

    @pl.when(si == pl.num_programs(1) - 1)
    def _():
        st_ref[...] = s_ref[...]


def _ret_tables(seq):
    cl = min(CHUNK, seq)
    blk = min(RET_BLOCK, seq)
    log_g = jnp.log(1.0 - 2.0 ** (-5.0 - jnp.arange(RET_HEADS, dtype=F32)))
    idx = jnp.arange(blk, dtype=F32)
    diff = idx[:, None] - idx[None, :]
    cn = jnp.arange(blk)[:, None] // cl
    cm = jnp.arange(blk)[None, :] // cl
    expo = jnp.where(cm == cn, jnp.abs(diff), diff)
    dm = jnp.where(cm <= cn, jnp.exp(log_g[:, None, None] * expo[None]), 0.0)
    xi = jnp.exp(log_g[:, None] * (idx + 1.0)[None, :])[..., None]
    zeta = jnp.exp(log_g[:, None] * (blk - 1.0 - idx)[None, :])[..., None]
    block_decay = tuple(math.exp(math.log(1.0 - 2.0 ** (-5.0 - h)) * blk) for h in range(RET_HEADS))
    return blk, dm, xi, zeta, block_decay


def _retention(q, k, v, g, s0, w_ret_out_b):
    bsz, seq, _ = q.shape
    blk, dm, xi, zeta, block_decay = _ret_tables(seq)
    row = lambda w: pl.BlockSpec((None, blk, w), lambda b, s: (b, s, 0))
    const = lambda shape: pl.BlockSpec(shape, lambda b, s: (0,) * len(shape))
    state = pl.BlockSpec((None, RET_HEADS, RET_DK, RET_DV), lambda b, s: (b, 0, 0, 0))
    return pl.pallas_call(
        functools.partial(_ret_kernel, block_decay=block_decay),
        grid=(bsz, seq // blk),
        in_specs=[row(RET_QK), row(RET_QK), row(RET_V), row(RET_V), state,
                  const(dm.shape), const(xi.shape), const(zeta.shape), const((RET_V, D_MODEL))],
        out_specs=[row(D_MODEL), state],
        out_shape=[jax.ShapeDtypeStruct((bsz, seq, D_MODEL), BF16),
                   jax.ShapeDtypeStruct((bsz, RET_HEADS, RET_DK, RET_DV), F32)],
        scratch_shapes=[pltpu.VMEM((RET_HEADS, RET_DK, RET_DV), F32)],
        compiler_params=_cparams("parallel", "arbitrary"),
        name="retention",
    )(q, k, v, g, s0, dm, xi, zeta, w_ret_out_b)


def _store_row_tiles(ref, val):
    n = val.shape[0]
    for j in range(ROW_TILES):
        ref[pl.ds(j, n, stride=ROW_TILES), :] = val[:, j * LANE:(j + 1) * LANE]


def _load_row_tiles(ref, first_row, n):
    return jnp.concatenate([ref[pl.ds(first_row * ROW_TILES + j, n, stride=ROW_TILES), :]
                            for j in range(ROW_TILES)], axis=1)


def _merge_kernel(x_ref, ya_ref, yb_ref, ga_ref, gb_ref, gt_ref, sh_ref, sc_ref, g2_ref, wo_ref,
                  wr_ref, br_ref, x1_ref, n2_ref, re_ref, wc_ref, cnt_ref):
    merged = _sigmoid(ga_ref[...]) * ya_ref[...] + _sigmoid(gb_ref[...]) * yb_ref[...]
    x1 = x_ref[...] + gt_ref[...] * _bdot(merged, wo_ref[...])
    x1_ref[...] = x1
    n2 = x1 * lax.rsqrt(jnp.mean(x1 * x1, axis=-1, keepdims=True) + EPS) * g2_ref[...]
    n2 = n2 * (1.0 + sc_ref[...]) + sh_ref[...]
    _store_row_tiles(n2_ref, n2)

    nt_dot = lambda a, b: lax.dot_general(a, b, (((1,), (1,)), ((), ())), preferred_element_type=F32)
    n2_hi = n2.astype(BF16)
    n2_lo = (n2 - n2_hi.astype(F32)).astype(BF16)
    lt = (nt_dot(wr_ref[0], n2_hi) + nt_dot(wr_ref[0], n2_lo) + nt_dot(wr_ref[1], n2_hi)) + br_ref[...]
    tl = lt.shape[1]
    iota = lax.broadcasted_iota(jnp.int32, (8, tl), 0)
    gl = lt[0:8]
    gmax = jnp.max(gl, axis=0, keepdims=True)
    gi = jnp.min(jnp.where(gl == gmax, iota, 8), axis=0, keepdims=True)
    gw = 1.0 / jnp.sum(jnp.exp(gl - gmax), axis=0, keepdims=True)
    el = jnp.zeros((8, tl), F32)
    for g in range(MOE_GROUPS):
        el = jnp.where(gi == g, lt[8 * (g + 1):8 * (g + 2)], el)
    m1 = jnp.max(el, axis=0, keepdims=True)
    i1 = jnp.min(jnp.where(el == m1, iota, 8), axis=0, keepdims=True)
    el2 = jnp.where(iota == i1, -jnp.inf, el)
    m2 = jnp.max(el2, axis=0, keepdims=True)
    i2 = jnp.min(jnp.where(el2 == m2, iota, 8), axis=0, keepdims=True)
    e21 = jnp.exp(m2 - m1)
    w1 = gw / (1.0 + e21)
    w2 = w1 * e21
    e1 = gi * MOE_EXPERTS + i1
    e2 = gi * MOE_EXPERTS + i2
    re_ref[...] = jnp.where(iota == 0, e1, jnp.where(iota == 1, e2, 0))
    rw = jnp.where(iota == 0, w1, jnp.where(iota == 1, w2, 0.0))
    eye = (lax.broadcasted_iota(jnp.int32, (8, LANE), 0)
           == lax.broadcasted_iota(jnp.int32, (8, LANE), 1)).astype(F32)
    wc_ref[...] = lax.dot_general(rw, eye, (((0,), (0,)), ((), ())),
                                  precision=lax.Precision.HIGHEST, preferred_element_type=F32)
    ids = lax.broadcasted_iota(jnp.int32, (N_EXPERTS, tl), 0)
    hits = (ids == e1).astype(F32) + (ids == e2).astype(F32)
    cnt_ref[...] = jnp.broadcast_to(jnp.sum(hits, axis=1, keepdims=True), (N_EXPERTS, LANE)).astype(jnp.int32)


def _merge(x, ya, yb, ga, gb, gate1, shift2, scale2, g2, w_out_b, wr, br):
    bsz, seq, _ = x.shape
    tl = min(TOKEN_TILE, seq)
    row = pl.BlockSpec((None, tl, D_MODEL), lambda b, t: (b, t, 0))
    vec = pl.BlockSpec((None, 1, D_MODEL), lambda b, t: (b, 0, 0))
    const = lambda shape: pl.BlockSpec(shape, lambda b, t: (0,) * len(shape))
    route = pl.BlockSpec((None, 8, tl), lambda b, t: (b, 0, t))
    nt = seq // tl
    tiles = pl.BlockSpec((tl * ROW_TILES, LANE), lambda b, t: (b * nt + t, 0))
    return pl.pallas_call(
        _merge_kernel,
        grid=(bsz, nt),
        in_specs=[row, row, row, row, row, vec, vec, vec, const((1, D_MODEL)),
                  const((D_MODEL, D_MODEL)), const((2, ROUTE_ROWS, D_MODEL)), const((ROUTE_ROWS, 1))],
        out_specs=[row, tiles, route,
                   pl.BlockSpec((tl, LANE), lambda b, t: (b * nt + t, 0)),
                   pl.BlockSpec((None, N_EXPERTS, LANE), lambda b, t: (b * nt + t, 0, 0))],
        out_shape=[jax.ShapeDtypeStruct((bsz, seq, D_MODEL), F32),
                   jax.ShapeDtypeStruct((bsz * seq * ROW_TILES, LANE), F32),
                   jax.ShapeDtypeStruct((bsz, 8, seq), jnp.int32),
                   jax.ShapeDtypeStruct((bsz * seq, LANE), F32),
                   jax.ShapeDtypeStruct((bsz * nt, N_EXPERTS, LANE), jnp.int32)],
        compiler_params=_cparams("parallel", "parallel"),
        name="merge",
    )(x, ya, yb, ga, gb, gate1, shift2, scale2, g2, w_out_b, wr, br)


def _router_weights(w_rg, b_rg, w_re, b_re):
    wr = jnp.zeros((ROUTE_ROWS, D_MODEL), F32)
    wr = wr.at[0:MOE_GROUPS].set(w_rg.astype(F32).T)
    wr = wr.at[8:].set(jnp.transpose(w_re.astype(F32), (0, 2, 1)).reshape(N_EXPERTS, D_MODEL))
    br = jnp.full((ROUTE_ROWS,), -1e30, F32)
    br = br.at[0:MOE_GROUPS].set(b_rg.astype(F32))
    br = br.at[8:].set(b_re.astype(F32).reshape(N_EXPERTS))
    wr_hi = wr.astype(BF16)
    wr_lo = (wr - wr_hi.astype(F32)).astype(BF16)
    return jnp.stack([wr_hi, wr_lo]), br.reshape(ROUTE_ROWS, 1)


def _row_gather(idx_ref, src_hbm, buf, sem, slot, n):
    def body(pair, carry):
        for queue in range(DMA_QUEUES):
            r = pair * DMA_QUEUES + queue
            src = pl.multiple_of(idx_ref[0, r] * ROW_TILES, ROW_TILES)
            dst = pl.multiple_of(r * ROW_TILES, ROW_TILES)
            pltpu.make_async_copy(src_hbm.at[pl.ds(src, ROW_TILES), :],
                                  buf.at[slot, pl.ds(dst, ROW_TILES), :], sem.at[slot]).start(priority=queue)
        return carry
    lax.fori_loop(0, n // DMA_QUEUES, body, 0, unroll=DMA_UNROLL // DMA_QUEUES)


def _row_gather_wait(src_hbm, buf, sem, slot, n):
    pltpu.make_async_copy(src_hbm.at[pl.ds(0, n * ROW_TILES), :], buf.at[slot], sem.at[slot]).wait()


def _plan_kernel(e_ref, base_ref, c0_ref, tri_ref, pos_ref, carry_ref):
    @pl.when(pl.program_id(0) == 0)
    def _():
        carry_ref[...] = c0_ref[...]

    tb = e_ref.shape[1]
    ids = lax.broadcasted_iota(jnp.int32, (N_EXPERTS, tb), 0)
    carry = carry_ref[...]
    base = base_ref[...]
    out = []
    for k in range(2):
        hit = ids == e_ref[k:k + 1, :]
        incl = _bdot(jnp.where(hit, 1.0, 0.0).astype(BF16), tri_ref[...])
        out.append(jnp.sum(jnp.where(hit, incl - 1.0 + carry + base, 0.0), axis=0, keepdims=True))
        carry = carry + incl[:, tb - 1:tb]
    carry_ref[...] = carry
    pos_ref[...] = jnp.concatenate(out, axis=1).astype(jnp.int32)


def _plan(route_e, pad_start, placed):
    bsz, _, seq = route_e.shape
    tb = min(ROUTE_TILE, seq)
    nb = seq // tb
    r = jnp.arange(tb)
    tri = (r[:, None] <= r[None, :]).astype(BF16)
    col = pl.BlockSpec((N_EXPERTS, 1), lambda i: (0, 0))
    return pl.pallas_call(
        _plan_kernel,
        grid=(bsz * nb,),
        in_specs=[pl.BlockSpec((None, 8, tb), lambda i: (i // nb, 0, i % nb)), col, col,
                  pl.BlockSpec((tb, tb), lambda i: (0, 0))],
        out_specs=pl.BlockSpec((None, 1, 2 * tb), lambda i: (i, 0, 0)),
        out_shape=jax.ShapeDtypeStruct((bsz * nb, 1, 2 * tb), jnp.int32),
        scratch_shapes=[pltpu.VMEM((N_EXPERTS, 1), F32)],
        compiler_params=_cparams("arbitrary"),
        name="plan",
    )(route_e, pad_start.astype(F32).reshape(N_EXPERTS, 1), placed.astype(F32).reshape(N_EXPERTS, 1), tri)


ZERO_CHUNKS = tuple(2 ** b for b in range(8, -1, -1))


def _dispatch_kernel(lo_ref, hi_ref, nu_ref, pos_ref, x_ref, xs_hbm, zero_ref, sem, *, tb, tm, n_tiles):
    def rows_of(k):
        def body(pair, carry):
            for queue in range(DMA_QUEUES):
                r = pair * DMA_QUEUES + queue
                src = pl.multiple_of(r * ROW_TILES, ROW_TILES)
                dst = pl.multiple_of(pos_ref[0, k * tb + r] * ROW_TILES, ROW_TILES)
                pltpu.make_async_copy(x_ref.at[pl.ds(src, ROW_TILES), :],
                                      xs_hbm.at[pl.ds(dst, ROW_TILES), :], sem.at[0]).start(priority=queue)
            return carry
        lax.fori_loop(0, tb // DMA_QUEUES, body, 0, unroll=DMA_UNROLL // DMA_QUEUES)

    rows_of(0)
    rows_of(1)

    def zero_copy(row, size):
        dst = pl.multiple_of(row * ROW_TILES, ROW_TILES)
        return pltpu.make_async_copy(zero_ref.at[pl.ds(0, size * ROW_TILES), :],
                                     xs_hbm.at[pl.ds(dst, size * ROW_TILES), :], sem.at[1])

    @pl.when(pl.program_id(0) == pl.num_programs(0) - 1)
    def _():
        zero_ref[...] = jnp.zeros_like(zero_ref)

        def per_expert(e, carry):
            lo = lo_ref[e]
            n = hi_ref[e] - lo
            for wait in (False, True):
                row = lo
                for size in ZERO_CHUNKS:
                    @pl.when((n & size) != 0)
                    def _(row=row, size=size, wait=wait):
                        zero_copy(row, size).wait() if wait else zero_copy(row, size).start()
                    row = row + (n & size)
            return carry
        lax.fori_loop(0, N_EXPERTS, per_expert, 0)

        chunk = min(ZERO_CHUNKS[0], tm)
        for wait in (False, True):
            def per_tile(t, carry, wait=wait):
                for c in range(tm // chunk):
                    cp = zero_copy(t * tm + c * chunk, chunk)
                    cp.wait() if wait else cp.start()
                return carry
            lax.fori_loop(nu_ref[0], n_tiles, per_tile, 0)

    for _ in range(2):
        pltpu.make_async_copy(x_ref, xs_hbm.at[pl.ds(0, tb * ROW_TILES), :], sem.at[0]).wait()


def _dispatch(pos, n2_tiles, pad_lo, pad_hi, n_used, n_rows, tm):
    n_blocks, _, two_tb = pos.shape
    tb = two_tb // 2
    grid_spec = pltpu.PrefetchScalarGridSpec(
        num_scalar_prefetch=3,
        grid=(n_blocks,),
        in_specs=[pl.BlockSpec((None, 1, two_tb), lambda i, lo, hi, nu: (i, 0, 0), memory_space=pltpu.SMEM),
                  pl.BlockSpec((tb * ROW_TILES, LANE), lambda i, lo, hi, nu: (i, 0))],
        out_specs=pl.BlockSpec(memory_space=pl.ANY),
        scratch_shapes=[pltpu.VMEM((ZERO_CHUNKS[0] * ROW_TILES, LANE), F32), pltpu.SemaphoreType.DMA((2,))],
    )
    return pl.pallas_call(
        functools.partial(_dispatch_kernel, tb=tb, tm=tm, n_tiles=n_rows // tm),
        grid_spec=grid_spec,
        out_shape=jax.ShapeDtypeStruct((n_rows * ROW_TILES, LANE), F32),
        compiler_params=_cparams("arbitrary"),
        name="dispatch",
    )(pad_lo, pad_hi, n_used, pos, n2_tiles)


def _expert_kernel(te_ref, nu_ref, x_ref, w1_ref, w3_ref, w2_ref, y_ref, *, tm):
    @pl.when(pl.program_id(0) < nu_ref[0])
    def _():
        xb = _load_row_tiles(x_ref, 0, tm).astype(BF16)
        a = _bdot(xb, w1_ref[...])
        h = a * _sigmoid(a) * _bdot(xb, w3_ref[...])
        _store_row_tiles(y_ref, _bdot(h.astype(BF16), w2_ref[...]))

    @pl.when(pl.program_id(0) >= nu_ref[0])
    def _():
        y_ref[...] = jnp.zeros_like(y_ref)


def _experts(xs, tile_expert, n_used, w1b, w3b, w2b, tm):
    n_tiles = xs.shape[0] // (tm * ROW_TILES)
    tile = lambda i, te, nu: (jnp.maximum(jnp.minimum(i, nu[0] - 1), 0), 0)
    grid_spec = pltpu.PrefetchScalarGridSpec(
        num_scalar_prefetch=2,
        grid=(n_tiles,),
        in_specs=[pl.BlockSpec((tm * ROW_TILES, LANE), tile),
                  pl.BlockSpec((None, D_MODEL, MOE_FF), lambda i, te, nu: (te[i], 0, 0)),
                  pl.BlockSpec((None, D_MODEL, MOE_FF), lambda i, te, nu: (te[i], 0, 0)),
                  pl.BlockSpec((None, MOE_FF, D_MODEL), lambda i, te, nu: (te[i], 0, 0))],
        out_specs=pl.BlockSpec((tm * ROW_TILES, LANE), lambda i, te, nu: (i, 0)),
    )
    return pl.pallas_call(
        functools.partial(_expert_kernel, tm=tm),
        grid_spec=grid_spec,
        out_shape=jax.ShapeDtypeStruct(xs.shape, F32),
        compiler_params=_cparams("arbitrary"),
        name="experts",
    )(tile_expert, n_used, xs, w1b, w3b, w2b)


def _expert_layout(counts, tm, n_tok):
    n_tiles = 2 * n_tok // tm + N_EXPERTS
    padded = ((counts + tm - 1) // tm) * tm
    pad_end = jnp.cumsum(padded)
    pad_start = pad_end - padded
    n_used = pad_end[-1] // tm
    tile_start = jnp.minimum(jnp.arange(n_tiles, dtype=jnp.int32), n_used - 1) * tm
    tile_expert = jnp.sum((pad_end[None, :] <= tile_start[:, None]).astype(jnp.int32), axis=1)
    return (pad_start, pad_start + counts, pad_end, tile_expert.astype(jnp.int32),
            n_used.astype(jnp.int32).reshape(1), n_tiles * tm)


def _combine_kernel(cur_ref, nxt_ref, x1_ref, wc_ref, gt_ref, fn_ref, y_hbm, o_ref, ybuf, sem, *, tl):
    i = pl.program_id(0)
    slot = i % 2

    @pl.when(i == 0)
    def _():
        _row_gather(cur_ref, y_hbm, ybuf, sem, 0, 2 * tl)

    @pl.when(i + 1 < pl.num_programs(0))
    def _():
        _row_gather(nxt_ref, y_hbm, ybuf, sem, 1 - slot, 2 * tl)

    _row_gather_wait(y_hbm, ybuf, sem, slot, 2 * tl)
    rows = ybuf.at[slot]
    moe = wc_ref[:, 0:1] * _load_row_tiles(rows, 0, tl) + wc_ref[:, 1:2] * _load_row_tiles(rows, tl, tl)
    x2 = x1_ref[...] + gt_ref[...] * moe
    o_ref[...] = x2 * lax.rsqrt(jnp.mean(x2 * x2, axis=-1, keepdims=True) + EPS) * fn_ref[...]


def _combine(x1_flat, pos, y_rows, wcol, gate2, final_norm, seq):
    n_tok = x1_flat.shape[0]
    n_tiles, _, two_tl = pos.shape
    tl = two_tl // 2
    last = n_tiles - 1
    return pl.pallas_call(
        functools.partial(_combine_kernel, tl=tl),
        grid=(n_tiles,),
        in_specs=[pl.BlockSpec((None, 1, 2 * tl), lambda i: (i, 0, 0), memory_space=pltpu.SMEM),
                  pl.BlockSpec((None, 1, 2 * tl), lambda i: (jnp.minimum(i + 1, last), 0, 0),
                               memory_space=pltpu.SMEM),
                  pl.BlockSpec((tl, D_MODEL), lambda i: (i, 0)),
                  pl.BlockSpec((tl, LANE), lambda i: (i, 0)),
                  pl.BlockSpec((None, 1, D_MODEL), lambda i: ((i * tl) // seq, 0, 0)),
                  pl.BlockSpec((1, D_MODEL), lambda i: (0, 0)),
                  pl.BlockSpec(memory_space=pl.ANY)],
        out_specs=pl.BlockSpec((tl, D_MODEL), lambda i: (i, 0)),
        out_shape=jax.ShapeDtypeStruct((n_tok, D_MODEL), F32),
        scratch_shapes=[pltpu.VMEM((2, 2 * tl * ROW_TILES, LANE), F32), pltpu.SemaphoreType.DMA((2,))],
        compiler_params=_cparams("arbitrary"),
        name="combine",
    )(pos, pos, x1_flat, wcol, gate2, final_norm, y_rows)


def _rope_tables(seq, offset):
    half = RET_DK // 2
    theta = 1.0 / (ROPE_BASE ** jnp.linspace(0.0, 1.0, half, dtype=F32))
    pos = offset + jnp.arange(seq)
    ang = pos.astype(F32)[:, None] * theta[None, :]
    cos = jnp.cos(ang)
    sin = jnp.sin(ang)
    return jnp.concatenate([cos, cos], axis=1), jnp.concatenate([-sin, sin], axis=1)


def _layer(x, mod, h0, s0, offset, p, final_norm, expert_tile):
    bsz, seq, _ = x.shape
    sh1, sc1, gt1, sh2, sc2, gt2 = [m.reshape(bsz, 1, D_MODEL) for m in jnp.split(mod, 6, axis=-1)]
    cos2, sin2 = _rope_tables(seq, offset)
    u, q, k, v, g, ga, gb = _inproj(x, sh1, sc1, p['norm1'], p['w_in'], cos2, sin2)
    ya, h_t = _s5(u, h0, p['a_lanes'], p['bm'], p['cm'], p['d'], p['w_glu'], p['b_glu'], p['w_s5_out'])
    yb, s_t = _retention(q, k, v, g, s0, p['w_ret_out'])
    x1, n2, route_e, wcol, cnt = _merge(x, ya, yb, ga, gb, gt1, sh2, sc2, p['norm2'], p['w_out'],
                                        p['wr'], p['br'])
    counts = jnp.sum(cnt[:, :, 0], axis=0)
    pad_start, pad_lo, pad_hi, tile_expert, n_used, n_rows = _expert_layout(counts, expert_tile, bsz * seq)
    pos = _plan(route_e, pad_start, jnp.zeros_like(counts))
    xs = _dispatch(pos, n2, pad_lo, pad_hi, n_used, n_rows, expert_tile)
    y_rows = _experts(xs, tile_expert, n_used, p['w1'], p['w3'], p['w2'], expert_tile)
    y = _combine(x1.reshape(bsz * seq, D_MODEL), pos, y_rows, wcol, gt2, final_norm, seq)
    return y.reshape(bsz, seq, D_MODEL), h_t, s_t


def kernel(x_prompt, x_sample, state_s5_re, state_s5_im, state_ret, c_prompt, c_sample, w_ada, b_ada, norm1, norm2, w_in, s5_a_re, s5_a_im, s5_log_dt, s5_b_re, s5_b_im, s5_c_re, s5_c_im, s5_d, s5_w_glu, s5_b_glu, w_s5_out, w_ret_out, w_out, w_rg, b_rg, w_re, b_re, w1, w3, w2, final_norm):
    depth = w_ada.shape[0]
    assert depth == 1
    bp = x_prompt.shape[0]
    bs, seq_s, _ = x_sample.shape
    l = 0
    a_lanes, bm, cm = _s5_params(s5_a_re[l], s5_a_im[l], s5_log_dt[l], s5_b_re[l], s5_b_im[l],
                                 s5_c_re[l], s5_c_im[l])
    wr, br = _router_weights(w_rg[l], b_rg[l], w_re[l], b_re[l])
    p = dict(
        norm1=norm1[l].astype(F32).reshape(1, D_MODEL), norm2=norm2[l].astype(F32).reshape(1, D_MODEL),
        w_in=w_in[l].astype(BF16), a_lanes=a_lanes, bm=bm, cm=cm,
        d=s5_d[l].astype(F32).reshape(1, S5_WIDTH), w_glu=s5_w_glu[l].astype(BF16),
        b_glu=s5_b_glu[l].astype(F32).reshape(1, S5_WIDTH), w_s5_out=w_s5_out[l].astype(BF16),
        w_ret_out=w_ret_out[l].astype(BF16), w_out=w_out[l].astype(BF16), wr=wr, br=br,
        w1=w1[l].astype(BF16).reshape(N_EXPERTS, D_MODEL, MOE_FF),
        w3=w3[l].astype(BF16).reshape(N_EXPERTS, D_MODEL, MOE_FF),
        w2=w2[l].astype(BF16).reshape(N_EXPERTS, MOE_FF, D_MODEL))
    fn = final_norm.astype(F32).reshape(1, D_MODEL)
    mod = _mod(jnp.concatenate([c_prompt, c_sample], axis=0).astype(F32), w_ada[l], b_ada[l])

    h0_p = jnp.zeros((bp, S5_LANES), F32)
    s0_p = jnp.zeros((bp, RET_HEADS, RET_DK, RET_DV), F32)
    y_p, h_p, s_p = _layer(x_prompt, mod[:bp], h0_p, s0_p, 0, p, fn, EXPERT_TILE)
    h0_s = _s5_state_to_lanes(state_s5_re[l], state_s5_im[l])
    y_s, h_s, s_s = _layer(x_sample, mod[bp:], h0_s, state_ret[l].astype(F32), PAST_LEN, p, fn,
                           min(EXPERT_TILE, 128))
    p_re, p_im = _s5_state_from_lanes(h_p)
    s_re, s_im = _s5_state_from_lanes(h_s)
    return (y_p, y_s, p_re[None], p_im[None], s_p[None], s_re[None], s_im[None], s_s[None])
```

```python
import functools
import math

import jax
import jax.numpy as jnp
from jax import lax
from jax.experimental import pallas as pl
from jax.experimental.pallas import tpu as pltpu

F32 = jnp.float32
BF16 = jnp.bfloat16

D_MODEL = 1024
PAST_LEN = 2048
CHUNK = 64
S5_WIDTH = 512
S5_GROUP = 16
S5_GROUPS = 32
S5_STATE = 64
S5_LANES = 2 * S5_GROUPS * S5_STATE
S5_CHUNKS = 4
RET_HEADS = 4
RET_DK = 128
RET_DV = 256
RET_QK = RET_HEADS * RET_DK
RET_V = RET_HEADS * RET_DV
ROPE_BASE = 10000.0
MOE_GROUPS = 4
MOE_EXPERTS = 8
N_EXPERTS = MOE_GROUPS * MOE_EXPERTS
PAIRS_PER_GROUP = MOE_EXPERTS * (MOE_EXPERTS - 1) // 2
N_PAIRS = 128
MOE_FF = 256
EPS = 1e-6
IN_WIDTH = S5_WIDTH + 2 * RET_QK + 2 * RET_V + 2 * D_MODEL
ROUTE_ROWS = 8 * (1 + MOE_GROUPS)

BATCH_GROUP = 8
TOKEN_TILE = 512
S5_TIME_TILE = 64
RET_BLOCK = 256
EXPERT_TILE = 256
ROUTE_TILE = 512
VMEM_LIMIT = 56 * 1024 * 1024
LANE = 128
SUBLANE = 8
ROW_TILES = D_MODEL // LANE
DMA_UNROLL = 8
DMA_QUEUES = 2


def _cparams(*sem):
    return pltpu.CompilerParams(dimension_semantics=sem, vmem_limit_bytes=VMEM_LIMIT)


def _bdot(a, b):
    return jnp.dot(a, b, preferred_element_type=F32)


def _sigmoid(x):
    return 0.5 * jnp.tanh(0.5 * x) + 0.5


def _mod_kernel(c_ref, w_ref, b_ref, o_ref):
    c = c_ref[...]
    a = (c * _sigmoid(c)).astype(BF16)
    o_ref[...] = _bdot(a, w_ref[...].astype(BF16)) + b_ref[...]


def _mod(c, w_ada, b_ada):
    n = c.shape[0]
    return pl.pallas_call(
        _mod_kernel,
        grid=(6,),
        in_specs=[pl.BlockSpec((n, D_MODEL), lambda j: (0, 0)),
                  pl.BlockSpec((D_MODEL, D_MODEL), lambda j: (0, j)),
                  pl.BlockSpec((1, D_MODEL), lambda j: (0, j))],
        out_specs=pl.BlockSpec((n, D_MODEL), lambda j: (0, j)),
        out_shape=jax.ShapeDtypeStruct((n, 6 * D_MODEL), F32),
        compiler_params=_cparams("parallel"),
        name="mod",
    )(c, w_ada, b_ada.reshape(1, -1))


def _rope(x, cos2, sin2):
    return x * cos2 + pltpu.roll(x, RET_DK // 2, 1) * sin2


def _inproj_kernel(x_ref, sh_ref, sc_ref, g_ref, w_ref, cos_ref, sin_ref,
                   u_ref, q_ref, k_ref, v_ref, gs_ref, ga_ref, gb_ref):
    x = x_ref[...]
    n = x * lax.rsqrt(jnp.mean(x * x, axis=-1, keepdims=True) + EPS) * g_ref[...]
    nb = (n * (1.0 + sc_ref[...]) + sh_ref[...]).astype(BF16)
    cos2 = cos_ref[...]
    sin2 = sin_ref[...]
    o = 0
    u_ref[...] = _bdot(nb, w_ref[:, o:o + S5_WIDTH]).astype(BF16)
    o += S5_WIDTH
    for h in range(RET_HEADS):
        qh = _bdot(nb, w_ref[:, o + h * RET_DK:o + (h + 1) * RET_DK])
        q_ref[:, h * RET_DK:(h + 1) * RET_DK] = _rope(qh, cos2, sin2).astype(BF16)
    o += RET_QK
    for h in range(RET_HEADS):
        kh = _bdot(nb, w_ref[:, o + h * RET_DK:o + (h + 1) * RET_DK])
        k_ref[:, h * RET_DK:(h + 1) * RET_DK] = (_rope(kh, cos2, sin2) * (RET_DK ** -0.5)).astype(BF16)
    o += RET_QK
    for ref in (v_ref, gs_ref, ga_ref, gb_ref):
        ref[...] = _bdot(nb, w_ref[:, o:o + D_MODEL]).astype(BF16)
        o += D_MODEL


def _inproj(x, shift, scale, g1, w_in_b, cos2, sin2):
    bsz, seq, _ = x.shape
    tl = min(TOKEN_TILE, seq)
    row = lambda w: pl.BlockSpec((None, tl, w), lambda b, t: (b, t, 0))
    vec = pl.BlockSpec((None, 1, D_MODEL), lambda b, t: (b, 0, 0))
    shapes = [S5_WIDTH, RET_QK, RET_QK, RET_V, RET_V, D_MODEL, D_MODEL]
    return pl.pallas_call(
        _inproj_kernel,
        grid=(bsz, seq // tl),
        in_specs=[row(D_MODEL), vec, vec,
                  pl.BlockSpec((1, D_MODEL), lambda b, t: (0, 0)),
                  pl.BlockSpec((D_MODEL, IN_WIDTH), lambda b, t: (0, 0), pipeline_mode=pl.Buffered(1)),
                  pl.BlockSpec((tl, RET_DK), lambda b, t: (t, 0)),
                  pl.BlockSpec((tl, RET_DK), lambda b, t: (t, 0))],
        out_specs=[row(w) for w in shapes],
        out_shape=[jax.ShapeDtypeStruct((bsz, seq, w), BF16) for w in shapes],
        compiler_params=_cparams("parallel", "parallel"),
        name="inproj",
    )(x, shift, scale, g1, w_in_b, cos2, sin2)


def _gelu_tanh(y):
    return 0.5 * y * (1.0 + jnp.tanh(math.sqrt(2.0 / math.pi) * (y + 0.044715 * (y * y * y))))


def _s5_kernel(u_ref, h0_ref, a_ref, pm_ref, pt_ref, bm_ref, cm_ref, d_ref, wg_ref, bg_ref, wo_ref,
               ya_ref, ht_ref, bu_ref, hs_ref, *, tt):
    ti = pl.program_id(1)
    rows = BATCH_GROUP * tt
    half = S5_LANES // (2 * S5_CHUNKS)

    @pl.when(ti == 0)
    def _():
        hs_ref[...] = h0_ref[...]

    u2 = _bdot(pm_ref[...], u_ref[...].reshape(rows, S5_WIDTH)).astype(BF16)
    kc = S5_WIDTH // S5_CHUNKS
    ys = []
    for c in range(S5_CHUNKS):
        lanes = slice(c * 2 * half, (c + 1) * 2 * half)
        lre = slice(c * 2 * half, c * 2 * half + half)
        lim = slice(c * 2 * half + half, (c + 1) * 2 * half)
        bu_ref[:, lanes] = _bdot(u2[:, c * kc:(c + 1) * kc], bm_ref[c])
        are = a_ref[:, lre]
        aim = a_ref[:, lim]
        hre = hs_ref[:, lre]
        him = hs_ref[:, lim]
        for t in range(tt):
            rsel = slice(t * BATCH_GROUP, (t + 1) * BATCH_GROUP)
            hre, him = (are * hre - aim * him + bu_ref[rsel, lre],
                        are * him + aim * hre + bu_ref[rsel, lim])
            bu_ref[rsel, lre] = hre
            bu_ref[rsel, lim] = him
        hs_ref[:, lre] = hre
        hs_ref[:, lim] = him
        ys.append(_bdot(bu_ref[:, lanes].astype(BF16), cm_ref[c]))
    y = jnp.concatenate(ys, axis=1) + d_ref[...] * u2.astype(F32)
    z = _gelu_tanh(y)
    gl = _bdot(z.astype(BF16), wg_ref[...]) + bg_ref[...]
    o = (z * _sigmoid(gl)).astype(BF16)
    ob = _bdot(pt_ref[...], o).astype(BF16)
    ya_ref[...] = _bdot(ob, wo_ref[...]).reshape(BATCH_GROUP, tt, D_MODEL).astype(BF16)

    @pl.when(ti == pl.num_programs(1) - 1)
    def _():
        ht_ref[...] = hs_ref[...]


def _s5(u, h0, a_lanes, bm, cm, d, wg, bg, wo):
    bsz, seq, _ = u.shape
    tt = min(S5_TIME_TILE, seq)
    rows = BATCH_GROUP * tt
    const = lambda shape: pl.BlockSpec(shape, lambda b, t: (0,) * len(shape))
    r = jnp.arange(rows)
    perm = ((r[:, None] % BATCH_GROUP) * tt + r[:, None] // BATCH_GROUP == r[None, :]).astype(BF16)
    return pl.pallas_call(
        functools.partial(_s5_kernel, tt=tt),
        grid=(bsz // BATCH_GROUP, seq // tt),
        in_specs=[pl.BlockSpec((BATCH_GROUP, tt, S5_WIDTH), lambda b, t: (b, t, 0)),
                  pl.BlockSpec((BATCH_GROUP, S5_LANES), lambda b, t: (b, 0)),
                  const((BATCH_GROUP, S5_LANES)), const((rows, rows)), const((rows, rows)),
                  const(bm.shape), const(cm.shape), const((1, S5_WIDTH)),
                  const((S5_WIDTH, S5_WIDTH)), const((1, S5_WIDTH)), const((S5_WIDTH, D_MODEL))],
        out_specs=[pl.BlockSpec((BATCH_GROUP, tt, D_MODEL), lambda b, t: (b, t, 0)),
                   pl.BlockSpec((BATCH_GROUP, S5_LANES), lambda b, t: (b, 0))],
        out_shape=[jax.ShapeDtypeStruct((bsz, seq, D_MODEL), BF16),
                   jax.ShapeDtypeStruct((bsz, S5_LANES), F32)],
        scratch_shapes=[pltpu.VMEM((rows, S5_LANES), F32),
                        pltpu.VMEM((BATCH_GROUP, S5_LANES), F32)],
        compiler_params=_cparams("parallel", "arbitrary"),
        name="s5",
    )(u, h0, a_lanes, perm, perm.T, bm, cm, d, wg, bg, wo)


def _s5_params(a_re, a_im, log_dt, b_re, b_im, c_re, c_im):
    a_re = a_re.astype(F32)
    a_im = a_im.astype(F32)
    dt = jnp.exp(log_dt.astype(F32))[:, None]
    mag = jnp.exp(a_re * dt)
    ang = a_im * dt
    ab_re = mag * jnp.cos(ang)
    ab_im = mag * jnp.sin(ang)
    den = a_re * a_re + a_im * a_im
    nr = ab_re - 1.0
    ni = ab_im
    f_re = (nr * a_re + ni * a_im) / den
    f_im = (ni * a_re - nr * a_im) / den
    b_re = b_re.astype(F32)
    b_im = b_im.astype(F32)
    bb_re = f_re[..., None] * b_re - f_im[..., None] * b_im
    bb_im = f_re[..., None] * b_im + f_im[..., None] * b_re
    gpc = S5_GROUPS // S5_CHUNKS
    eye = jnp.eye(gpc, dtype=F32)

    def lanes(x):
        return x.reshape(S5_CHUNKS, gpc * S5_STATE)

    a_lanes = jnp.concatenate([lanes(ab_re), lanes(ab_im)], axis=1).reshape(1, S5_LANES)
    a_lanes = jnp.broadcast_to(a_lanes, (BATCH_GROUP, S5_LANES))

    def in_blocks(bb):
        bb = bb.reshape(S5_CHUNKS, gpc, S5_STATE, S5_GROUP)
        return jnp.einsum('cgpj,gh->cgjhp', bb, eye).reshape(S5_CHUNKS, gpc * S5_GROUP, gpc * S5_STATE)

    bm = jnp.concatenate([in_blocks(bb_re), in_blocks(bb_im)], axis=2).astype(BF16)

    def out_blocks(cc):
        cc = cc.astype(F32).reshape(S5_CHUNKS, gpc, S5_GROUP, S5_STATE)
        return jnp.einsum('cgjp,gh->cgphj', cc, eye).reshape(S5_CHUNKS, gpc * S5_STATE, gpc * S5_GROUP)

    cm = jnp.concatenate([out_blocks(c_re), -out_blocks(c_im)], axis=1).astype(BF16)
    return a_lanes, bm, cm


def _s5_state_to_lanes(h_re, h_im):
    bsz = h_re.shape[0]
    re = h_re.astype(F32).reshape(bsz, S5_CHUNKS, -1)
    im = h_im.astype(F32).reshape(bsz, S5_CHUNKS, -1)
    return jnp.concatenate([re, im], axis=2).reshape(bsz, S5_LANES)


def _s5_state_from_lanes(h):
    bsz = h.shape[0]
    h = h.reshape(bsz, S5_CHUNKS, 2, S5_GROUPS // S5_CHUNKS, S5_STATE)
    return (h[:, :, 0].reshape(bsz, S5_GROUPS, S5_STATE), h[:, :, 1].reshape(bsz, S5_GROUPS, S5_STATE))


def _ret_kernel(q_ref, k_ref, v_ref, g_ref, s0_ref, dm_ref, xi_ref, zeta_ref, wo_ref,
                yb_ref, st_ref, s_ref, *, block_decay):
    si = pl.program_id(1)

    @pl.when(si == 0)
    def _():
        s_ref[...] = s0_ref[...]

    gated = []
    for h in range(RET_HEADS):
        qh = q_ref[:, h * RET_DK:(h + 1) * RET_DK]
        kh = k_ref[:, h * RET_DK:(h + 1) * RET_DK]
        vh = v_ref[:, h * RET_DV:(h + 1) * RET_DV]
        scores = lax.dot_general(qh, kh, (((1,), (1,)), ((), ())), preferred_element_type=F32) * dm_ref[h]
        state = s_ref[h]
        o = _bdot(scores.astype(BF16), vh) + _bdot(qh, state.astype(BF16)) * xi_ref[h]
        o = o * lax.rsqrt(jnp.mean(o * o, axis=-1, keepdims=True) + EPS)
        gh = g_ref[:, h * RET_DV:(h + 1) * RET_DV].astype(F32)
        gated.append((o * (gh * _sigmoid(gh))).astype(BF16))
        kz = (kh.astype(F32) * zeta_ref[h]).astype(BF16)
        kv = lax.dot_general(kz, vh, (((0,), (0,)), ((), ())), preferred_element_type=F32)
        s_ref[h] = block_decay[h] * state + kv
    yb_ref[...] = _bdot(jnp.concatenate(gated, axis=1), wo_ref[...]).astype(BF16)

    @pl.when(si == pl.num_programs(1) - 1)
    def _():
        st_ref[...] = s_ref[...]


def _ret_tables(seq):
    cl = min(CHUNK, seq)
    blk = min(RET_BLOCK, seq)
    log_g = jnp.log(1.0 - 2.0 ** (-5.0 - jnp.arange(RET_HEADS, dtype=F32)))
    idx = jnp.arange(blk, dtype=F32)
    diff = idx[:, None] - idx[None, :]
    cn = jnp.arange(blk)[:, None] // cl
    cm = jnp.arange(blk)[None, :] // cl
    expo = jnp.where(cm == cn, jnp.abs(diff), diff)
    dm = jnp.where(cm <= cn, jnp.exp(log_g[:, None, None] * expo[None]), 0.0)
    xi = jnp.exp(log_g[:, None] * (idx + 1.0)[None, :])[..., None]
    zeta = jnp.exp(log_g[:, None] * (blk - 1.0 - idx)[None, :])[..., None]
    block_decay = tuple(math.exp(math.log(1.0 - 2.0 ** (-5.0 - h)) * blk) for h in range(RET_HEADS))
    return blk, dm, xi, zeta, block_decay


def _retention(q, k, v, g, s0, w_ret_out_b):
    bsz, seq, _ = q.shape
    blk, dm, xi, zeta, block_decay = _ret_tables(seq)
    row = lambda w: pl.BlockSpec((None, blk, w), lambda b, s: (b, s, 0))
    const = lambda shape: pl.BlockSpec(shape, lambda b, s: (0,) * len(shape))
    state = pl.BlockSpec((None, RET_HEADS, RET_DK, RET_DV), lambda b, s: (b, 0, 0, 0))
    return pl.pallas_call(
        functools.partial(_ret_kernel, block_decay=block_decay),
        grid=(bsz, seq // blk),
        in_specs=[row(RET_QK), row(RET_QK), row(RET_V), row(RET_V), state,
                  const(dm.shape), const(xi.shape), const(zeta.shape), const((RET_V, D_MODEL))],
        out_specs=[row(D_MODEL), state],
        out_shape=[jax.ShapeDtypeStruct((bsz, seq, D_MODEL), BF16),
                   jax.ShapeDtypeStruct((bsz, RET_HEADS, RET_DK, RET_DV), F32)],
        scratch_shapes=[pltpu.VMEM((RET_HEADS, RET_DK, RET_DV), F32)],
        compiler_params=_cparams("parallel", "arbitrary"),
        name="retention",
    )(q, k, v, g, s0, dm, xi, zeta, w_ret_out_b)


def _store_row_tiles(ref, val, span=ROW_TILES, offset=0):
    n = val.shape[0]
    for j in range(ROW_TILES):
        ref[pl.ds(offset + j, n, stride=span), :] = val[:, j * LANE:(j + 1) * LANE]


def _load_row_tiles(ref, n, span=ROW_TILES, offset=0):
    return jnp.concatenate([ref[pl.ds(offset + j, n, stride=span), :] for j in range(ROW_TILES)], axis=1)


def _merge_kernel(x_ref, ya_ref, yb_ref, ga_ref, gb_ref, gt_ref, sh_ref, sc_ref, g2_ref, wo_ref,
                  wr_ref, br_ref, x1_ref, n2_ref, re_ref, wc_ref, cnt_ref):
    merged = _sigmoid(ga_ref[...]) * ya_ref[...] + _sigmoid(gb_ref[...]) * yb_ref[...]
    x1 = x_ref[...] + gt_ref[...] * _bdot(merged, wo_ref[...])
    x1_ref[...] = x1
    n2 = x1 * lax.rsqrt(jnp.mean(x1 * x1, axis=-1, keepdims=True) + EPS) * g2_ref[...]
    n2 = n2 * (1.0 + sc_ref[...]) + sh_ref[...]
    _store_row_tiles(n2_ref, n2)

    nt_dot = lambda a, b: lax.dot_general(a, b, (((1,), (1,)), ((), ())), preferred_element_type=F32)
    n2_hi = n2.astype(BF16)
    n2_lo = (n2 - n2_hi.astype(F32)).astype(BF16)
    lt = (nt_dot(wr_ref[0], n2_hi) + nt_dot(wr_ref[0], n2_lo) + nt_dot(wr_ref[1], n2_hi)) + br_ref[...]
    tl = lt.shape[1]
    iota = lax.broadcasted_iota(jnp.int32, (8, tl), 0)
    gl = lt[0:8]
    gmax = jnp.max(gl, axis=0, keepdims=True)
    gi = jnp.min(jnp.where(gl == gmax, iota, 8), axis=0, keepdims=True)
    gw = 1.0 / jnp.sum(jnp.exp(gl - gmax), axis=0, keepdims=True)
    el = jnp.zeros((8, tl), F32)
    for g in range(MOE_GROUPS):
        el = jnp.where(gi == g, lt[8 * (g + 1):8 * (g + 2)], el)
    m1 = jnp.max(el, axis=0, keepdims=True)
    i1 = jnp.min(jnp.where(el == m1, iota, 8), axis=0, keepdims=True)
    el2 = jnp.where(iota == i1, -jnp.inf, el)
    m2 = jnp.max(el2, axis=0, keepdims=True)
    i2 = jnp.min(jnp.where(el2 == m2, iota, 8), axis=0, keepdims=True)
    e21 = jnp.exp(m2 - m1)
    w1 = gw / (1.0 + e21)
    w2 = w1 * e21
    lo = jnp.minimum(i1, i2)
    hi = jnp.maximum(i1, i2)
    pair = gi * PAIRS_PER_GROUP + lo * (MOE_EXPERTS - 1) - ((lo * (lo - 1)) >> 1) + (hi - lo - 1)
    re_ref[...] = jnp.where(iota == 0, pair, 0)
    rw = jnp.where(iota == 0, jnp.where(i1 < i2, w1, w2), jnp.where(iota == 1, jnp.where(i1 < i2, w2, w1), 0.0))
    eye = (lax.broadcasted_iota(jnp.int32, (8, LANE), 0)
           == lax.broadcasted_iota(jnp.int32, (8, LANE), 1)).astype(F32)
    wc_ref[...] = lax.dot_general(rw, eye, (((0,), (0,)), ((), ())),
                                  precision=lax.Precision.HIGHEST, preferred_element_type=F32)
    ids = lax.broadcasted_iota(jnp.int32, (N_PAIRS, tl), 0)
    hits = jnp.where(ids == pair, 1.0, 0.0)
    cnt_ref[...] = jnp.broadcast_to(jnp.sum(hits, axis=1, keepdims=True), (N_PAIRS, LANE)).astype(jnp.int32)


def _merge(x, ya, yb, ga, gb, gate1, shift2, scale2, g2, w_out_b, wr, br):
    bsz, seq, _ = x.shape
    tl = min(TOKEN_TILE, seq)
    row = pl.BlockSpec((None, tl, D_MODEL), lambda b, t: (b, t, 0))
    vec = pl.BlockSpec((None, 1, D_MODEL), lambda b, t: (b, 0, 0))
    const = lambda shape: pl.BlockSpec(shape, lambda b, t: (0,) * len(shape))
    route = pl.BlockSpec((None, 8, tl), lambda b, t: (b, 0, t))
    nt = seq // tl
    tiles = pl.BlockSpec((tl * ROW_TILES, LANE), lambda b, t: (b * nt + t, 0))
    return pl.pallas_call(
        _merge_kernel,
        grid=(bsz, nt),
        in_specs=[row, row, row, row, row, vec, vec, vec, const((1, D_MODEL)),
                  const((D_MODEL, D_MODEL)), const((2, ROUTE_ROWS, D_MODEL)), const((ROUTE_ROWS, 1))],
        out_specs=[row, tiles, route,
                   pl.BlockSpec((tl, LANE), lambda b, t: (b * nt + t, 0)),
                   pl.BlockSpec((None, N_PAIRS, LANE), lambda b, t: (b * nt + t, 0, 0))],
        out_shape=[jax.ShapeDtypeStruct((bsz, seq, D_MODEL), F32),
                   jax.ShapeDtypeStruct((bsz * seq * ROW_TILES, LANE), F32),
                   jax.ShapeDtypeStruct((bsz, 8, seq), jnp.int32),
                   jax.ShapeDtypeStruct((bsz * seq, LANE), F32),
                   jax.ShapeDtypeStruct((bsz * nt, N_PAIRS, LANE), jnp.int32)],
        compiler_params=_cparams("parallel", "parallel"),
        name="merge",
    )(x, ya, yb, ga, gb, gate1, shift2, scale2, g2, w_out_b, wr, br)


def _router_weights(w_rg, b_rg, w_re, b_re):
    wr = jnp.zeros((ROUTE_ROWS, D_MODEL), F32)
    wr = wr.at[0:MOE_GROUPS].set(w_rg.astype(F32).T)
    wr = wr.at[8:].set(jnp.transpose(w_re.astype(F32), (0, 2, 1)).reshape(N_EXPERTS, D_MODEL))
    br = jnp.full((ROUTE_ROWS,), -1e30, F32)
    br = br.at[0:MOE_GROUPS].set(b_rg.astype(F32))
    br = br.at[8:].set(b_re.astype(F32).reshape(N_EXPERTS))
    wr_hi = wr.astype(BF16)
    wr_lo = (wr - wr_hi.astype(F32)).astype(BF16)
    return jnp.stack([wr_hi, wr_lo]), br.reshape(ROUTE_ROWS, 1)


def _row_gather(idx_ref, src_hbm, buf, sem, slot, n, span):
    def body(pair, carry):
        for queue in range(DMA_QUEUES):
            r = pair * DMA_QUEUES + queue
            src = pl.multiple_of(idx_ref[0, r] * span, span)
            dst = pl.multiple_of(r * span, span)
            pltpu.make_async_copy(src_hbm.at[pl.ds(src, span), :],
                                  buf.at[slot, pl.ds(dst, span), :], sem.at[slot]).start(priority=queue)
        return carry
    lax.fori_loop(0, n // DMA_QUEUES, body, 0, unroll=DMA_UNROLL // DMA_QUEUES)


def _row_gather_wait(src_hbm, buf, sem, slot, n, span):
    pltpu.make_async_copy(src_hbm.at[pl.ds(0, n * span), :], buf.at[slot], sem.at[slot]).wait()


def _plan_kernel(e_ref, base_ref, c0_ref, tri_ref, pos_ref, carry_ref):
    @pl.when(pl.program_id(0) == 0)
    def _():
        carry_ref[...] = c0_ref[...]

    tb = e_ref.shape[1]
    hit = lax.broadcasted_iota(jnp.int32, (N_PAIRS, tb), 0) == e_ref[0:1, :]
    incl = _bdot(jnp.where(hit, 1.0, 0.0).astype(BF16), tri_ref[...])
    row = jnp.where(hit, incl - 1.0 + carry_ref[...] + base_ref[...], 0.0)
    pos_ref[...] = jnp.sum(row, axis=0, keepdims=True).astype(jnp.int32)
    carry_ref[...] = carry_ref[...] + incl[:, tb - 1:tb]


def _plan(route_e, pad_start, placed):
    bsz, _, seq = route_e.shape
    tb = min(ROUTE_TILE, seq)
    nb = seq // tb
    r = jnp.arange(tb)
    tri = (r[:, None] <= r[None, :]).astype(BF16)
    col = pl.BlockSpec((N_PAIRS, 1), lambda i: (0, 0))
    return pl.pallas_call(
        _plan_kernel,
        grid=(bsz * nb,),
        in_specs=[pl.BlockSpec((None, 8, tb), lambda i: (i // nb, 0, i % nb)), col, col,
                  pl.BlockSpec((tb, tb), lambda i: (0, 0))],
        out_specs=pl.BlockSpec((None, 1, tb), lambda i: (i, 0, 0)),
        out_shape=jax.ShapeDtypeStruct((bsz * nb, 1, tb), jnp.int32),
        scratch_shapes=[pltpu.VMEM((N_PAIRS, 1), F32)],
        compiler_params=_cparams("arbitrary"),
        name="plan",
    )(route_e, pad_start.astype(F32).reshape(N_PAIRS, 1), placed.astype(F32).reshape(N_PAIRS, 1), tri)


ZERO_CHUNKS = tuple(2 ** b for b in range(8, -1, -1))


def _dispatch_kernel(lo_ref, hi_ref, nu_ref, posa_ref, xa_ref, posb_ref, xb_ref, xs_hbm, zero_ref, sem,
                     *, tb, nb_rows, tm, n_tiles):
    last = pl.num_programs(0) - 1

    def scatter(pos_ref, x_ref, n):
        def body(pair, carry):
            for queue in range(DMA_QUEUES):
                r = pair * DMA_QUEUES + queue
                src = pl.multiple_of(r * ROW_TILES, ROW_TILES)
                dst = pl.multiple_of(pos_ref[0, r] * ROW_TILES, ROW_TILES)
                pltpu.make_async_copy(x_ref.at[pl.ds(src, ROW_TILES), :],
                                      xs_hbm.at[pl.ds(dst, ROW_TILES), :], sem.at[0]).start(priority=queue)
            return carry
        lax.fori_loop(0, n // DMA_QUEUES, body, 0, unroll=DMA_UNROLL // DMA_QUEUES)

    def scatter_wait(x_ref, n):
        pltpu.make_async_copy(x_ref, xs_hbm.at[pl.ds(0, n * ROW_TILES), :], sem.at[0]).wait()

    def zero_copy(row, size):
        dst = pl.multiple_of(row * ROW_TILES, ROW_TILES)
        return pltpu.make_async_copy(zero_ref.at[pl.ds(0, size * ROW_TILES), :],
                                     xs_hbm.at[pl.ds(dst, size * ROW_TILES), :], sem.at[1])

    @pl.when(pl.program_id(0) < last)
    def _():
        scatter(posa_ref, xa_ref, tb)
        scatter_wait(xa_ref, tb)

    @pl.when(pl.program_id(0) == last)
    def _():
        scatter(posb_ref, xb_ref, nb_rows)
        zero_ref[...] = jnp.zeros_like(zero_ref)

        def per_pair(e, carry):
            lo = lo_ref[e]
            n = hi_ref[e] - lo
            for wait in (False, True):
                row = lo
                for size in ZERO_CHUNKS:
                    @pl.when((n & size) != 0)
                    def _(row=row, size=size, wait=wait):
                        zero_copy(row, size).wait() if wait else zero_copy(row, size).start()
                    row = row + (n & size)
            return carry
        lax.fori_loop(0, N_PAIRS, per_pair, 0)

        chunk = min(ZERO_CHUNKS[0], tm)
        for wait in (False, True):
            def per_tile(t, carry, wait=wait):
                for c in range(tm // chunk):
                    cp = zero_copy(t * tm + c * chunk, chunk)
                    cp.wait() if wait else cp.start()
                return carry
            lax.fori_loop(nu_ref[0], n_tiles, per_tile, 0)
        scatter_wait(xb_ref, nb_rows)


def _dispatch(pos_a, rows_a, pos_b, rows_b, pad_lo, pad_hi, n_used, n_rows, tm):
    na, _, tb = pos_a.shape
    nb_rows = rows_b.shape[0] // ROW_TILES
    pos_b = pos_b.reshape(1, 1, nb_rows)
    block_a = lambda i, lo, hi, nu: (jnp.minimum(i, na - 1), 0, 0)
    grid_spec = pltpu.PrefetchScalarGridSpec(
        num_scalar_prefetch=3,
        grid=(na + 1,),
        in_specs=[pl.BlockSpec((None, 1, tb), block_a, memory_space=pltpu.SMEM),
                  pl.BlockSpec((tb * ROW_TILES, LANE), lambda i, lo, hi, nu: (jnp.minimum(i, na - 1), 0)),
                  pl.BlockSpec((None, 1, nb_rows), lambda i, lo, hi, nu: (0, 0, 0), memory_space=pltpu.SMEM),
                  pl.BlockSpec((nb_rows * ROW_TILES, LANE), lambda i, lo, hi, nu: (0, 0))],
        out_specs=pl.BlockSpec(memory_space=pl.ANY),
        scratch_shapes=[pltpu.VMEM((ZERO_CHUNKS[0] * ROW_TILES, LANE), F32), pltpu.SemaphoreType.DMA((2,))],
    )
    return pl.pallas_call(
        functools.partial(_dispatch_kernel, tb=tb, nb_rows=nb_rows, tm=tm, n_tiles=n_rows // tm),
        grid_spec=grid_spec,
        out_shape=jax.ShapeDtypeStruct((n_rows * ROW_TILES, LANE), F32),
        compiler_params=_cparams("arbitrary"),
        name="dispatch",
    )(pad_lo, pad_hi, n_used, pos_a, rows_a, pos_b, rows_b)


def _expert_kernel(ta_ref, tb_ref, nu_ref, x_ref, w1a_ref, w3a_ref, w2a_ref, w1b_ref, w3b_ref, w2b_ref,
                   y_ref, *, tm):
    @pl.when(pl.program_id(0) < nu_ref[0])
    def _():
        xb = _load_row_tiles(x_ref, tm).astype(BF16)
        for slot, (w1_ref, w3_ref, w2_ref) in enumerate(((w1a_ref, w3a_ref, w2a_ref),
                                                         (w1b_ref, w3b_ref, w2b_ref))):
            a = _bdot(xb, w1_ref[...])
            h = a * _sigmoid(a) * _bdot(xb, w3_ref[...])
            _store_row_tiles(y_ref, _bdot(h.astype(BF16), w2_ref[...]), span=2 * ROW_TILES,
                             offset=slot * ROW_TILES)

    @pl.when(pl.program_id(0) >= nu_ref[0])
    def _():
        y_ref[...] = jnp.zeros_like(y_ref)


def _experts(xs, tile_a, tile_b, n_used, w1b, w3b, w2b, tm):
    n_tiles = xs.shape[0] // (tm * ROW_TILES)
    tile = lambda i, ta, tb, nu: (jnp.maximum(jnp.minimum(i, nu[0] - 1), 0), 0)
    first = lambda i, ta, tb, nu: (ta[i], 0, 0)
    second = lambda i, ta, tb, nu: (tb[i], 0, 0)
    up = lambda which: pl.BlockSpec((None, D_MODEL, MOE_FF), which)
    down = lambda which: pl.BlockSpec((None, MOE_FF, D_MODEL), which)
    grid_spec = pltpu.PrefetchScalarGridSpec(
        num_scalar_prefetch=3,
        grid=(n_tiles,),
        in_specs=[pl.BlockSpec((tm * ROW_TILES, LANE), tile),
                  up(first), up(first), down(first), up(second), up(second), down(second)],
        out_specs=pl.BlockSpec((tm * 2 * ROW_TILES, LANE), lambda i, ta, tb, nu: (i, 0)),
    )
    return pl.pallas_call(
        functools.partial(_expert_kernel, tm=tm),
        grid_spec=grid_spec,
        out_shape=jax.ShapeDtypeStruct((2 * xs.shape[0], LANE), F32),
        compiler_params=_cparams("arbitrary"),
        name="experts",
    )(tile_a, tile_b, n_used, xs, w1b, w3b, w2b, w1b, w3b, w2b)


def _expert_layout(counts, tm, n_tok):
    n_tiles = n_tok // tm + MOE_GROUPS * PAIRS_PER_GROUP
    padded = ((counts + tm - 1) // tm) * tm
    pad_end = jnp.cumsum(padded)
    pad_start = pad_end - padded
    n_used = pad_end[-1] // tm
    tile_start = jnp.minimum(jnp.arange(n_tiles, dtype=jnp.int32), n_used - 1) * tm
    tile_pair = jnp.sum((pad_end[None, :] <= tile_start[:, None]).astype(jnp.int32), axis=1)
    tile_pair = jnp.minimum(tile_pair, MOE_GROUPS * PAIRS_PER_GROUP - 1)
    members = [(lo, hi) for lo in range(MOE_EXPERTS) for hi in range(lo + 1, MOE_EXPERTS)]
    lo_of = jnp.asarray([m[0] for m in members], jnp.int32)
    hi_of = jnp.asarray([m[1] for m in members], jnp.int32)
    group = tile_pair // PAIRS_PER_GROUP
    tile_a = group * MOE_EXPERTS + lo_of[tile_pair % PAIRS_PER_GROUP]
    tile_b = group * MOE_EXPERTS + hi_of[tile_pair % PAIRS_PER_GROUP]
    return (pad_start, pad_start + counts, pad_end, tile_a.astype(jnp.int32), tile_b.astype(jnp.int32),
            n_used.astype(jnp.int32).reshape(1), n_tiles * tm)


def _combine_kernel(cur_ref, nxt_ref, x1_ref, wc_ref, gt_ref, fn_ref, y_hbm, o_ref, ybuf, sem, *, tl):
    i = pl.program_id(0)
    slot = i % 2

    span = 2 * ROW_TILES

    @pl.when(i == 0)
    def _():
        _row_gather(cur_ref, y_hbm, ybuf, sem, 0, tl, span)

    @pl.when(i + 1 < pl.num_programs(0))
    def _():
        _row_gather(nxt_ref, y_hbm, ybuf, sem, 1 - slot, tl, span)

    _row_gather_wait(y_hbm, ybuf, sem, slot, tl, span)
    rows = ybuf.at[slot]
    moe = (wc_ref[:, 0:1] * _load_row_tiles(rows, tl, span, 0)
           + wc_ref[:, 1:2] * _load_row_tiles(rows, tl, span, ROW_TILES))
    x2 = x1_ref[...] + gt_ref[...] * moe
    o_ref[...] = x2 * lax.rsqrt(jnp.mean(x2 * x2, axis=-1, keepdims=True) + EPS) * fn_ref[...]


def _combine(x1_flat, pos, y_rows, wcol, gate2, final_norm, seq):
    n_tok = x1_flat.shape[0]
    n_tiles, _, tl = pos.shape
    last = n_tiles - 1
    return pl.pallas_call(
        functools.partial(_combine_kernel, tl=tl),
        grid=(n_tiles,),
        in_specs=[pl.BlockSpec((None, 1, tl), lambda i: (i, 0, 0), memory_space=pltpu.SMEM),
                  pl.BlockSpec((None, 1, tl), lambda i: (jnp.minimum(i + 1, last), 0, 0),
                               memory_space=pltpu.SMEM),
                  pl.BlockSpec((tl, D_MODEL), lambda i: (i, 0)),
                  pl.BlockSpec((tl, LANE), lambda i: (i, 0)),
                  pl.BlockSpec((None, 1, D_MODEL), lambda i: ((i * tl) // seq, 0, 0)),
                  pl.BlockSpec((1, D_MODEL), lambda i: (0, 0)),
                  pl.BlockSpec(memory_space=pl.ANY)],
        out_specs=pl.BlockSpec((tl, D_MODEL), lambda i: (i, 0)),
        out_shape=jax.ShapeDtypeStruct((n_tok, D_MODEL), F32),
        scratch_shapes=[pltpu.VMEM((2, 2 * tl * ROW_TILES, LANE), F32), pltpu.SemaphoreType.DMA((2,))],
        compiler_params=_cparams("arbitrary"),
        name="combine",
    )(pos, pos, x1_flat, wcol, gate2, final_norm, y_rows)


def _rope_tables(seq, offset):
    half = RET_DK // 2
    theta = 1.0 / (ROPE_BASE ** jnp.linspace(0.0, 1.0, half, dtype=F32))
    pos = offset + jnp.arange(seq)
    ang = pos.astype(F32)[:, None] * theta[None, :]
    cos = jnp.cos(ang)
    sin = jnp.sin(ang)
    return jnp.concatenate([cos, cos], axis=1), jnp.concatenate([-sin, sin], axis=1)


def _mixers(x, mod, h0, s0, offset, p):
    bsz, seq, _ = x.shape
    sh1, sc1, gt1, sh2, sc2, gt2 = [m.reshape(bsz, 1, D_MODEL) for m in jnp.split(mod, 6, axis=-1)]
    cos2, sin2 = _rope_tables(seq, offset)
    u, q, k, v, g, ga, gb = _inproj(x, sh1, sc1, p['norm1'], p['w_in'], cos2, sin2)
    ya, h_t = _s5(u, h0, p['a_lanes'], p['bm'], p['cm'], p['d'], p['w_glu'], p['b_glu'], p['w_s5_out'])
    yb, s_t = _retention(q, k, v, g, s0, p['w_ret_out'])
    x1, n2, route_e, wcol, cnt = _merge(x, ya, yb, ga, gb, gt1, sh2, sc2, p['norm2'], p['w_out'],
                                        p['wr'], p['br'])
    return dict(x1=x1.reshape(bsz * seq, D_MODEL), n2=n2, route=route_e, wcol=wcol,
                counts=jnp.sum(cnt[:, :, 0], axis=0), gate2=gt2, seq=seq, h=h_t, s=s_t)


def _moe(a, b, p, final_norm):
    n_tok = a['x1'].shape[0] + b['x1'].shape[0]
    pad_start, pad_lo, pad_hi, tile_a, tile_b, n_used, n_rows = _expert_layout(
        a['counts'] + b['counts'], EXPERT_TILE, n_tok)
    pos_a = _plan(a['route'], pad_start, jnp.zeros_like(a['counts']))
    pos_b = _plan(b['route'], pad_start, a['counts'])
    xs = _dispatch(pos_a, a['n2'], pos_b, b['n2'], pad_lo, pad_hi, n_used, n_rows, EXPERT_TILE)
    y_rows = _experts(xs, tile_a, tile_b, n_used, p['w1'], p['w3'], p['w2'], EXPERT_TILE)
    return [_combine(m['x1'], pos, y_rows, m['wcol'], m['gate2'], final_norm, m['seq'])
            for m, pos in ((a, pos_a), (b, pos_b))]


def kernel(x_prompt, x_sample, state_s5_re, state_s5_im, state_ret, c_prompt, c_sample, w_ada, b_ada, norm1, norm2, w_in, s5_a_re, s5_a_im, s5_log_dt, s5_b_re, s5_b_im, s5_c_re, s5_c_im, s5_d, s5_w_glu, s5_b_glu, w_s5_out, w_ret_out, w_out, w_rg, b_rg, w_re, b_re, w1, w3, w2, final_norm):
    depth = w_ada.shape[0]
    assert depth == 1
    bp = x_prompt.shape[0]
    bs, seq_s, _ = x_sample.shape
    l = 0
    a_lanes, bm, cm = _s5_params(s5_a_re[l], s5_a_im[l], s5_log_dt[l], s5_b_re[l], s5_b_im[l],
                                 s5_c_re[l], s5_c_im[l])
    wr, br = _router_weights(w_rg[l], b_rg[l], w_re[l], b_re[l])
    p = dict(
        norm1=norm1[l].astype(F32).reshape(1, D_MODEL), norm2=norm2[l].astype(F32).reshape(1, D_MODEL),
        w_in=w_in[l].astype(BF16), a_lanes=a_lanes, bm=bm, cm=cm,
        d=s5_d[l].astype(F32).reshape(1, S5_WIDTH), w_glu=s5_w_glu[l].astype(BF16),
        b_glu=s5_b_glu[l].astype(F32).reshape(1, S5_WIDTH), w_s5_out=w_s5_out[l].astype(BF16),
        w_ret_out=w_ret_out[l].astype(BF16), w_out=w_out[l].astype(BF16), wr=wr, br=br,
        w1=w1[l].astype(BF16).reshape(N_EXPERTS, D_MODEL, MOE_FF),
        w3=w3[l].astype(BF16).reshape(N_EXPERTS, D_MODEL, MOE_FF),
        w2=w2[l].astype(BF16).reshape(N_EXPERTS, MOE_FF, D_MODEL))
    fn = final_norm.astype(F32).reshape(1, D_MODEL)
    mod = _mod(jnp.concatenate([c_prompt, c_sample], axis=0).astype(F32), w_ada[l], b_ada[l])

    h0_p = jnp.zeros((bp, S5_LANES), F32)
    s0_p = jnp.zeros((bp, RET_HEADS, RET_DK, RET_DV), F32)
    prompt = _mixers(x_prompt, mod[:bp], h0_p, s0_p, 0, p)
    h0_s = _s5_state_to_lanes(state_s5_re[l], state_s5_im[l])
    sample = _mixers(x_sample, mod[bp:], h0_s, state_ret[l].astype(F32), PAST_LEN, p)
    y_p, y_s = _moe(prompt, sample, p, fn)
    p_re, p_im = _s5_state_from_lanes(prompt['h'])
    s_re, s_im = _s5_state_from_lanes(sample['h'])
    return (y_p.reshape(x_prompt.shape), y_s.reshape(x_sample.shape), p_re[None], p_im[None],
            prompt['s'][None], s_re[None], s_im[None], sample['s'][None])
```

```python
import functools
import math

import jax
import jax.numpy as jnp
from jax import lax
from jax.experimental import pallas as pl
from jax.experimental.pallas import tpu as pltpu

F32 = jnp.float32
BF16 = jnp.bfloat16

D_MODEL = 1024
PAST_LEN = 2048
CHUNK = 64
S5_WIDTH = 512
S5_GROUP = 16
S5_GROUPS = 32
S5_STATE = 64
S5_LANES = 2 * S5_GROUPS * S5_STATE
S5_CHUNKS = 4
RET_HEADS = 4
RET_DK = 128
RET_DV = 256
RET_QK = RET_HEADS * RET_DK
RET_V = RET_HEADS * RET_DV
ROPE_BASE = 10000.0
MOE_GROUPS = 4
MOE_EXPERTS = 8
N_EXPERTS = MOE_GROUPS * MOE_EXPERTS
PAIRS_PER_GROUP = MOE_EXPERTS * (MOE_EXPERTS - 1) // 2
N_PAIRS = 128
MOE_FF = 256
EPS = 1e-6
IN_WIDTH = S5_WIDTH + 2 * RET_QK + 2 * RET_V + 2 * D_MODEL
ROUTE_ROWS = 8 * (1 + MOE_GROUPS)

BATCH_GROUP = 8
TOKEN_TILE = 512
INPROJ_TILE = 512
S5_TIME_TILE = 64
RET_BLOCK = 256
EXPERT_TILE = 256
ROUTE_TILE = 512
DISPATCH_TILE = 2048
VMEM_LIMIT = 56 * 1024 * 1024
LANE = 128
SUBLANE = 8
ROW_TILES = D_MODEL // LANE
DMA_UNROLL = 8
DMA_QUEUES = 2


def _cparams(*sem):
    return pltpu.CompilerParams(dimension_semantics=sem, vmem_limit_bytes=VMEM_LIMIT)


def _bdot(a, b):
    return jnp.dot(a, b, preferred_element_type=F32)


def _sigmoid(x):
    return 0.5 * jnp.tanh(0.5 * x) + 0.5


def _mod_kernel(c_ref, w_ref, b_ref, o_ref):
    c = c_ref[...]
    a = (c * _sigmoid(c)).astype(BF16)
    o_ref[...] = _bdot(a, w_ref[...].astype(BF16)) + b_ref[...]


def _mod(c, w_ada, b_ada):
    n = c.shape[0]
    return pl.pallas_call(
        _mod_kernel,
        grid=(6,),
        in_specs=[pl.BlockSpec((n, D_MODEL), lambda j: (0, 0)),
                  pl.BlockSpec((D_MODEL, D_MODEL), lambda j: (0, j)),
                  pl.BlockSpec((1, D_MODEL), lambda j: (0, j))],
        out_specs=pl.BlockSpec((n, D_MODEL), lambda j: (0, j)),
        out_shape=jax.ShapeDtypeStruct((n, 6 * D_MODEL), F32),
        compiler_params=_cparams("parallel"),
        name="mod",
    )(c, w_ada, b_ada.reshape(1, -1))


def _rope(x, cos2, sin2):
    return x * cos2 + pltpu.roll(x, RET_DK // 2, 1) * sin2


def _inproj_kernel(x_ref, sh_ref, sc_ref, g_ref, w_ref, cos_ref, sin_ref,
                   u_ref, q_ref, k_ref, v_ref, gs_ref, ga_ref, gb_ref):
    x = x_ref[...]
    n = x * lax.rsqrt(jnp.mean(x * x, axis=-1, keepdims=True) + EPS) * g_ref[...]
    nb = (n * (1.0 + sc_ref[...]) + sh_ref[...]).astype(BF16)
    cos2 = cos_ref[...]
    sin2 = sin_ref[...]
    o = 0
    u_ref[...] = _bdot(nb, w_ref[:, o:o + S5_WIDTH]).astype(BF16)
    o += S5_WIDTH
    q = _bdot(nb, w_ref[:, o:o + RET_QK])
    for h in range(RET_HEADS):
        head = slice(h * RET_DK, (h + 1) * RET_DK)
        q_ref[:, head] = _rope(q[:, head], cos2, sin2).astype(BF16)
    o += RET_QK
    k = _bdot(nb, w_ref[:, o:o + RET_QK])
    for h in range(RET_HEADS):
        head = slice(h * RET_DK, (h + 1) * RET_DK)
        k_ref[:, head] = (_rope(k[:, head], cos2, sin2) * (RET_DK ** -0.5)).astype(BF16)
    o += RET_QK
    for ref in (v_ref, gs_ref, ga_ref, gb_ref):
        ref[...] = _bdot(nb, w_ref[:, o:o + D_MODEL]).astype(BF16)
        o += D_MODEL


def _mod_spec(vec, tl):
    if vec.shape[1] == 1:
        return pl.BlockSpec((None, 1, D_MODEL), lambda b, t: (b, 0, 0))
    return pl.BlockSpec((None, tl, D_MODEL), lambda b, t: (b, t, 0))


def _inproj(x, shift, scale, g1, w_in_b, cos2, sin2):
    bsz, seq, _ = x.shape
    tl = min(INPROJ_TILE, seq)
    row = lambda w: pl.BlockSpec((None, tl, w), lambda b, t: (b, t, 0))
    vec = _mod_spec(shift, tl)
    shapes = [S5_WIDTH, RET_QK, RET_QK, RET_V, RET_V, D_MODEL, D_MODEL]
    return pl.pallas_call(
        _inproj_kernel,
        grid=(bsz, seq // tl),
        in_specs=[row(D_MODEL), vec, vec,
                  pl.BlockSpec((1, D_MODEL), lambda b, t: (0, 0)),
                  pl.BlockSpec((D_MODEL, IN_WIDTH), lambda b, t: (0, 0), pipeline_mode=pl.Buffered(1)),
                  pl.BlockSpec((tl, RET_DK), lambda b, t: (t, 0)),
                  pl.BlockSpec((tl, RET_DK), lambda b, t: (t, 0))],
        out_specs=[row(w) for w in shapes],
        out_shape=[jax.ShapeDtypeStruct((bsz, seq, w), BF16) for w in shapes],
        compiler_params=_cparams("parallel", "parallel"),
        name="inproj",
    )(x, shift, scale, g1, w_in_b, cos2, sin2)


def _gelu_tanh(y):
    return 0.5 * y * (1.0 + jnp.tanh(math.sqrt(2.0 / math.pi) * (y + 0.044715 * (y * y * y))))


def _s5_kernel(u_ref, h0_ref, a_ref, pm_ref, pt_ref, bm_ref, cm_ref, d_ref, wg_ref, bg_ref, wo_ref,
               ya_ref, ht_ref, bu_ref, hs_ref, *, tt):
    ti = pl.program_id(1)
    rows = BATCH_GROUP * tt
    half = S5_LANES // (2 * S5_CHUNKS)

    @pl.when(ti == 0)
    def _():
        hs_ref[...] = h0_ref[...]

    u2 = _bdot(pm_ref[...], u_ref[...].reshape(rows, S5_WIDTH)).astype(BF16)
    kc = S5_WIDTH // S5_CHUNKS
    ys = []
    for c in range(S5_CHUNKS):
        lanes = slice(c * 2 * half, (c + 1) * 2 * half)
        lre = slice(c * 2 * half, c * 2 * half + half)
        lim = slice(c * 2 * half + half, (c + 1) * 2 * half)
        bu_ref[:, lanes] = _bdot(u2[:, c * kc:(c + 1) * kc], bm_ref[c])
        are = a_ref[:, lre]
        aim = a_ref[:, lim]
        hre = hs_ref[:, lre]
        him = hs_ref[:, lim]
        for t in range(tt):
            rsel = slice(t * BATCH_GROUP, (t + 1) * BATCH_GROUP)
            hre, him = (are * hre - aim * him + bu_ref[rsel, lre],
                        are * him + aim * hre + bu_ref[rsel, lim])
            bu_ref[rsel, lre] = hre
            bu_ref[rsel, lim] = him
        hs_ref[:, lre] = hre
        hs_ref[:, lim] = him
        ys.append(_bdot(bu_ref[:, lanes].astype(BF16), cm_ref[c]))
    y = jnp.concatenate(ys, axis=1) + d_ref[...] * u2.astype(F32)
    z = _gelu_tanh(y)
    gl = _bdot(z.astype(BF16), wg_ref[...]) + bg_ref[...]
    o = (z * _sigmoid(gl)).astype(BF16)
    ob = _bdot(pt_ref[...], o).astype(BF16)
    ya_ref[...] = _bdot(ob, wo_ref[...]).reshape(BATCH_GROUP, tt, D_MODEL).astype(BF16)

    @pl.when(ti == pl.num_programs(1) - 1)
    def _():
        ht_ref[...] = hs_ref[...]


def _s5(u, h0, a_lanes, bm, cm, d, wg, bg, wo):
    bsz, seq, _ = u.shape
    tt = min(S5_TIME_TILE, seq)
    rows = BATCH_GROUP * tt
    const = lambda shape: pl.BlockSpec(shape, lambda b, t: (0,) * len(shape))
    r = jnp.arange(rows)
    perm = ((r[:, None] % BATCH_GROUP) * tt + r[:, None] // BATCH_GROUP == r[None, :]).astype(BF16)
    return pl.pallas_call(
        functools.partial(_s5_kernel, tt=tt),
        grid=(bsz // BATCH_GROUP, seq // tt),
        in_specs=[pl.BlockSpec((BATCH_GROUP, tt, S5_WIDTH), lambda b, t: (b, t, 0)),
                  pl.BlockSpec((BATCH_GROUP, S5_LANES), lambda b, t: (b, 0)),
                  const((BATCH_GROUP, S5_LANES)), const((rows, rows)), const((rows, rows)),
                  const(bm.shape), const(cm.shape), const((1, S5_WIDTH)),
                  const((S5_WIDTH, S5_WIDTH)), const((1, S5_WIDTH)), const((S5_WIDTH, D_MODEL))],
        out_specs=[pl.BlockSpec((BATCH_GROUP, tt, D_MODEL), lambda b, t: (b, t, 0)),
                   pl.BlockSpec((BATCH_GROUP, S5_LANES), lambda b, t: (b, 0))],
        out_shape=[jax.ShapeDtypeStruct((bsz, seq, D_MODEL), BF16),
                   jax.ShapeDtypeStruct((bsz, S5_LANES), F32)],
        scratch_shapes=[pltpu.VMEM((rows, S5_LANES), F32),
                        pltpu.VMEM((BATCH_GROUP, S5_LANES), F32)],
        compiler_params=_cparams("parallel", "arbitrary"),
        name="s5",
    )(u, h0, a_lanes, perm, perm.T, bm, cm, d, wg, bg, wo)


def _s5_params(a_re, a_im, log_dt, b_re, b_im, c_re, c_im):
    a_re = a_re.astype(F32)
    a_im = a_im.astype(F32)
    dt = jnp.exp(log_dt.astype(F32))[:, None]
    mag = jnp.exp(a_re * dt)
    ang = a_im * dt
    ab_re = mag * jnp.cos(ang)
    ab_im = mag * jnp.sin(ang)
    den = a_re * a_re + a_im * a_im
    nr = ab_re - 1.0
    ni = ab_im
    f_re = (nr * a_re + ni * a_im) / den
    f_im = (ni * a_re - nr * a_im) / den
    b_re = b_re.astype(F32)
    b_im = b_im.astype(F32)
    bb_re = f_re[..., None] * b_re - f_im[..., None] * b_im
    bb_im = f_re[..., None] * b_im + f_im[..., None] * b_re
    gpc = S5_GROUPS // S5_CHUNKS
    eye = jnp.eye(gpc, dtype=F32)

    def lanes(x):
        return x.reshape(S5_CHUNKS, gpc * S5_STATE)

    a_lanes = jnp.concatenate([lanes(ab_re), lanes(ab_im)], axis=1).reshape(1, S5_LANES)
    a_lanes = jnp.broadcast_to(a_lanes, (BATCH_GROUP, S5_LANES))

    def in_blocks(bb):
        bb = bb.reshape(S5_CHUNKS, gpc, S5_STATE, S5_GROUP)
        return jnp.einsum('cgpj,gh->cgjhp', bb, eye).reshape(S5_CHUNKS, gpc * S5_GROUP, gpc * S5_STATE)

    bm = jnp.concatenate([in_blocks(bb_re), in_blocks(bb_im)], axis=2).astype(BF16)

    def out_blocks(cc):
        cc = cc.astype(F32).reshape(S5_CHUNKS, gpc, S5_GROUP, S5_STATE)
        return jnp.einsum('cgjp,gh->cgphj', cc, eye).reshape(S5_CHUNKS, gpc * S5_STATE, gpc * S5_GROUP)

    cm = jnp.concatenate([out_blocks(c_re), -out_blocks(c_im)], axis=1).astype(BF16)
    return a_lanes, bm, cm


def _s5_state_to_lanes(h_re, h_im):
    bsz = h_re.shape[0]
    re = h_re.astype(F32).reshape(bsz, S5_CHUNKS, -1)
    im = h_im.astype(F32).reshape(bsz, S5_CHUNKS, -1)
    return jnp.concatenate([re, im], axis=2).reshape(bsz, S5_LANES)


def _s5_state_from_lanes(h):
    bsz = h.shape[0]
    h = h.reshape(bsz, S5_CHUNKS, 2, S5_GROUPS // S5_CHUNKS, S5_STATE)
    return (h[:, :, 0].reshape(bsz, S5_GROUPS, S5_STATE), h[:, :, 1].reshape(bsz, S5_GROUPS, S5_STATE))


def _ret_kernel(q_ref, k_ref, v_ref, g_ref, s0_ref, dm_ref, xi_ref, zeta_ref, wo_ref,
                yb_ref, st_ref, s_ref, *, block_decay):
    si = pl.program_id(1)

    @pl.when(si == 0)
    def _():
        s_ref[...] = s0_ref[...]

    gated = []
    for h in range(RET_HEADS):
        qh = q_ref[:, h * RET_DK:(h + 1) * RET_DK]
        kh = k_ref[:, h * RET_DK:(h + 1) * RET_DK]
        vh = v_ref[:, h * RET_DV:(h + 1) * RET_DV]
        scores = lax.dot_general(qh, kh, (((1,), (1,)), ((), ())), preferred_element_type=F32) * dm_ref[h]
        state = s_ref[h]
        o = _bdot(scores.astype(BF16), vh) + _bdot(qh, state.astype(BF16)) * xi_ref[h]
        o = o * lax.rsqrt(jnp.mean(o * o, axis=-1, keepdims=True) + EPS)
        gh = g_ref[:, h * RET_DV:(h + 1) * RET_DV].astype(F32)
        gated.append((o * (gh * _sigmoid(gh))).astype(BF16))
        kz = (kh.astype(F32) * zeta_ref[h]).astype(BF16)
        kv = lax.dot_general(kz, vh, (((0,), (0,)), ((), ())), preferred_element_type=F32)
        s_ref[h] = block_decay[h] * state + kv
    yb_ref[...] = _bdot(jnp.concatenate(gated, axis=1), wo_ref[...]).astype(BF16)

    @pl.when(si == pl.num_programs(1) - 1)
    def _():
        st_ref[...] = s_ref[...]


def _ret_tables(seq):
    cl = min(CHUNK, seq)
    blk = min(RET_BLOCK, seq)
    log_g = jnp.log(1.0 - 2.0 ** (-5.0 - jnp.arange(RET_HEADS, dtype=F32)))
    idx = jnp.arange(blk, dtype=F32)
    diff = idx[:, None] - idx[None, :]
    cn = jnp.arange(blk)[:, None] // cl
    cm = jnp.arange(blk)[None, :] // cl
    expo = jnp.where(cm == cn, jnp.abs(diff), diff)
    dm = jnp.where(cm <= cn, jnp.exp(log_g[:, None, None] * expo[None]), 0.0)
    xi = jnp.exp(log_g[:, None] * (idx + 1.0)[None, :])[..., None]
    zeta = jnp.exp(log_g[:, None] * (blk - 1.0 - idx)[None, :])[..., None]
    block_decay = tuple(math.exp(math.log(1.0 - 2.0 ** (-5.0 - h)) * blk) for h in range(RET_HEADS))
    return blk, dm, xi, zeta, block_decay


def _retention(q, k, v, g, s0, w_ret_out_b):
    bsz, seq, _ = q.shape
    blk, dm, xi, zeta, block_decay = _ret_tables(seq)
    row = lambda w: pl.BlockSpec((None, blk, w), lambda b, s: (b, s, 0))
    const = lambda shape: pl.BlockSpec(shape, lambda b, s: (0,) * len(shape))
    state = pl.BlockSpec((None, RET_HEADS, RET_DK, RET_DV), lambda b, s: (b, 0, 0, 0))
    return pl.pallas_call(
        functools.partial(_ret_kernel, block_decay=block_decay),
        grid=(bsz, seq // blk),
        in_specs=[row(RET_QK), row(RET_QK), row(RET_V), row(RET_V), state,
                  const(dm.shape), const(xi.shape), const(zeta.shape), const((RET_V, D_MODEL))],
        out_specs=[row(D_MODEL), state],
        out_shape=[jax.ShapeDtypeStruct((bsz, seq, D_MODEL), BF16),
                   jax.ShapeDtypeStruct((bsz, RET_HEADS, RET_DK, RET_DV), F32)],
        scratch_shapes=[pltpu.VMEM((RET_HEADS, RET_DK, RET_DV), F32)],
        compiler_params=_cparams("parallel", "arbitrary"),
        name="retention",
    )(q, k, v, g, s0, dm, xi, zeta, w_ret_out_b)


def _store_row_tiles(ref, val, span=ROW_TILES, offset=0):
    n = val.shape[0]
    for j in range(ROW_TILES):
        ref[pl.ds(offset + j, n, stride=span), :] = val[:, j * LANE:(j + 1) * LANE]


def _load_row_tiles(ref, n, span=ROW_TILES, offset=0):
    return jnp.concatenate([ref[pl.ds(offset + j, n, stride=span), :] for j in range(ROW_TILES)], axis=1)


def _merge_kernel(x_ref, ya_ref, yb_ref, ga_ref, gb_ref, gt_ref, sh_ref, sc_ref, g2_ref, wo_ref,
                  wr_ref, br_ref, x1_ref, n2_ref, re_ref, wc_ref, cnt_ref):
    merged = _sigmoid(ga_ref[...]) * ya_ref[...] + _sigmoid(gb_ref[...]) * yb_ref[...]
    x1 = x_ref[...] + gt_ref[...] * _bdot(merged, wo_ref[...])
    x1_ref[...] = x1.astype(BF16)
    n2 = x1 * lax.rsqrt(jnp.mean(x1 * x1, axis=-1, keepdims=True) + EPS) * g2_ref[...]
    n2 = n2 * (1.0 + sc_ref[...]) + sh_ref[...]
    _store_row_tiles(n2_ref, n2)

    nt_dot = lambda a, b: lax.dot_general(a, b, (((1,), (1,)), ((), ())), preferred_element_type=F32)
    n2_hi = n2.astype(BF16)
    n2_lo = (n2 - n2_hi.astype(F32)).astype(BF16)
    lt = (nt_dot(wr_ref[0], n2_hi) + nt_dot(wr_ref[0], n2_lo) + nt_dot(wr_ref[1], n2_hi)) + br_ref[...]
    tl = lt.shape[1]
    iota = lax.broadcasted_iota(jnp.int32, (8, tl), 0)
    gl = lt[0:8]
    gmax = jnp.max(gl, axis=0, keepdims=True)
    gi = jnp.min(jnp.where(gl == gmax, iota, 8), axis=0, keepdims=True)
    gw = 1.0 / jnp.sum(jnp.exp(gl - gmax), axis=0, keepdims=True)
    el = jnp.zeros((8, tl), F32)
    for g in range(MOE_GROUPS):
        el = jnp.where(gi == g, lt[8 * (g + 1):8 * (g + 2)], el)
    m1 = jnp.max(el, axis=0, keepdims=True)
    i1 = jnp.min(jnp.where(el == m1, iota, 8), axis=0, keepdims=True)
    el2 = jnp.where(iota == i1, -jnp.inf, el)
    m2 = jnp.max(el2, axis=0, keepdims=True)
    i2 = jnp.min(jnp.where(el2 == m2, iota, 8), axis=0, keepdims=True)
    e21 = jnp.exp(m2 - m1)
    w1 = gw / (1.0 + e21)
    w2 = w1 * e21
    lo = jnp.minimum(i1, i2)
    hi = jnp.maximum(i1, i2)
    pair = gi * PAIRS_PER_GROUP + lo * (MOE_EXPERTS - 1) - ((lo * (lo - 1)) >> 1) + (hi - lo - 1)
    re_ref[...] = jnp.where(iota == 0, pair, 0)
    rw = jnp.where(iota == 0, jnp.where(i1 < i2, w1, w2), jnp.where(iota == 1, jnp.where(i1 < i2, w2, w1), 0.0))
    eye = (lax.broadcasted_iota(jnp.int32, (8, LANE), 0)
           == lax.broadcasted_iota(jnp.int32, (8, LANE), 1)).astype(F32)
    wc_ref[...] = lax.dot_general(rw, eye, (((0,), (0,)), ((), ())),
                                  precision=lax.Precision.HIGHEST, preferred_element_type=F32)
    ids = lax.broadcasted_iota(jnp.int32, (N_PAIRS, tl), 0)
    hits = jnp.where(ids == pair, 1.0, 0.0)
    cnt_ref[...] = jnp.broadcast_to(jnp.sum(hits, axis=1, keepdims=True), (N_PAIRS, LANE)).astype(jnp.int32)


def _merge(x, ya, yb, ga, gb, gate1, shift2, scale2, g2, w_out_b, wr, br):
    bsz, seq, _ = x.shape
    tl = min(TOKEN_TILE, seq)
    row = pl.BlockSpec((None, tl, D_MODEL), lambda b, t: (b, t, 0))
    vec = _mod_spec(gate1, tl)
    const = lambda shape: pl.BlockSpec(shape, lambda b, t: (0,) * len(shape))
    route = pl.BlockSpec((None, 8, tl), lambda b, t: (b, 0, t))
    nt = seq // tl
    tiles = pl.BlockSpec((tl * ROW_TILES, LANE), lambda b, t: (b * nt + t, 0))
    return pl.pallas_call(
        _merge_kernel,
        grid=(bsz, nt),
        in_specs=[row, row, row, row, row, vec, vec, vec, const((1, D_MODEL)),
                  const((D_MODEL, D_MODEL)), const((2, ROUTE_ROWS, D_MODEL)), const((ROUTE_ROWS, 1))],
        out_specs=[row, tiles, route,
                   pl.BlockSpec((tl, LANE), lambda b, t: (b * nt + t, 0)),
                   pl.BlockSpec((None, N_PAIRS, LANE), lambda b, t: (b * nt + t, 0, 0))],
        out_shape=[jax.ShapeDtypeStruct((bsz, seq, D_MODEL), BF16),
                   jax.ShapeDtypeStruct((bsz * seq * ROW_TILES, LANE), F32),
                   jax.ShapeDtypeStruct((bsz, 8, seq), jnp.int32),
                   jax.ShapeDtypeStruct((bsz * seq, LANE), F32),
                   jax.ShapeDtypeStruct((bsz * nt, N_PAIRS, LANE), jnp.int32)],
        compiler_params=_cparams("parallel", "parallel"),
        name="merge",
    )(x, ya, yb, ga, gb, gate1, shift2, scale2, g2, w_out_b, wr, br)


def _router_weights(w_rg, b_rg, w_re, b_re):
    wr = jnp.zeros((ROUTE_ROWS, D_MODEL), F32)
    wr = wr.at[0:MOE_GROUPS].set(w_rg.astype(F32).T)
    wr = wr.at[8:].set(jnp.transpose(w_re.astype(F32), (0, 2, 1)).reshape(N_EXPERTS, D_MODEL))
    br = jnp.full((ROUTE_ROWS,), -1e30, F32)
    br = br.at[0:MOE_GROUPS].set(b_rg.astype(F32))
    br = br.at[8:].set(b_re.astype(F32).reshape(N_EXPERTS))
    wr_hi = wr.astype(BF16)
    wr_lo = (wr - wr_hi.astype(F32)).astype(BF16)
    return jnp.stack([wr_hi, wr_lo]), br.reshape(ROUTE_ROWS, 1)


def _row_gather(idx_ref, src_hbm, buf, sem, slot, n, span):
    def body(pair, carry):
        for queue in range(DMA_QUEUES):
            r = pair * DMA_QUEUES + queue
            src = pl.multiple_of(idx_ref[0, r] * span, span)
            dst = pl.multiple_of(r * span, span)
            pltpu.make_async_copy(src_hbm.at[pl.ds(src, span), :],
                                  buf.at[slot, pl.ds(dst, span), :], sem.at[slot]).start(priority=queue)
        return carry
    lax.fori_loop(0, n // DMA_QUEUES, body, 0, unroll=DMA_UNROLL // DMA_QUEUES)


def _row_gather_wait(src_hbm, buf, sem, slot, n, span):
    pltpu.make_async_copy(src_hbm.at[pl.ds(0, n * span), :], buf.at[slot], sem.at[slot]).wait()


def _plan_kernel(e_ref, base_ref, c0_ref, tri_ref, pos_ref, carry_ref):
    @pl.when(pl.program_id(0) == 0)
    def _():
        carry_ref[...] = c0_ref[...]

    tb = e_ref.shape[1]
    hit = lax.broadcasted_iota(jnp.int32, (N_PAIRS, tb), 0) == e_ref[0:1, :]
    incl = _bdot(jnp.where(hit, 1.0, 0.0).astype(BF16), tri_ref[...])
    row = jnp.where(hit, incl - 1.0 + carry_ref[...] + base_ref[...], 0.0)
    pos_ref[...] = jnp.sum(row, axis=0, keepdims=True).astype(jnp.int32)
    carry_ref[...] = carry_ref[...] + incl[:, tb - 1:tb]


def _plan(route_e, pad_start, placed):
    bsz, _, seq = route_e.shape
    tb = min(ROUTE_TILE, seq)
    nb = seq // tb
    r = jnp.arange(tb)
    tri = (r[:, None] <= r[None, :]).astype(BF16)
    col = pl.BlockSpec((N_PAIRS, 1), lambda i: (0, 0))
    return pl.pallas_call(
        _plan_kernel,
        grid=(bsz * nb,),
        in_specs=[pl.BlockSpec((None, 8, tb), lambda i: (i // nb, 0, i % nb)), col, col,
                  pl.BlockSpec((tb, tb), lambda i: (0, 0))],
        out_specs=pl.BlockSpec((None, 1, tb), lambda i: (i, 0, 0)),
        out_shape=jax.ShapeDtypeStruct((bsz * nb, 1, tb), jnp.int32),
        scratch_shapes=[pltpu.VMEM((N_PAIRS, 1), F32)],
        compiler_params=_cparams("arbitrary"),
        name="plan",
    )(route_e, pad_start.astype(F32).reshape(N_PAIRS, 1), placed.astype(F32).reshape(N_PAIRS, 1), tri)


ZERO_CHUNKS = tuple(2 ** b for b in range(8, -1, -1))


def _dispatch_kernel(lo_ref, hi_ref, nu_ref, posa_ref, xa_ref, posb_ref, xb_ref, xs_hbm, zero_ref, sem,
                     *, tb, nb_rows, tm, n_tiles):
    last = pl.num_programs(0) - 1

    def scatter(pos_ref, x_ref, n):
        def body(pair, carry):
            for queue in range(DMA_QUEUES):
                r = pair * DMA_QUEUES + queue
                src = pl.multiple_of(r * ROW_TILES, ROW_TILES)
                dst = pl.multiple_of(pos_ref[0, r] * ROW_TILES, ROW_TILES)
                pltpu.make_async_copy(x_ref.at[pl.ds(src, ROW_TILES), :],
                                      xs_hbm.at[pl.ds(dst, ROW_TILES), :], sem.at[0]).start(priority=queue)
            return carry
        lax.fori_loop(0, n // DMA_QUEUES, body, 0, unroll=DMA_UNROLL // DMA_QUEUES)

    def scatter_wait(x_ref, n):
        pltpu.make_async_copy(x_ref, xs_hbm.at[pl.ds(0, n * ROW_TILES), :], sem.at[0]).wait()

    def zero_copy(row, size):
        dst = pl.multiple_of(row * ROW_TILES, ROW_TILES)
        return pltpu.make_async_copy(zero_ref.at[pl.ds(0, size * ROW_TILES), :],
                                     xs_hbm.at[pl.ds(dst, size * ROW_TILES), :], sem.at[1])

    @pl.when(pl.program_id(0) < last)
    def _():
        scatter(posa_ref, xa_ref, tb)
        scatter_wait(xa_ref, tb)

    @pl.when(pl.program_id(0) == last)
    def _():
        scatter(posb_ref, xb_ref, nb_rows)
        zero_ref[...] = jnp.zeros_like(zero_ref)

        def per_pair(e, carry):
            lo = lo_ref[e]
            n = hi_ref[e] - lo
            for wait in (False, True):
                row = lo
                for size in ZERO_CHUNKS:
                    @pl.when((n & size) != 0)
                    def _(row=row, size=size, wait=wait):
                        zero_copy(row, size).wait() if wait else zero_copy(row, size).start()
                    row = row + (n & size)
            return carry
        lax.fori_loop(0, N_PAIRS, per_pair, 0)

        chunk = min(ZERO_CHUNKS[0], tm)
        for wait in (False, True):
            def per_tile(t, carry, wait=wait):
                for c in range(tm // chunk):
                    cp = zero_copy(t * tm + c * chunk, chunk)
                    cp.wait() if wait else cp.start()
                return carry
            lax.fori_loop(nu_ref[0], n_tiles, per_tile, 0)
        scatter_wait(xb_ref, nb_rows)


def _dispatch(pos_a, rows_a, pos_b, rows_b, pad_lo, pad_hi, n_used, n_rows, tm):
    tb = min(DISPATCH_TILE, pos_a.size)
    na = pos_a.size // tb
    pos_a = pos_a.reshape(na, 1, tb)
    nb_rows = rows_b.shape[0] // ROW_TILES
    pos_b = pos_b.reshape(1, 1, nb_rows)
    block_a = lambda i, lo, hi, nu: (jnp.minimum(i, na - 1), 0, 0)
    grid_spec = pltpu.PrefetchScalarGridSpec(
        num_scalar_prefetch=3,
        grid=(na + 1,),
        in_specs=[pl.BlockSpec((None, 1, tb), block_a, memory_space=pltpu.SMEM),
                  pl.BlockSpec((tb * ROW_TILES, LANE), lambda i, lo, hi, nu: (jnp.minimum(i, na - 1), 0)),
                  pl.BlockSpec((None, 1, nb_rows), lambda i, lo, hi, nu: (0, 0, 0), memory_space=pltpu.SMEM),
                  pl.BlockSpec((nb_rows * ROW_TILES, LANE), lambda i, lo, hi, nu: (0, 0))],
        out_specs=pl.BlockSpec(memory_space=pl.ANY),
        scratch_shapes=[pltpu.VMEM((ZERO_CHUNKS[0] * ROW_TILES, LANE), F32), pltpu.SemaphoreType.DMA((2,))],
    )
    return pl.pallas_call(
        functools.partial(_dispatch_kernel, tb=tb, nb_rows=nb_rows, tm=tm, n_tiles=n_rows // tm),
        grid_spec=grid_spec,
        out_shape=jax.ShapeDtypeStruct((n_rows * ROW_TILES, LANE), F32),
        compiler_params=_cparams("arbitrary"),
        name="dispatch",
    )(pad_lo, pad_hi, n_used, pos_a, rows_a, pos_b, rows_b)


def _expert_kernel(ta_ref, tb_ref, nu_ref, x_ref, w1a_ref, w3a_ref, w2a_ref, w1b_ref, w3b_ref, w2b_ref,
                   y_ref, *, tm):
    @pl.when(pl.program_id(0) < nu_ref[0])
    def _():
        xb = _load_row_tiles(x_ref, tm).astype(BF16)
        for slot, (w1_ref, w3_ref, w2_ref) in enumerate(((w1a_ref, w3a_ref, w2a_ref),
                                                         (w1b_ref, w3b_ref, w2b_ref))):
            a = _bdot(xb, w1_ref[...])
            h = a * _sigmoid(a) * _bdot(xb, w3_ref[...])
            _store_row_tiles(y_ref, _bdot(h.astype(BF16), w2_ref[...]), span=2 * ROW_TILES,
                             offset=slot * ROW_TILES)

    @pl.when(pl.program_id(0) >= nu_ref[0])
    def _():
        y_ref[...] = jnp.zeros_like(y_ref)


def _experts(xs, tile_a, tile_b, n_used, w1b, w3b, w2b, tm):
    n_tiles = xs.shape[0] // (tm * ROW_TILES)
    tile = lambda i, ta, tb, nu: (jnp.maximum(jnp.minimum(i, nu[0] - 1), 0), 0)
    first = lambda i, ta, tb, nu: (ta[i], 0, 0)
    second = lambda i, ta, tb, nu: (tb[i], 0, 0)
    up = lambda which: pl.BlockSpec((None, D_MODEL, MOE_FF), which)
    down = lambda which: pl.BlockSpec((None, MOE_FF, D_MODEL), which)
    grid_spec = pltpu.PrefetchScalarGridSpec(
        num_scalar_prefetch=3,
        grid=(n_tiles,),
        in_specs=[pl.BlockSpec((tm * ROW_TILES, LANE), tile),
                  up(first), up(first), down(first), up(second), up(second), down(second)],
        out_specs=pl.BlockSpec((tm * 2 * ROW_TILES, LANE), lambda i, ta, tb, nu: (i, 0)),
    )
    return pl.pallas_call(
        functools.partial(_expert_kernel, tm=tm),
        grid_spec=grid_spec,
        out_shape=jax.ShapeDtypeStruct((2 * xs.shape[0], LANE), F32),
        compiler_params=_cparams("arbitrary"),
        name="experts",
    )(tile_a, tile_b, n_used, xs, w1b, w3b, w2b, w1b, w3b, w2b)


def _expert_layout(counts, tm, n_tok):
    n_tiles = n_tok // tm + MOE_GROUPS * PAIRS_PER_GROUP
    padded = ((counts + tm - 1) // tm) * tm
    pad_end = jnp.cumsum(padded)
    pad_start = pad_end - padded
    n_used = pad_end[-1] // tm
    tile_start = jnp.minimum(jnp.arange(n_tiles, dtype=jnp.int32), n_used - 1) * tm
    tile_pair = jnp.sum((pad_end[None, :] <= tile_start[:, None]).astype(jnp.int32), axis=1)
    tile_pair = jnp.minimum(tile_pair, MOE_GROUPS * PAIRS_PER_GROUP - 1)
    members = [(lo, hi) for lo in range(MOE_EXPERTS) for hi in range(lo + 1, MOE_EXPERTS)]
    lo_of = jnp.asarray([m[0] for m in members], jnp.int32)
    hi_of = jnp.asarray([m[1] for m in members], jnp.int32)
    group = tile_pair // PAIRS_PER_GROUP
    tile_a = group * MOE_EXPERTS + lo_of[tile_pair % PAIRS_PER_GROUP]
    tile_b = group * MOE_EXPERTS + hi_of[tile_pair % PAIRS_PER_GROUP]
    return (pad_start, pad_start + counts, pad_end, tile_a.astype(jnp.int32), tile_b.astype(jnp.int32),
            n_used.astype(jnp.int32).reshape(1), n_tiles * tm)


def _combine_kernel(cur_ref, nxt_ref, x1_ref, wc_ref, gt_ref, fn_ref, y_hbm, o_ref, ybuf, sem, *, tl):
    i = pl.program_id(0)
    slot = i % 2

    span = 2 * ROW_TILES

    @pl.when(i == 0)
    def _():
        _row_gather(cur_ref, y_hbm, ybuf, sem, 0, tl, span)

    @pl.when(i + 1 < pl.num_programs(0))
    def _():
        _row_gather(nxt_ref, y_hbm, ybuf, sem, 1 - slot, tl, span)

    _row_gather_wait(y_hbm, ybuf, sem, slot, tl, span)
    rows = ybuf.at[slot]
    moe = (wc_ref[:, 0:1] * _load_row_tiles(rows, tl, span, 0)
           + wc_ref[:, 1:2] * _load_row_tiles(rows, tl, span, ROW_TILES))
    x2 = x1_ref[...].astype(F32) + gt_ref[...] * moe
    o_ref[...] = x2 * lax.rsqrt(jnp.mean(x2 * x2, axis=-1, keepdims=True) + EPS) * fn_ref[...]


def _combine(x1_flat, pos, y_rows, wcol, gate2, final_norm, seq):
    n_tok = x1_flat.shape[0]
    n_tiles, _, tl = pos.shape
    last = n_tiles - 1
    if gate2.shape[1] == 1:
        gate_spec = pl.BlockSpec((None, 1, D_MODEL), lambda i: ((i * tl) // seq, 0, 0))
    else:
        gate2 = gate2.reshape(n_tok, D_MODEL)
        gate_spec = pl.BlockSpec((tl, D_MODEL), lambda i: (i, 0))
    return pl.pallas_call(
        functools.partial(_combine_kernel, tl=tl),
        grid=(n_tiles,),
        in_specs=[pl.BlockSpec((None, 1, tl), lambda i: (i, 0, 0), memory_space=pltpu.SMEM),
                  pl.BlockSpec((None, 1, tl), lambda i: (jnp.minimum(i + 1, last), 0, 0),
                               memory_space=pltpu.SMEM),
                  pl.BlockSpec((tl, D_MODEL), lambda i: (i, 0)),
                  pl.BlockSpec((tl, LANE), lambda i: (i, 0)),
                  gate_spec,
                  pl.BlockSpec((1, D_MODEL), lambda i: (0, 0)),
                  pl.BlockSpec(memory_space=pl.ANY)],
        out_specs=pl.BlockSpec((tl, D_MODEL), lambda i: (i, 0)),
        out_shape=jax.ShapeDtypeStruct((n_tok, D_MODEL), F32),
        scratch_shapes=[pltpu.VMEM((2, 2 * tl * ROW_TILES, LANE), F32), pltpu.SemaphoreType.DMA((2,))],
        compiler_params=_cparams("arbitrary"),
        name="combine",
    )(pos, pos, x1_flat, wcol, gate2, final_norm, y_rows)


def _rope_tables(seq, offset):
    half = RET_DK // 2
    theta = 1.0 / (ROPE_BASE ** jnp.linspace(0.0, 1.0, half, dtype=F32))
    pos = offset + jnp.arange(seq)
    ang = pos.astype(F32)[:, None] * theta[None, :]
    cos = jnp.cos(ang)
    sin = jnp.sin(ang)
    return jnp.concatenate([cos, cos], axis=1), jnp.concatenate([-sin, sin], axis=1)


def _mixers(x, mod, h0, s0, offset, p):
    bsz, seq, _ = x.shape
    mods = [m.reshape(bsz, 1, D_MODEL) for m in jnp.split(mod, 6, axis=-1)]
    cos2, sin2 = _rope_tables(seq, offset)
    flat = seq < TOKEN_TILE and bsz * seq <= TOKEN_TILE
    sets, rows = (1, bsz * seq) if flat else (bsz, seq)
    tok = lambda a: a.reshape(sets, rows, a.shape[-1])
    if flat:
        mods = [jnp.broadcast_to(m, (bsz, seq, D_MODEL)).reshape(1, rows, D_MODEL) for m in mods]
        cos2, sin2 = jnp.tile(cos2, (bsz, 1)), jnp.tile(sin2, (bsz, 1))
    sh1, sc1, gt1, sh2, sc2, gt2 = mods
    seqs = lambda a: a.reshape(bsz, seq, a.shape[-1])
    u, q, k, v, g, ga, gb = _inproj(tok(x), sh1, sc1, p['norm1'], p['w_in'], cos2, sin2)
    ya, h_t = _s5(seqs(u), h0, p['a_lanes'], p['bm'], p['cm'], p['d'], p['w_glu'], p['b_glu'], p['w_s5_out'])
    yb, s_t = _retention(seqs(q), seqs(k), seqs(v), seqs(g), s0, p['w_ret_out'])
    x1, n2, route_e, wcol, cnt = _merge(tok(x), tok(ya), tok(yb), ga, gb, gt1, sh2, sc2, p['norm2'],
                                        p['w_out'], p['wr'], p['br'])
    return dict(x1=x1.reshape(bsz * seq, D_MODEL), n2=n2, route=route_e, wcol=wcol,
                counts=jnp.sum(cnt[:, :, 0], axis=0), gate2=gt2, seq=rows, h=h_t, s=s_t)


def _moe(a, b, p, final_norm):
    n_tok = a['x1'].shape[0] + b['x1'].shape[0]
    pad_start, pad_lo, pad_hi, tile_a, tile_b, n_used, n_rows = _expert_layout(
        a['counts'] + b['counts'], EXPERT_TILE, n_tok)
    pos_a = _plan(a['route'], pad_start, jnp.zeros_like(a['counts']))
    pos_b = _plan(b['route'], pad_start, a['counts'])
    xs = _dispatch(pos_a, a['n2'], pos_b, b['n2'], pad_lo, pad_hi, n_used, n_rows, EXPERT_TILE)
    y_rows = _experts(xs, tile_a, tile_b, n_used, p['w1'], p['w3'], p['w2'], EXPERT_TILE)
    return [_combine(m['x1'], pos, y_rows, m['wcol'], m['gate2'], final_norm, m['seq'])
            for m, pos in ((a, pos_a), (b, pos_b))]


def kernel(x_prompt, x_sample, state_s5_re, state_s5_im, state_ret, c_prompt, c_sample, w_ada, b_ada, norm1, norm2, w_in, s5_a_re, s5_a_im, s5_log_dt, s5_b_re, s5_b_im, s5_c_re, s5_c_im, s5_d, s5_w_glu, s5_b_glu, w_s5_out, w_ret_out, w_out, w_rg, b_rg, w_re, b_re, w1, w3, w2, final_norm):
    depth = w_ada.shape[0]
    assert depth == 1
    bp = x_prompt.shape[0]
    bs, seq_s, _ = x_sample.shape
    l = 0
    a_lanes, bm, cm = _s5_params(s5_a_re[l], s5_a_im[l], s5_log_dt[l], s5_b_re[l], s5_b_im[l],
                                 s5_c_re[l], s5_c_im[l])
    wr, br = _router_weights(w_rg[l], b_rg[l], w_re[l], b_re[l])
    p = dict(
        norm1=norm1[l].astype(F32).reshape(1, D_MODEL), norm2=norm2[l].astype(F32).reshape(1, D_MODEL),
        w_in=w_in[l].astype(BF16), a_lanes=a_lanes, bm=bm, cm=cm,
        d=s5_d[l].astype(F32).reshape(1, S5_WIDTH), w_glu=s5_w_glu[l].astype(BF16),
        b_glu=s5_b_glu[l].astype(F32).reshape(1, S5_WIDTH), w_s5_out=w_s5_out[l].astype(BF16),
        w_ret_out=w_ret_out[l].astype(BF16), w_out=w_out[l].astype(BF16), wr=wr, br=br,
        w1=w1[l].astype(BF16).reshape(N_EXPERTS, D_MODEL, MOE_FF),
        w3=w3[l].astype(BF16).reshape(N_EXPERTS, D_MODEL, MOE_FF),
        w2=w2[l].astype(BF16).reshape(N_EXPERTS, MOE_FF, D_MODEL))
    fn = final_norm.astype(F32).reshape(1, D_MODEL)
    mod = _mod(jnp.concatenate([c_prompt, c_sample], axis=0).astype(F32), w_ada[l], b_ada[l])

    h0_p = jnp.zeros((bp, S5_LANES), F32)
    s0_p = jnp.zeros((bp, RET_HEADS, RET_DK, RET_DV), F32)
    prompt = _mixers(x_prompt, mod[:bp], h0_p, s0_p, 0, p)
    h0_s = _s5_state_to_lanes(state_s5_re[l], state_s5_im[l])
    sample = _mixers(x_sample, mod[bp:], h0_s, state_ret[l].astype(F32), PAST_LEN, p)
    y_p, y_s = _moe(prompt, sample, p, fn)
    p_re, p_im = _s5_state_from_lanes(prompt['h'])
    s_re, s_im = _s5_state_from_lanes(sample['h'])
    return (y_p.reshape(x_prompt.shape), y_s.reshape(x_sample.shape), p_re[None], p_im[None],
            prompt['s'][None], s_re[None], s_im[None], sample['s'][None])
```

```python
import functools
import math

import jax
import jax.numpy as jnp
from jax import lax
from jax.experimental import pallas as pl
from jax.experimental.pallas import tpu as pltpu

F32 = jnp.float32
BF16 = jnp.bfloat16

D_MODEL = 1024
PAST_LEN = 2048
CHUNK = 64
S5_WIDTH = 512
S5_GROUP = 16
S5_GROUPS = 32
S5_STATE = 64
S5_LANES = 2 * S5_GROUPS * S5_STATE
S5_CHUNKS = 4
RET_HEADS = 4
RET_DK = 128
RET_DV = 256
RET_QK = RET_HEADS * RET_DK
RET_V = RET_HEADS * RET_DV
ROPE_BASE = 10000.0
MOE_GROUPS = 4
MOE_EXPERTS = 8
N_EXPERTS = MOE_GROUPS * MOE_EXPERTS
PAIRS_PER_GROUP = MOE_EXPERTS * (MOE_EXPERTS - 1) // 2
N_PAIRS = 128
MOE_FF = 256
EPS = 1e-6
IN_WIDTH = S5_WIDTH + 2 * RET_QK + 2 * RET_V + 2 * D_MODEL
ROUTE_ROWS = 8 * (1 + MOE_GROUPS)

BATCH_GROUP = 8
TOKEN_TILE = 512
INPROJ_TILE = 512
S5_TIME_TILE = 64
RET_BLOCK = 256
RET_BLOCKS_PER_STEP = 4
EXPERT_TILE = 256
ROUTE_TILE = 512
DISPATCH_TILE = 2048
VMEM_LIMIT = 56 * 1024 * 1024
LANE = 128
SUBLANE = 8
ROW_TILES = D_MODEL // LANE
DMA_UNROLL = 8
DMA_QUEUES = 2


def _cparams(*sem):
    return pltpu.CompilerParams(dimension_semantics=sem, vmem_limit_bytes=VMEM_LIMIT)


def _bdot(a, b):
    return jnp.dot(a, b, preferred_element_type=F32)


def _sigmoid(x):
    return 0.5 * jnp.tanh(0.5 * x) + 0.5


def _mod_kernel(c_ref, w_ref, b_ref, o_ref):
    c = c_ref[...]
    a = (c * _sigmoid(c)).astype(BF16)
    o_ref[...] = _bdot(a, w_ref[...].astype(BF16)) + b_ref[...]


def _mod(c, w_ada, b_ada):
    n = c.shape[0]
    return pl.pallas_call(
        _mod_kernel,
        grid=(6,),
        in_specs=[pl.BlockSpec((n, D_MODEL), lambda j: (0, 0)),
                  pl.BlockSpec((D_MODEL, D_MODEL), lambda j: (0, j)),
                  pl.BlockSpec((1, D_MODEL), lambda j: (0, j))],
        out_specs=pl.BlockSpec((n, D_MODEL), lambda j: (0, j)),
        out_shape=jax.ShapeDtypeStruct((n, 6 * D_MODEL), F32),
        compiler_params=_cparams("parallel"),
        name="mod",
    )(c, w_ada, b_ada.reshape(1, -1))


def _rope(x, cos2, sin2):
    return x * cos2 + pltpu.roll(x, RET_DK // 2, 1) * sin2


def _inproj_kernel(x_ref, sh_ref, sc_ref, g_ref, w_ref, cos_ref, sin_ref,
                   u_ref, q_ref, k_ref, v_ref, gs_ref, ga_ref, gb_ref):
    x = x_ref[...]
    n = x * lax.rsqrt(jnp.mean(x * x, axis=-1, keepdims=True) + EPS) * g_ref[...]
    nb = (n * (1.0 + sc_ref[...]) + sh_ref[...]).astype(BF16)
    cos2 = cos_ref[...]
    sin2 = sin_ref[...]
    o = 0
    u_ref[...] = _bdot(nb, w_ref[:, o:o + S5_WIDTH]).astype(BF16)
    o += S5_WIDTH
    q = _bdot(nb, w_ref[:, o:o + RET_QK])
    for h in range(RET_HEADS):
        head = slice(h * RET_DK, (h + 1) * RET_DK)
        q_ref[:, head] = _rope(q[:, head], cos2, sin2).astype(BF16)
    o += RET_QK
    k = _bdot(nb, w_ref[:, o:o + RET_QK])
    for h in range(RET_HEADS):
        head = slice(h * RET_DK, (h + 1) * RET_DK)
        k_ref[:, head] = (_rope(k[:, head], cos2, sin2) * (RET_DK ** -0.5)).astype(BF16)
    o += RET_QK
    for ref in (v_ref, gs_ref, ga_ref, gb_ref):
        ref[...] = _bdot(nb, w_ref[:, o:o + D_MODEL]).astype(BF16)
        o += D_MODEL


def _mod_spec(vec, tl):
    if vec.shape[1] == 1:
        return pl.BlockSpec((None, 1, D_MODEL), lambda b, t: (b, 0, 0))
    return pl.BlockSpec((None, tl, D_MODEL), lambda b, t: (b, t, 0))


def _inproj(x, shift, scale, g1, w_in_b, cos2, sin2):
    bsz, seq, _ = x.shape
    tl = min(INPROJ_TILE, seq)
    row = lambda w: pl.BlockSpec((None, tl, w), lambda b, t: (b, t, 0))
    vec = _mod_spec(shift, tl)
    shapes = [S5_WIDTH, RET_QK, RET_QK, RET_V, RET_V, D_MODEL, D_MODEL]
    return pl.pallas_call(
        _inproj_kernel,
        grid=(bsz, seq // tl),
        in_specs=[row(D_MODEL), vec, vec,
                  pl.BlockSpec((1, D_MODEL), lambda b, t: (0, 0)),
                  pl.BlockSpec((D_MODEL, IN_WIDTH), lambda b, t: (0, 0), pipeline_mode=pl.Buffered(1)),
                  pl.BlockSpec((tl, RET_DK), lambda b, t: (t, 0)),
                  pl.BlockSpec((tl, RET_DK), lambda b, t: (t, 0))],
        out_specs=[row(w) for w in shapes],
        out_shape=[jax.ShapeDtypeStruct((bsz, seq, w), BF16) for w in shapes],
        compiler_params=_cparams("parallel", "parallel"),
        name="inproj",
    )(x, shift, scale, g1, w_in_b, cos2, sin2)


def _gelu_tanh(y):
    return 0.5 * y * (1.0 + jnp.tanh(math.sqrt(2.0 / math.pi) * (y + 0.044715 * (y * y * y))))


def _s5_kernel(u_ref, h0_ref, a_ref, pm_ref, pt_ref, bm_ref, cm_ref, d_ref, wg_ref, bg_ref, wo_ref,
               ya_ref, ht_ref, hs_ref, *bu_refs, tt):
    ti = pl.program_id(1)
    rows = BATCH_GROUP * tt
    half = S5_LANES // (2 * S5_CHUNKS)

    @pl.when(ti == 0)
    def _():
        hs_ref[...] = h0_ref[...]

    u2 = _bdot(pm_ref[...], u_ref[...].reshape(rows, S5_WIDTH)).astype(BF16)
    kc = S5_WIDTH // S5_CHUNKS
    ys = []
    for c, bu_ref in enumerate(bu_refs):
        lre = slice(c * 2 * half, c * 2 * half + half)
        lim = slice(c * 2 * half + half, (c + 1) * 2 * half)
        bu_ref[...] = _bdot(u2[:, c * kc:(c + 1) * kc], bm_ref[c])
        are = a_ref[:, lre]
        aim = a_ref[:, lim]
        hre = hs_ref[:, lre]
        him = hs_ref[:, lim]
        for t in range(tt):
            rsel = slice(t * BATCH_GROUP, (t + 1) * BATCH_GROUP)
            hre, him = (are * hre - aim * him + bu_ref[rsel, :half],
                        are * him + aim * hre + bu_ref[rsel, half:])
            bu_ref[rsel, :half] = hre
            bu_ref[rsel, half:] = him
        hs_ref[:, lre] = hre
        hs_ref[:, lim] = him
        ys.append(_bdot(bu_ref[...].astype(BF16), cm_ref[c]))
    y = jnp.concatenate(ys, axis=1) + d_ref[...] * u2.astype(F32)
    z = _gelu_tanh(y)
    gl = _bdot(z.astype(BF16), wg_ref[...]) + bg_ref[...]
    o = (z * _sigmoid(gl)).astype(BF16)
    ob = _bdot(pt_ref[...], o).astype(BF16)
    ya_ref[...] = _bdot(ob, wo_ref[...]).reshape(BATCH_GROUP, tt, D_MODEL).astype(BF16)

    @pl.when(ti == pl.num_programs(1) - 1)
    def _():
        ht_ref[...] = hs_ref[...]


def _s5(u, h0, a_lanes, bm, cm, d, wg, bg, wo):
    bsz, seq, _ = u.shape
    tt = min(S5_TIME_TILE, seq)
    rows = BATCH_GROUP * tt
    const = lambda shape: pl.BlockSpec(shape, lambda b, t: (0,) * len(shape))
    r = jnp.arange(rows)
    perm = ((r[:, None] % BATCH_GROUP) * tt + r[:, None] // BATCH_GROUP == r[None, :]).astype(BF16)
    return pl.pallas_call(
        functools.partial(_s5_kernel, tt=tt),
        grid=(bsz // BATCH_GROUP, seq // tt),
        in_specs=[pl.BlockSpec((BATCH_GROUP, tt, S5_WIDTH), lambda b, t: (b, t, 0)),
                  pl.BlockSpec((BATCH_GROUP, S5_LANES), lambda b, t: (b, 0)),
                  const((BATCH_GROUP, S5_LANES)), const((rows, rows)), const((rows, rows)),
                  const(bm.shape), const(cm.shape), const((1, S5_WIDTH)),
                  const((S5_WIDTH, S5_WIDTH)), const((1, S5_WIDTH)), const((S5_WIDTH, D_MODEL))],
        out_specs=[pl.BlockSpec((BATCH_GROUP, tt, D_MODEL), lambda b, t: (b, t, 0)),
                   pl.BlockSpec((BATCH_GROUP, S5_LANES), lambda b, t: (b, 0))],
        out_shape=[jax.ShapeDtypeStruct((bsz, seq, D_MODEL), BF16),
                   jax.ShapeDtypeStruct((bsz, S5_LANES), F32)],
        scratch_shapes=[pltpu.VMEM((BATCH_GROUP, S5_LANES), F32)]
        + [pltpu.VMEM((rows, S5_LANES // S5_CHUNKS), F32)] * S5_CHUNKS,
        compiler_params=_cparams("parallel", "arbitrary"),
        name="s5",
    )(u, h0, a_lanes, perm, perm.T, bm, cm, d, wg, bg, wo)


def _s5_params(a_re, a_im, log_dt, b_re, b_im, c_re, c_im):
    a_re = a_re.astype(F32)
    a_im = a_im.astype(F32)
    dt = jnp.exp(log_dt.astype(F32))[:, None]
    mag = jnp.exp(a_re * dt)
    ang = a_im * dt
    ab_re = mag * jnp.cos(ang)
    ab_im = mag * jnp.sin(ang)
    den = a_re * a_re + a_im * a_im
    nr = ab_re - 1.0
    ni = ab_im
    f_re = (nr * a_re + ni * a_im) / den
    f_im = (ni * a_re - nr * a_im) / den
    b_re = b_re.astype(F32)
    b_im = b_im.astype(F32)
    bb_re = f_re[..., None] * b_re - f_im[..., None] * b_im
    bb_im = f_re[..., None] * b_im + f_im[..., None] * b_re
    gpc = S5_GROUPS // S5_CHUNKS
    eye = jnp.eye(gpc, dtype=F32)

    def lanes(x):
        return x.reshape(S5_CHUNKS, gpc * S5_STATE)

    a_lanes = jnp.concatenate([lanes(ab_re), lanes(ab_im)], axis=1).reshape(1, S5_LANES)
    a_lanes = jnp.broadcast_to(a_lanes, (BATCH_GROUP, S5_LANES))

    def in_blocks(bb):
        bb = bb.reshape(S5_CHUNKS, gpc, S5_STATE, S5_GROUP)
        return jnp.einsum('cgpj,gh->cgjhp', bb, eye).reshape(S5_CHUNKS, gpc * S5_GROUP, gpc * S5_STATE)

    bm = jnp.concatenate([in_blocks(bb_re), in_blocks(bb_im)], axis=2).astype(BF16)

    def out_blocks(cc):
        cc = cc.astype(F32).reshape(S5_CHUNKS, gpc, S5_GROUP, S5_STATE)
        return jnp.einsum('cgjp,gh->cgphj', cc, eye).reshape(S5_CHUNKS, gpc * S5_STATE, gpc * S5_GROUP)

    cm = jnp.concatenate([out_blocks(c_re), -out_blocks(c_im)], axis=1).astype(BF16)
    return a_lanes, bm, cm


def _s5_state_to_lanes(h_re, h_im):
    bsz = h_re.shape[0]
    re = h_re.astype(F32).reshape(bsz, S5_CHUNKS, -1)
    im = h_im.astype(F32).reshape(bsz, S5_CHUNKS, -1)
    return jnp.concatenate([re, im], axis=2).reshape(bsz, S5_LANES)


def _s5_state_from_lanes(h):
    bsz = h.shape[0]
    h = h.reshape(bsz, S5_CHUNKS, 2, S5_GROUPS // S5_CHUNKS, S5_STATE)
    return (h[:, :, 0].reshape(bsz, S5_GROUPS, S5_STATE), h[:, :, 1].reshape(bsz, S5_GROUPS, S5_STATE))


def _ret_kernel(q_ref, k_ref, v_ref, g_ref, s0_ref, dm_ref, xi_ref, zeta_ref, wo_ref,
                yb_ref, st_ref, s_ref, *, block_decay):
    si = pl.program_id(1)

    @pl.when(si == 0)
    def _():
        s_ref[...] = s0_ref[...]

    blk = dm_ref.shape[1]
    states = [s_ref[h] for h in range(RET_HEADS)]
    gated = []
    for sub in range(q_ref.shape[0] // blk):
        rows = slice(sub * blk, (sub + 1) * blk)
        heads = []
        for h in range(RET_HEADS):
            qh = q_ref[rows, h * RET_DK:(h + 1) * RET_DK]
            kh = k_ref[rows, h * RET_DK:(h + 1) * RET_DK]
            vh = v_ref[rows, h * RET_DV:(h + 1) * RET_DV]
            scores = lax.dot_general(qh, kh, (((1,), (1,)), ((), ())), preferred_element_type=F32) * dm_ref[h]
            o = _bdot(scores.astype(BF16), vh) + _bdot(qh, states[h].astype(BF16)) * xi_ref[h]
            o = o * lax.rsqrt(jnp.mean(o * o, axis=-1, keepdims=True) + EPS)
            gh = g_ref[rows, h * RET_DV:(h + 1) * RET_DV].astype(F32)
            heads.append((o * (gh * _sigmoid(gh))).astype(BF16))
            kz = (kh.astype(F32) * zeta_ref[h]).astype(BF16)
            kv = lax.dot_general(kz, vh, (((0,), (0,)), ((), ())), preferred_element_type=F32)
            states[h] = block_decay[h] * states[h] + kv
        gated.append(jnp.concatenate(heads, axis=1))
    for h in range(RET_HEADS):
        s_ref[h] = states[h]
    yb_ref[...] = _bdot(jnp.concatenate(gated, axis=0), wo_ref[...]).astype(BF16)

    @pl.when(si == pl.num_programs(1) - 1)
    def _():
        st_ref[...] = s_ref[...]


def _ret_tables(seq):
    cl = min(CHUNK, seq)
    blk = min(RET_BLOCK, seq)
    log_g = jnp.log(1.0 - 2.0 ** (-5.0 - jnp.arange(RET_HEADS, dtype=F32)))
    idx = jnp.arange(blk, dtype=F32)
    diff = idx[:, None] - idx[None, :]
    cn = jnp.arange(blk)[:, None] // cl
    cm = jnp.arange(blk)[None, :] // cl
    expo = jnp.where(cm == cn, jnp.abs(diff), diff)
    dm = jnp.where(cm <= cn, jnp.exp(log_g[:, None, None] * expo[None]), 0.0)
    xi = jnp.exp(log_g[:, None] * (idx + 1.0)[None, :])[..., None]
    zeta = jnp.exp(log_g[:, None] * (blk - 1.0 - idx)[None, :])[..., None]
    block_decay = tuple(math.exp(math.log(1.0 - 2.0 ** (-5.0 - h)) * blk) for h in range(RET_HEADS))
    return blk, dm, xi, zeta, block_decay


def _retention(q, k, v, g, s0, w_ret_out_b):
    bsz, seq, _ = q.shape
    blk, dm, xi, zeta, block_decay = _ret_tables(seq)
    step = min(RET_BLOCKS_PER_STEP * blk, seq)
    row = lambda w: pl.BlockSpec((None, step, w), lambda b, s: (b, s, 0))
    const = lambda shape: pl.BlockSpec(shape, lambda b, s: (0,) * len(shape))
    state = pl.BlockSpec((None, RET_HEADS, RET_DK, RET_DV), lambda b, s: (b, 0, 0, 0))
    return pl.pallas_call(
        functools.partial(_ret_kernel, block_decay=block_decay),
        grid=(bsz, seq // step),
        in_specs=[row(RET_QK), row(RET_QK), row(RET_V), row(RET_V), state,
                  const(dm.shape), const(xi.shape), const(zeta.shape), const((RET_V, D_MODEL))],
        out_specs=[row(D_MODEL), state],
        out_shape=[jax.ShapeDtypeStruct((bsz, seq, D_MODEL), BF16),
                   jax.ShapeDtypeStruct((bsz, RET_HEADS, RET_DK, RET_DV), F32)],
        scratch_shapes=[pltpu.VMEM((RET_HEADS, RET_DK, RET_DV), F32)],
        compiler_params=_cparams("parallel", "arbitrary"),
        name="retention",
    )(q, k, v, g, s0, dm, xi, zeta, w_ret_out_b)


def _to_row_tiles(val):
    n, w = val.shape
    return val.reshape(n * (w // LANE), LANE)


def _from_row_tiles(tiles, w):
    return tiles.reshape(tiles.shape[0] // (w // LANE), w)


def _merge_kernel(x_ref, ya_ref, yb_ref, ga_ref, gb_ref, gt_ref, sh_ref, sc_ref, g2_ref, wo_ref,
                  wr_ref, br_ref, x1_ref, n2_ref, re_ref, wc_ref, cnt_ref):
    merged = _sigmoid(ga_ref[...]) * ya_ref[...] + _sigmoid(gb_ref[...]) * yb_ref[...]
    x1 = x_ref[...] + gt_ref[...] * _bdot(merged, wo_ref[...])
    x1_ref[...] = x1.astype(BF16)
    n2 = x1 * lax.rsqrt(jnp.mean(x1 * x1, axis=-1, keepdims=True) + EPS) * g2_ref[...]
    n2 = n2 * (1.0 + sc_ref[...]) + sh_ref[...]
    for j in range(ROW_TILES):
        n2_ref[pl.ds(j, n2.shape[0], stride=ROW_TILES), :] = n2[:, j * LANE:(j + 1) * LANE]

    nt_dot = lambda a, b: lax.dot_general(a, b, (((1,), (1,)), ((), ())), preferred_element_type=F32)
    n2_hi = n2.astype(BF16)
    n2_lo = (n2 - n2_hi.astype(F32)).astype(BF16)
    lt = (nt_dot(wr_ref[0], n2_hi) + nt_dot(wr_ref[0], n2_lo) + nt_dot(wr_ref[1], n2_hi)) + br_ref[...]
    tl = lt.shape[1]
    iota = lax.broadcasted_iota(jnp.int32, (8, tl), 0)
    gl = lt[0:8]
    gmax = jnp.max(gl, axis=0, keepdims=True)
    gi = jnp.min(jnp.where(gl == gmax, iota, 8), axis=0, keepdims=True)
    gw = 1.0 / jnp.sum(jnp.exp(gl - gmax), axis=0, keepdims=True)
    el = jnp.zeros((8, tl), F32)
    for g in range(MOE_GROUPS):
        el = jnp.where(gi == g, lt[8 * (g + 1):8 * (g + 2)], el)
    m1 = jnp.max(el, axis=0, keepdims=True)
    i1 = jnp.min(jnp.where(el == m1, iota, 8), axis=0, keepdims=True)
    el2 = jnp.where(iota == i1, -jnp.inf, el)
    m2 = jnp.max(el2, axis=0, keepdims=True)
    i2 = jnp.min(jnp.where(el2 == m2, iota, 8), axis=0, keepdims=True)
    e21 = jnp.exp(m2 - m1)
    w1 = gw / (1.0 + e21)
    w2 = w1 * e21
    lo = jnp.minimum(i1, i2)
    hi = jnp.maximum(i1, i2)
    pair = gi * PAIRS_PER_GROUP + lo * (MOE_EXPERTS - 1) - ((lo * (lo - 1)) >> 1) + (hi - lo - 1)
    re_ref[...] = jnp.where(iota == 0, pair, 0)
    rw = jnp.where(iota == 0, jnp.where(i1 < i2, w1, w2), jnp.where(iota == 1, jnp.where(i1 < i2, w2, w1), 0.0))
    eye = (lax.broadcasted_iota(jnp.int32, (8, LANE), 0)
           == lax.broadcasted_iota(jnp.int32, (8, LANE), 1)).astype(F32)
    wc_ref[...] = lax.dot_general(rw, eye, (((0,), (0,)), ((), ())),
                                  precision=lax.Precision.HIGHEST, preferred_element_type=F32)
    ids = lax.broadcasted_iota(jnp.int32, (N_PAIRS, tl), 0)
    hits = jnp.where(ids == pair, 1.0, 0.0)
    cnt_ref[...] = jnp.broadcast_to(jnp.sum(hits, axis=1, keepdims=True), (N_PAIRS, LANE)).astype(jnp.int32)


def _merge(x, ya, yb, ga, gb, gate1, shift2, scale2, g2, w_out_b, wr, br):
    bsz, seq, _ = x.shape
    tl = min(TOKEN_TILE, seq)
    row = pl.BlockSpec((None, tl, D_MODEL), lambda b, t: (b, t, 0))
    vec = _mod_spec(gate1, tl)
    const = lambda shape: pl.BlockSpec(shape, lambda b, t: (0,) * len(shape))
    route = pl.BlockSpec((None, 8, tl), lambda b, t: (b, 0, t))
    nt = seq // tl
    tiles = pl.BlockSpec((tl * ROW_TILES, LANE), lambda b, t: (b * nt + t, 0))
    return pl.pallas_call(
        _merge_kernel,
        grid=(bsz, nt),
        in_specs=[row, row, row, row, row, vec, vec, vec, const((1, D_MODEL)),
                  const((D_MODEL, D_MODEL)), const((2, ROUTE_ROWS, D_MODEL)), const((ROUTE_ROWS, 1))],
        out_specs=[row, tiles, route,
                   pl.BlockSpec((tl, LANE), lambda b, t: (b * nt + t, 0)),
                   pl.BlockSpec((None, N_PAIRS, LANE), lambda b, t: (b * nt + t, 0, 0))],
        out_shape=[jax.ShapeDtypeStruct((bsz, seq, D_MODEL), BF16),
                   jax.ShapeDtypeStruct((bsz * seq * ROW_TILES, LANE), F32),
                   jax.ShapeDtypeStruct((bsz, 8, seq), jnp.int32),
                   jax.ShapeDtypeStruct((bsz * seq, LANE), F32),
                   jax.ShapeDtypeStruct((bsz * nt, N_PAIRS, LANE), jnp.int32)],
        compiler_params=_cparams("parallel", "parallel"),
        name="merge",
    )(x, ya, yb, ga, gb, gate1, shift2, scale2, g2, w_out_b, wr, br)


def _router_weights(w_rg, b_rg, w_re, b_re):
    wr = jnp.zeros((ROUTE_ROWS, D_MODEL), F32)
    wr = wr.at[0:MOE_GROUPS].set(w_rg.astype(F32).T)
    wr = wr.at[8:].set(jnp.transpose(w_re.astype(F32), (0, 2, 1)).reshape(N_EXPERTS, D_MODEL))
    br = jnp.full((ROUTE_ROWS,), -1e30, F32)
    br = br.at[0:MOE_GROUPS].set(b_rg.astype(F32))
    br = br.at[8:].set(b_re.astype(F32).reshape(N_EXPERTS))
    wr_hi = wr.astype(BF16)
    wr_lo = (wr - wr_hi.astype(F32)).astype(BF16)
    return jnp.stack([wr_hi, wr_lo]), br.reshape(ROUTE_ROWS, 1)


def _row_gather(idx_ref, src_hbm, buf, sem, slot, n, span):
    def body(pair, carry):
        for queue in range(DMA_QUEUES):
            r = pair * DMA_QUEUES + queue
            src = pl.multiple_of(idx_ref[0, r] * span, span)
            dst = pl.multiple_of(r * span, span)
            pltpu.make_async_copy(src_hbm.at[pl.ds(src, span), :],
                                  buf.at[slot, pl.ds(dst, span), :], sem.at[slot]).start(priority=queue)
        return carry
    lax.fori_loop(0, n // DMA_QUEUES, body, 0, unroll=DMA_UNROLL // DMA_QUEUES)


def _row_gather_wait(src_hbm, buf, sem, slot, n, span):
    pltpu.make_async_copy(src_hbm.at[pl.ds(0, n * span), :], buf.at[slot], sem.at[slot]).wait()


def _plan_kernel(e_ref, base_ref, c0_ref, tri_ref, pos_ref, carry_ref):
    @pl.when(pl.program_id(0) == 0)
    def _():
        carry_ref[...] = c0_ref[...]

    tb = e_ref.shape[1]
    hit = lax.broadcasted_iota(jnp.int32, (N_PAIRS, tb), 0) == e_ref[0:1, :]
    incl = _bdot(jnp.where(hit, 1.0, 0.0).astype(BF16), tri_ref[...])
    row = jnp.where(hit, incl - 1.0 + carry_ref[...] + base_ref[...], 0.0)
    pos_ref[...] = jnp.sum(row, axis=0, keepdims=True).astype(jnp.int32)
    carry_ref[...] = carry_ref[...] + incl[:, tb - 1:tb]


def _plan(route_e, pad_start, placed):
    bsz, _, seq = route_e.shape
    tb = min(ROUTE_TILE, seq)
    nb = seq // tb
    r = jnp.arange(tb)
    tri = (r[:, None] <= r[None, :]).astype(BF16)
    col = pl.BlockSpec((N_PAIRS, 1), lambda i: (0, 0))
    return pl.pallas_call(
        _plan_kernel,
        grid=(bsz * nb,),
        in_specs=[pl.BlockSpec((None, 8, tb), lambda i: (i // nb, 0, i % nb)), col, col,
                  pl.BlockSpec((tb, tb), lambda i: (0, 0))],
        out_specs=pl.BlockSpec((None, 1, tb), lambda i: (i, 0, 0)),
        out_shape=jax.ShapeDtypeStruct((bsz * nb, 1, tb), jnp.int32),
        scratch_shapes=[pltpu.VMEM((N_PAIRS, 1), F32)],
        compiler_params=_cparams("arbitrary"),
        name="plan",
    )(route_e, pad_start.astype(F32).reshape(N_PAIRS, 1), placed.astype(F32).reshape(N_PAIRS, 1), tri)


ZERO_CHUNKS = tuple(2 ** b for b in range(8, -1, -1))


def _dispatch_kernel(lo_ref, hi_ref, nu_ref, posa_ref, xa_ref, posb_ref, xb_ref, xs_hbm, zero_ref, sem,
                     *, tb, nb_rows, tm, n_tiles):
    last = pl.num_programs(0) - 1

    def scatter(pos_ref, x_ref, n):
        def body(pair, carry):
            for queue in range(DMA_QUEUES):
                r = pair * DMA_QUEUES + queue
                src = pl.multiple_of(r * ROW_TILES, ROW_TILES)
                dst = pl.multiple_of(pos_ref[0, r] * ROW_TILES, ROW_TILES)
                pltpu.make_async_copy(x_ref.at[pl.ds(src, ROW_TILES), :],
                                      xs_hbm.at[pl.ds(dst, ROW_TILES), :], sem.at[0]).start(priority=queue)
            return carry
        lax.fori_loop(0, n // DMA_QUEUES, body, 0, unroll=DMA_UNROLL // DMA_QUEUES)

    def scatter_wait(x_ref, n):
        pltpu.make_async_copy(x_ref, xs_hbm.at[pl.ds(0, n * ROW_TILES), :], sem.at[0]).wait()

    def zero_copy(row, size):
        dst = pl.multiple_of(row * ROW_TILES, ROW_TILES)
        return pltpu.make_async_copy(zero_ref.at[pl.ds(0, size * ROW_TILES), :],
                                     xs_hbm.at[pl.ds(dst, size * ROW_TILES), :], sem.at[1])

    @pl.when(pl.program_id(0) < last)
    def _():
        scatter(posa_ref, xa_ref, tb)
        scatter_wait(xa_ref, tb)

    @pl.when(pl.program_id(0) == last)
    def _():
        scatter(posb_ref, xb_ref, nb_rows)
        zero_ref[...] = jnp.zeros_like(zero_ref)

        def per_pair(e, carry):
            lo = lo_ref[e]
            n = hi_ref[e] - lo
            for wait in (False, True):
                row = lo
                for size in ZERO_CHUNKS:
                    @pl.when((n & size) != 0)
                    def _(row=row, size=size, wait=wait):
                        zero_copy(row, size).wait() if wait else zero_copy(row, size).start()
                    row = row + (n & size)
            return carry
        lax.fori_loop(0, N_PAIRS, per_pair, 0)

        chunk = min(ZERO_CHUNKS[0], tm)
        for wait in (False, True):
            def per_tile(t, carry, wait=wait):
                for c in range(tm // chunk):
                    cp = zero_copy(t * tm + c * chunk, chunk)
                    cp.wait() if wait else cp.start()
                return carry
            lax.fori_loop(nu_ref[0], n_tiles, per_tile, 0)
        scatter_wait(xb_ref, nb_rows)


def _dispatch(pos_a, rows_a, pos_b, rows_b, pad_lo, pad_hi, n_used, n_rows, tm):
    tb = min(DISPATCH_TILE, pos_a.size)
    na = pos_a.size // tb
    pos_a = pos_a.reshape(na, 1, tb)
    nb_rows = rows_b.shape[0] // ROW_TILES
    pos_b = pos_b.reshape(1, 1, nb_rows)
    block_a = lambda i, lo, hi, nu: (jnp.minimum(i, na - 1), 0, 0)
    grid_spec = pltpu.PrefetchScalarGridSpec(
        num_scalar_prefetch=3,
        grid=(na + 1,),
        in_specs=[pl.BlockSpec((None, 1, tb), block_a, memory_space=pltpu.SMEM),
                  pl.BlockSpec((tb * ROW_TILES, LANE), lambda i, lo, hi, nu: (jnp.minimum(i, na - 1), 0)),
                  pl.BlockSpec((None, 1, nb_rows), lambda i, lo, hi, nu: (0, 0, 0), memory_space=pltpu.SMEM),
                  pl.BlockSpec((nb_rows * ROW_TILES, LANE), lambda i, lo, hi, nu: (0, 0))],
        out_specs=pl.BlockSpec(memory_space=pl.ANY),
        scratch_shapes=[pltpu.VMEM((ZERO_CHUNKS[0] * ROW_TILES, LANE), F32), pltpu.SemaphoreType.DMA((2,))],
    )
    return pl.pallas_call(
        functools.partial(_dispatch_kernel, tb=tb, nb_rows=nb_rows, tm=tm, n_tiles=n_rows // tm),
        grid_spec=grid_spec,
        out_shape=jax.ShapeDtypeStruct((n_rows * ROW_TILES, LANE), F32),
        compiler_params=_cparams("arbitrary"),
        name="dispatch",
    )(pad_lo, pad_hi, n_used, pos_a, rows_a, pos_b, rows_b)


def _expert_kernel(ta_ref, tb_ref, nu_ref, x_ref, w1a_ref, w3a_ref, w2a_ref, w1b_ref, w3b_ref, w2b_ref,
                   y_ref, *, tm):
    @pl.when(pl.program_id(0) < nu_ref[0])
    def _():
        xb = _from_row_tiles(x_ref[...], D_MODEL).astype(BF16)
        ys = []
        for w1_ref, w3_ref, w2_ref in ((w1a_ref, w3a_ref, w2a_ref), (w1b_ref, w3b_ref, w2b_ref)):
            a = _bdot(xb, w1_ref[...])
            h = a * _sigmoid(a) * _bdot(xb, w3_ref[...])
            ys.append(_bdot(h.astype(BF16), w2_ref[...]))
        y_ref[...] = _to_row_tiles(jnp.concatenate(ys, axis=1))

    @pl.when(pl.program_id(0) >= nu_ref[0])
    def _():
        y_ref[...] = jnp.zeros_like(y_ref)


def _experts(xs, tile_a, tile_b, n_used, w1b, w3b, w2b, tm):
    n_tiles = xs.shape[0] // (tm * ROW_TILES)
    tile = lambda i, ta, tb, nu: (jnp.maximum(jnp.minimum(i, nu[0] - 1), 0), 0)
    first = lambda i, ta, tb, nu: (ta[i], 0, 0)
    second = lambda i, ta, tb, nu: (tb[i], 0, 0)
    up = lambda which: pl.BlockSpec((None, D_MODEL, MOE_FF), which)
    down = lambda which: pl.BlockSpec((None, MOE_FF, D_MODEL), which)
    grid_spec = pltpu.PrefetchScalarGridSpec(
        num_scalar_prefetch=3,
        grid=(n_tiles,),
        in_specs=[pl.BlockSpec((tm * ROW_TILES, LANE), tile),
                  up(first), up(first), down(first), up(second), up(second), down(second)],
        out_specs=pl.BlockSpec((tm * 2 * ROW_TILES, LANE), lambda i, ta, tb, nu: (i, 0)),
    )
    return pl.pallas_call(
        functools.partial(_expert_kernel, tm=tm),
        grid_spec=grid_spec,
        out_shape=jax.ShapeDtypeStruct((2 * xs.shape[0], LANE), F32),
        compiler_params=_cparams("arbitrary"),
        name="experts",
    )(tile_a, tile_b, n_used, xs, w1b, w3b, w2b, w1b, w3b, w2b)


def _expert_layout(counts, tm, n_tok):
    n_tiles = n_tok // tm + MOE_GROUPS * PAIRS_PER_GROUP
    padded = ((counts + tm - 1) // tm) * tm
    pad_end = jnp.cumsum(padded)
    pad_start = pad_end - padded
    n_used = pad_end[-1] // tm
    tile_start = jnp.minimum(jnp.arange(n_tiles, dtype=jnp.int32), n_used - 1) * tm
    tile_pair = jnp.sum((pad_end[None, :] <= tile_start[:, None]).astype(jnp.int32), axis=1)
    tile_pair = jnp.minimum(tile_pair, MOE_GROUPS * PAIRS_PER_GROUP - 1)
    members = [(lo, hi) for lo in range(MOE_EXPERTS) for hi in range(lo + 1, MOE_EXPERTS)]
    lo_of = jnp.asarray([m[0] for m in members], jnp.int32)
    hi_of = jnp.asarray([m[1] for m in members], jnp.int32)
    group = tile_pair // PAIRS_PER_GROUP
    tile_a = group * MOE_EXPERTS + lo_of[tile_pair % PAIRS_PER_GROUP]
    tile_b = group * MOE_EXPERTS + hi_of[tile_pair % PAIRS_PER_GROUP]
    return (pad_start, pad_start + counts, pad_end, tile_a.astype(jnp.int32), tile_b.astype(jnp.int32),
            n_used.astype(jnp.int32).reshape(1), n_tiles * tm)


def _combine_kernel(cur_ref, nxt_ref, x1_ref, wc_ref, gt_ref, fn_ref, y_hbm, o_ref, ybuf, sem, *, tl):
    i = pl.program_id(0)
    slot = i % 2

    span = 2 * ROW_TILES

    @pl.when(i == 0)
    def _():
        _row_gather(cur_ref, y_hbm, ybuf, sem, 0, tl, span)

    @pl.when(i + 1 < pl.num_programs(0))
    def _():
        _row_gather(nxt_ref, y_hbm, ybuf, sem, 1 - slot, tl, span)

    _row_gather_wait(y_hbm, ybuf, sem, slot, tl, span)
    both = _from_row_tiles(ybuf[slot], 2 * D_MODEL)
    moe = wc_ref[:, 0:1] * both[:, :D_MODEL] + wc_ref[:, 1:2] * both[:, D_MODEL:]
    x2 = x1_ref[...].astype(F32) + gt_ref[...] * moe
    o_ref[...] = x2 * lax.rsqrt(jnp.mean(x2 * x2, axis=-1, keepdims=True) + EPS) * fn_ref[...]


def _combine(x1_flat, pos, y_rows, wcol, gate2, final_norm, seq):
    n_tok = x1_flat.shape[0]
    n_tiles, _, tl = pos.shape
    last = n_tiles - 1
    if gate2.shape[1] == 1:
        gate_spec = pl.BlockSpec((None, 1, D_MODEL), lambda i: ((i * tl) // seq, 0, 0))
    else:
        gate2 = gate2.reshape(n_tok, D_MODEL)
        gate_spec = pl.BlockSpec((tl, D_MODEL), lambda i: (i, 0))
    return pl.pallas_call(
        functools.partial(_combine_kernel, tl=tl),
        grid=(n_tiles,),
        in_specs=[pl.BlockSpec((None, 1, tl), lambda i: (i, 0, 0), memory_space=pltpu.SMEM),
                  pl.BlockSpec((None, 1, tl), lambda i: (jnp.minimum(i + 1, last), 0, 0),
                               memory_space=pltpu.SMEM),
                  pl.BlockSpec((tl, D_MODEL), lambda i: (i, 0)),
                  pl.BlockSpec((tl, LANE), lambda i: (i, 0)),
                  gate_spec,
                  pl.BlockSpec((1, D_MODEL), lambda i: (0, 0)),
                  pl.BlockSpec(memory_space=pl.ANY)],
        out_specs=pl.BlockSpec((tl, D_MODEL), lambda i: (i, 0)),
        out_shape=jax.ShapeDtypeStruct((n_tok, D_MODEL), F32),
        scratch_shapes=[pltpu.VMEM((2, 2 * tl * ROW_TILES, LANE), F32), pltpu.SemaphoreType.DMA((2,))],
        compiler_params=_cparams("arbitrary"),
        name="combine",
    )(pos, pos, x1_flat, wcol, gate2, final_norm, y_rows)


def _rope_tables(seq, offset):
    half = RET_DK // 2
    theta = 1.0 / (ROPE_BASE ** jnp.linspace(0.0, 1.0, half, dtype=F32))
    pos = offset + jnp.arange(seq)
    ang = pos.astype(F32)[:, None] * theta[None, :]
    cos = jnp.cos(ang)
    sin = jnp.sin(ang)
    return jnp.concatenate([cos, cos], axis=1), jnp.concatenate([-sin, sin], axis=1)


def _mixers(x, mod, h0, s0, offset, p):
    bsz, seq, _ = x.shape
    mods = [m.reshape(bsz, 1, D_MODEL) for m in jnp.split(mod, 6, axis=-1)]
    cos2, sin2 = _rope_tables(seq, offset)
    flat = seq < TOKEN_TILE and bsz * seq <= TOKEN_TILE
    sets, rows = (1, bsz * seq) if flat else (bsz, seq)
    tok = lambda a: a.reshape(sets, rows, a.shape[-1])
    if flat:
        mods = [jnp.broadcast_to(m, (bsz, seq, D_MODEL)).reshape(1, rows, D_MODEL) for m in mods]
        cos2, sin2 = jnp.tile(cos2, (bsz, 1)), jnp.tile(sin2, (bsz, 1))
    sh1, sc1, gt1, sh2, sc2, gt2 = mods
    seqs = lambda a: a.reshape(bsz, seq, a.shape[-1])
    u, q, k, v, g, ga, gb = _inproj(tok(x), sh1, sc1, p['norm1'], p['w_in'], cos2, sin2)
    ya, h_t = _s5(seqs(u), h0, p['a_lanes'], p['bm'], p['cm'], p['d'], p['w_glu'], p['b_glu'], p['w_s5_out'])
    yb, s_t = _retention(seqs(q), seqs(k), seqs(v), seqs(g), s0, p['w_ret_out'])
    x1, n2, route_e, wcol, cnt = _merge(tok(x), tok(ya), tok(yb), ga, gb, gt1, sh2, sc2, p['norm2'],
                                        p['w_out'], p['wr'], p['br'])
    return dict(x1=x1.reshape(bsz * seq, D_MODEL), n2=n2, route=route_e, wcol=wcol,
                counts=jnp.sum(cnt[:, :, 0], axis=0), gate2=gt2, seq=rows, h=h_t, s=s_t)


def _moe(a, b, p, final_norm):
    n_tok = a['x1'].shape[0] + b['x1'].shape[0]
    pad_start, pad_lo, pad_hi, tile_a, tile_b, n_used, n_rows = _expert_layout(
        a['counts'] + b['counts'], EXPERT_TILE, n_tok)
    pos_a = _plan(a['route'], pad_start, jnp.zeros_like(a['counts']))
    pos_b = _plan(b['route'], pad_start, a['counts'])
    xs = _dispatch(pos_a, a['n2'], pos_b, b['n2'], pad_lo, pad_hi, n_used, n_rows, EXPERT_TILE)
    y_rows = _experts(xs, tile_a, tile_b, n_used, p['w1'], p['w3'], p['w2'], EXPERT_TILE)
    return [_combine(m['x1'], pos, y_rows, m['wcol'], m['gate2'], final_norm, m['seq'])
            for m, pos in ((a, pos_a), (b, pos_b))]


def kernel(x_prompt, x_sample, state_s5_re, state_s5_im, state_ret, c_prompt, c_sample, w_ada, b_ada, norm1, norm2, w_in, s5_a_re, s5_a_im, s5_log_dt, s5_b_re, s5_b_im, s5_c_re, s5_c_im, s5_d, s5_w_glu, s5_b_glu, w_s5_out, w_ret_out, w_out, w_rg, b_rg, w_re, b_re, w1, w3, w2, final_norm):
    depth = w_ada.shape[0]
    assert depth == 1
    bp = x_prompt.shape[0]
    bs, seq_s, _ = x_sample.shape
    l = 0
    a_lanes, bm, cm = _s5_params(s5_a_re[l], s5_a_im[l], s5_log_dt[l], s5_b_re[l], s5_b_im[l],
                                 s5_c_re[l], s5_c_im[l])
    wr, br = _router_weights(w_rg[l], b_rg[l], w_re[l], b_re[l])
    p = dict(
        norm1=norm1[l].astype(F32).reshape(1, D_MODEL), norm2=norm2[l].astype(F32).reshape(1, D_MODEL),
        w_in=w_in[l].astype(BF16), a_lanes=a_lanes, bm=bm, cm=cm,
        d=s5_d[l].astype(F32).reshape(1, S5_WIDTH), w_glu=s5_w_glu[l].astype(BF16),
        b_glu=s5_b_glu[l].astype(F32).reshape(1, S5_WIDTH), w_s5_out=w_s5_out[l].astype(BF16),
        w_ret_out=w_ret_out[l].astype(BF16), w_out=w_out[l].astype(BF16), wr=wr, br=br,
        w1=w1[l].astype(BF16).reshape(N_EXPERTS, D_MODEL, MOE_FF),
        w3=w3[l].astype(BF16).reshape(N_EXPERTS, D_MODEL, MOE_FF),
        w2=w2[l].astype(BF16).reshape(N_EXPERTS, MOE_FF, D_MODEL))
    fn = final_norm.astype(F32).reshape(1, D_MODEL)
    mod = _mod(jnp.concatenate([c_prompt, c_sample], axis=0).astype(F32), w_ada[l], b_ada[l])

    h0_p = jnp.zeros((bp, S5_LANES), F32)
    s0_p = jnp.zeros((bp, RET_HEADS, RET_DK, RET_DV), F32)
    prompt = _mixers(x_prompt, mod[:bp], h0_p, s0_p, 0, p)
    h0_s = _s5_state_to_lanes(state_s5_re[l], state_s5_im[l])
    sample = _mixers(x_sample, mod[bp:], h0_s, state_ret[l].astype(F32), PAST_LEN, p)
    y_p, y_s = _moe(prompt, sample, p, fn)
    p_re, p_im = _s5_state_from_lanes(prompt['h'])
    s_re, s_im = _s5_state_from_lanes(sample['h'])
    return (y_p.reshape(x_prompt.shape), y_s.reshape(x_sample.shape), p_re[None], p_im[None],
            prompt['s'][None], s_re[None], s_im[None], sample['s'][None])
```

```python
import functools
import math

import jax
import jax.numpy as jnp
from jax import lax
from jax.experimental import pallas as pl
from jax.experimental.pallas import tpu as pltpu

F32 = jnp.float32
BF16 = jnp.bfloat16

D_MODEL = 1024
PAST_LEN = 2048
CHUNK = 64
S5_WIDTH = 512
S5_GROUP = 16
S5_GROUPS = 32
S5_STATE = 64
S5_LANES = 2 * S5_GROUPS * S5_STATE
S5_CHUNKS = 4
RET_HEADS = 4
RET_DK = 128
RET_DV = 256
RET_QK = RET_HEADS * RET_DK
RET_V = RET_HEADS * RET_DV
ROPE_BASE = 10000.0
MOE_GROUPS = 4
MOE_EXPERTS = 8
N_EXPERTS = MOE_GROUPS * MOE_EXPERTS
PAIRS_PER_GROUP = MOE_EXPERTS * (MOE_EXPERTS - 1) // 2
N_PAIRS = 128
MOE_FF = 256
EPS = 1e-6
IN_WIDTH = S5_WIDTH + 2 * RET_QK + 2 * RET_V + 2 * D_MODEL
ROUTE_ROWS = 8 * (1 + MOE_GROUPS)

BATCH_GROUP = 8
S5_GROUPS_PER_STEP = 2
TOKEN_TILE = 512
MERGE_PARTS = 1
INPROJ_TILE = 512
S5_TIME_TILE = 64
RET_BLOCK = 256
RET_BLOCKS_PER_STEP = 4
EXPERT_TILE = 256
ROUTE_TILE = 512
DISPATCH_TILE = 2048
VMEM_LIMIT = 56 * 1024 * 1024
LANE = 128
SUBLANE = 8
ROW_TILES = D_MODEL // LANE
DMA_UNROLL = 8
DMA_QUEUES = 2


def _cparams(*sem):
    return pltpu.CompilerParams(dimension_semantics=sem, vmem_limit_bytes=VMEM_LIMIT)


def _bdot(a, b):
    return jnp.dot(a, b, preferred_element_type=F32)


def _sigmoid(x):
    return 0.5 * jnp.tanh(0.5 * x) + 0.5


def _mod_kernel(c_ref, w_ref, b_ref, o_ref):
    c = c_ref[...]
    a = (c * _sigmoid(c)).astype(BF16)
    o_ref[...] = _bdot(a, w_ref[...].astype(BF16)) + b_ref[...]


def _mod(c, w_ada, b_ada):
    n = c.shape[0]
    return pl.pallas_call(
        _mod_kernel,
        grid=(6,),
        in_specs=[pl.BlockSpec((n, D_MODEL), lambda j: (0, 0)),
                  pl.BlockSpec((D_MODEL, D_MODEL), lambda j: (0, j)),
                  pl.BlockSpec((1, D_MODEL), lambda j: (0, j))],
        out_specs=pl.BlockSpec((n, D_MODEL), lambda j: (0, j)),
        out_shape=jax.ShapeDtypeStruct((n, 6 * D_MODEL), F32),
        compiler_params=_cparams("parallel"),
        name="mod",
    )(c, w_ada, b_ada.reshape(1, -1))


def _rope(x, cos2, sin2):
    return x * cos2 + pltpu.roll(x, RET_DK // 2, 1) * sin2


def _inproj_kernel(x_ref, sh_ref, sc_ref, g_ref, w_ref, cos_ref, sin_ref,
                   u_ref, q_ref, k_ref, v_ref, gs_ref, ga_ref, gb_ref):
    x = x_ref[...]
    n = x * lax.rsqrt(jnp.mean(x * x, axis=-1, keepdims=True) + EPS) * g_ref[...]
    nb = (n * (1.0 + sc_ref[...]) + sh_ref[...]).astype(BF16)
    cos2 = cos_ref[...]
    sin2 = sin_ref[...]
    o = 0
    u_ref[...] = _bdot(nb, w_ref[:, o:o + S5_WIDTH]).astype(BF16)
    o += S5_WIDTH
    q = _bdot(nb, w_ref[:, o:o + RET_QK])
    for h in range(RET_HEADS):
        head = slice(h * RET_DK, (h + 1) * RET_DK)
        q_ref[:, head] = _rope(q[:, head], cos2, sin2).astype(BF16)
    o += RET_QK
    k = _bdot(nb, w_ref[:, o:o + RET_QK])
    for h in range(RET_HEADS):
        head = slice(h * RET_DK, (h + 1) * RET_DK)
        k_ref[:, head] = (_rope(k[:, head], cos2, sin2) * (RET_DK ** -0.5)).astype(BF16)
    o += RET_QK
    for ref in (v_ref, gs_ref, ga_ref, gb_ref):
        ref[...] = _bdot(nb, w_ref[:, o:o + D_MODEL]).astype(BF16)
        o += D_MODEL


def _mod_spec(vec, tl):
    if vec.shape[1] == 1:
        return pl.BlockSpec((None, 1, D_MODEL), lambda b, t: (b, 0, 0))
    return pl.BlockSpec((None, tl, D_MODEL), lambda b, t: (b, t, 0))


def _inproj(x, shift, scale, g1, w_in_b, cos2, sin2):
    bsz, seq, _ = x.shape
    tl = min(INPROJ_TILE, seq)
    row = lambda w: pl.BlockSpec((None, tl, w), lambda b, t: (b, t, 0))
    vec = _mod_spec(shift, tl)
    shapes = [S5_WIDTH, RET_QK, RET_QK, RET_V, RET_V, D_MODEL, D_MODEL]
    return pl.pallas_call(
        _inproj_kernel,
        grid=(bsz, seq // tl),
        in_specs=[row(D_MODEL), vec, vec,
                  pl.BlockSpec((1, D_MODEL), lambda b, t: (0, 0)),
                  pl.BlockSpec((D_MODEL, IN_WIDTH), lambda b, t: (0, 0), pipeline_mode=pl.Buffered(1)),
                  pl.BlockSpec((tl, RET_DK), lambda b, t: (t, 0)),
                  pl.BlockSpec((tl, RET_DK), lambda b, t: (t, 0))],
        out_specs=[row(w) for w in shapes],
        out_shape=[jax.ShapeDtypeStruct((bsz, seq, w), BF16) for w in shapes],
        compiler_params=_cparams("parallel", "parallel"),
        name="inproj",
    )(x, shift, scale, g1, w_in_b, cos2, sin2)


def _gelu_tanh(y):
    return 0.5 * y * (1.0 + jnp.tanh(math.sqrt(2.0 / math.pi) * (y + 0.044715 * (y * y * y))))


def _s5_kernel(u_ref, h0_ref, a_ref, pm_ref, pt_ref, bm_ref, cm_ref, d_ref, wg_ref, bg_ref, wo_ref,
               ya_ref, ht_ref, hs_ref, *bu_refs, tt):
    ti = pl.program_id(1)
    rows = BATCH_GROUP * tt
    half = S5_LANES // (2 * S5_CHUNKS)

    @pl.when(ti == 0)
    def _():
        hs_ref[...] = h0_ref[...]

    kc = S5_WIDTH // S5_CHUNKS
    n_grp = u_ref.shape[0] // BATCH_GROUP
    streams = lambda g: slice(g * BATCH_GROUP, (g + 1) * BATCH_GROUP)
    u2, ys = {}, {g: [] for g in range(n_grp)}

    def permute(g):
        u2[g] = _bdot(pm_ref[...], u_ref[streams(g)].reshape(rows, S5_WIDTH)).astype(BF16)

    def input_map(g, c):
        bu_refs[g * S5_CHUNKS + c][...] = _bdot(u2[g][:, c * kc:(c + 1) * kc], bm_ref[c])

    def recurrence(g, c):
        bu_ref = bu_refs[g * S5_CHUNKS + c]
        lre = slice(c * 2 * half, c * 2 * half + half)
        lim = slice(c * 2 * half + half, (c + 1) * 2 * half)
        are = a_ref[:, lre]
        aim = a_ref[:, lim]
        hre = hs_ref[streams(g), lre]
        him = hs_ref[streams(g), lim]
        for t in range(tt):
            rsel = slice(t * BATCH_GROUP, (t + 1) * BATCH_GROUP)
            hre, him = (are * hre - aim * him + bu_ref[rsel, :half],
                        are * him + aim * hre + bu_ref[rsel, half:])
            bu_ref[rsel, :half] = hre
            bu_ref[rsel, half:] = him
        hs_ref[streams(g), lre] = hre
        hs_ref[streams(g), lim] = him

    def output_map(g, c):
        ys[g].append(_bdot(bu_refs[g * S5_CHUNKS + c][...].astype(BF16), cm_ref[c]))

    def tail(g):
        y = jnp.concatenate(ys[g], axis=1) + d_ref[...] * u2[g].astype(F32)
        z = _gelu_tanh(y)
        gl = _bdot(z.astype(BF16), wg_ref[...]) + bg_ref[...]
        o = (z * _sigmoid(gl)).astype(BF16)
        ob = _bdot(pt_ref[...], o).astype(BF16)
        ya_ref[streams(g)] = _bdot(ob, wo_ref[...]).reshape(BATCH_GROUP, tt, D_MODEL).astype(BF16)

    permute(0)
    for c in range(S5_CHUNKS):
        input_map(0, c)
    for g in range(n_grp + 1):
        if g + 1 < n_grp:
            permute(g + 1)
        for c in range(S5_CHUNKS):
            if g >= 1:
                output_map(g - 1, c)
            if g + 1 < n_grp:
                input_map(g + 1, c)
            if g < n_grp:
                recurrence(g, c)
        if g >= 1:
            tail(g - 1)

    @pl.when(ti == pl.num_programs(1) - 1)
    def _():
        ht_ref[...] = hs_ref[...]


def _s5(u, h0, a_lanes, bm, cm, d, wg, bg, wo):
    bsz, seq, _ = u.shape
    tt = min(S5_TIME_TILE, seq)
    rows = BATCH_GROUP * tt
    groups = S5_GROUPS_PER_STEP if bsz % (BATCH_GROUP * S5_GROUPS_PER_STEP) == 0 else 1
    nb = BATCH_GROUP * groups
    const = lambda shape: pl.BlockSpec(shape, lambda b, t: (0,) * len(shape))
    r = jnp.arange(rows)
    perm = ((r[:, None] % BATCH_GROUP) * tt + r[:, None] // BATCH_GROUP == r[None, :]).astype(BF16)
    return pl.pallas_call(
        functools.partial(_s5_kernel, tt=tt),
        grid=(bsz // nb, seq // tt),
        in_specs=[pl.BlockSpec((nb, tt, S5_WIDTH), lambda b, t: (b, t, 0)),
                  pl.BlockSpec((nb, S5_LANES), lambda b, t: (b, 0)),
                  const((BATCH_GROUP, S5_LANES)), const((rows, rows)), const((rows, rows)),
                  const(bm.shape), const(cm.shape), const((1, S5_WIDTH)),
                  const((S5_WIDTH, S5_WIDTH)), const((1, S5_WIDTH)), const((S5_WIDTH, D_MODEL))],
        out_specs=[pl.BlockSpec((nb, tt, D_MODEL), lambda b, t: (b, t, 0)),
                   pl.BlockSpec((nb, S5_LANES), lambda b, t: (b, 0))],
        out_shape=[jax.ShapeDtypeStruct((bsz, seq, D_MODEL), BF16),
                   jax.ShapeDtypeStruct((bsz, S5_LANES), F32)],
        scratch_shapes=[pltpu.VMEM((nb, S5_LANES), F32)]
        + [pltpu.VMEM((rows, S5_LANES // S5_CHUNKS), F32)] * (S5_CHUNKS * groups),
        compiler_params=_cparams("parallel", "arbitrary"),
        name="s5",
    )(u, h0, a_lanes, perm, perm.T, bm, cm, d, wg, bg, wo)


def _s5_params(a_re, a_im, log_dt, b_re, b_im, c_re, c_im):
    a_re = a_re.astype(F32)
    a_im = a_im.astype(F32)
    dt = jnp.exp(log_dt.astype(F32))[:, None]
    mag = jnp.exp(a_re * dt)
    ang = a_im * dt
    ab_re = mag * jnp.cos(ang)
    ab_im = mag * jnp.sin(ang)
    den = a_re * a_re + a_im * a_im
    nr = ab_re - 1.0
    ni = ab_im
    f_re = (nr * a_re + ni * a_im) / den
    f_im = (ni * a_re - nr * a_im) / den
    b_re = b_re.astype(F32)
    b_im = b_im.astype(F32)
    bb_re = f_re[..., None] * b_re - f_im[..., None] * b_im
    bb_im = f_re[..., None] * b_im + f_im[..., None] * b_re
    gpc = S5_GROUPS // S5_CHUNKS
    eye = jnp.eye(gpc, dtype=F32)

    def lanes(x):
        return x.reshape(S5_CHUNKS, gpc * S5_STATE)

    a_lanes = jnp.concatenate([lanes(ab_re), lanes(ab_im)], axis=1).reshape(1, S5_LANES)
    a_lanes = jnp.broadcast_to(a_lanes, (BATCH_GROUP, S5_LANES))

    def in_blocks(bb):
        bb = bb.reshape(S5_CHUNKS, gpc, S5_STATE, S5_GROUP)
        return jnp.einsum('cgpj,gh->cgjhp', bb, eye).reshape(S5_CHUNKS, gpc * S5_GROUP, gpc * S5_STATE)

    bm = jnp.concatenate([in_blocks(bb_re), in_blocks(bb_im)], axis=2).astype(BF16)

    def out_blocks(cc):
        cc = cc.astype(F32).reshape(S5_CHUNKS, gpc, S5_GROUP, S5_STATE)
        return jnp.einsum('cgjp,gh->cgphj', cc, eye).reshape(S5_CHUNKS, gpc * S5_STATE, gpc * S5_GROUP)

    cm = jnp.concatenate([out_blocks(c_re), -out_blocks(c_im)], axis=1).astype(BF16)
    return a_lanes, bm, cm


def _s5_state_to_lanes(h_re, h_im):
    bsz = h_re.shape[0]
    re = h_re.astype(F32).reshape(bsz, S5_CHUNKS, -1)
    im = h_im.astype(F32).reshape(bsz, S5_CHUNKS, -1)
    return jnp.concatenate([re, im], axis=2).reshape(bsz, S5_LANES)


def _s5_state_from_lanes(h):
    bsz = h.shape[0]
    h = h.reshape(bsz, S5_CHUNKS, 2, S5_GROUPS // S5_CHUNKS, S5_STATE)
    return (h[:, :, 0].reshape(bsz, S5_GROUPS, S5_STATE), h[:, :, 1].reshape(bsz, S5_GROUPS, S5_STATE))


def _ret_kernel(q_ref, k_ref, v_ref, g_ref, s0_ref, dm_ref, xi_ref, zeta_ref, wo_ref,
                yb_ref, st_ref, s_ref, *, block_decay):
    si = pl.program_id(1)

    @pl.when(si == 0)
    def _():
        s_ref[...] = s0_ref[...]

    blk = dm_ref.shape[1]
    states = [s_ref[h] for h in range(RET_HEADS)]
    gated = []
    for sub in range(q_ref.shape[0] // blk):
        rows = slice(sub * blk, (sub + 1) * blk)
        heads = []
        for h in range(RET_HEADS):
            qh = q_ref[rows, h * RET_DK:(h + 1) * RET_DK]
            kh = k_ref[rows, h * RET_DK:(h + 1) * RET_DK]
            vh = v_ref[rows, h * RET_DV:(h + 1) * RET_DV]
            scores = lax.dot_general(qh, kh, (((1,), (1,)), ((), ())), preferred_element_type=F32) * dm_ref[h]
            o = _bdot(scores.astype(BF16), vh) + _bdot(qh, states[h].astype(BF16)) * xi_ref[h]
            o = o * lax.rsqrt(jnp.mean(o * o, axis=-1, keepdims=True) + EPS)
            gh = g_ref[rows, h * RET_DV:(h + 1) * RET_DV].astype(F32)
            heads.append((o * (gh * _sigmoid(gh))).astype(BF16))
            kz = (kh.astype(F32) * zeta_ref[h]).astype(BF16)
            kv = lax.dot_general(kz, vh, (((0,), (0,)), ((), ())), preferred_element_type=F32)
            states[h] = block_decay[h] * states[h] + kv
        gated.append(jnp.concatenate(heads, axis=1))
    for h in range(RET_HEADS):
        s_ref[h] = states[h]
    yb_ref[...] = _bdot(jnp.concatenate(gated, axis=0), wo_ref[...]).astype(BF16)

    @pl.when(si == pl.num_programs(1) - 1)
    def _():
        st_ref[...] = s_ref[...]


def _ret_tables(seq):
    cl = min(CHUNK, seq)
    blk = min(RET_BLOCK, seq)
    log_g = jnp.log(1.0 - 2.0 ** (-5.0 - jnp.arange(RET_HEADS, dtype=F32)))
    idx = jnp.arange(blk, dtype=F32)
    diff = idx[:, None] - idx[None, :]
    cn = jnp.arange(blk)[:, None] // cl
    cm = jnp.arange(blk)[None, :] // cl
    expo = jnp.where(cm == cn, jnp.abs(diff), diff)
    dm = jnp.where(cm <= cn, jnp.exp(log_g[:, None, None] * expo[None]), 0.0)
    xi = jnp.exp(log_g[:, None] * (idx + 1.0)[None, :])[..., None]
    zeta = jnp.exp(log_g[:, None] * (blk - 1.0 - idx)[None, :])[..., None]
    block_decay = tuple(math.exp(math.log(1.0 - 2.0 ** (-5.0 - h)) * blk) for h in range(RET_HEADS))
    return blk, dm, xi, zeta, block_decay


def _retention(q, k, v, g, s0, w_ret_out_b):
    bsz, seq, _ = q.shape
    blk, dm, xi, zeta, block_decay = _ret_tables(seq)
    step = min(RET_BLOCKS_PER_STEP * blk, seq)
    row = lambda w: pl.BlockSpec((None, step, w), lambda b, s: (b, s, 0))
    const = lambda shape: pl.BlockSpec(shape, lambda b, s: (0,) * len(shape))
    state = pl.BlockSpec((None, RET_HEADS, RET_DK, RET_DV), lambda b, s: (b, 0, 0, 0))
    return pl.pallas_call(
        functools.partial(_ret_kernel, block_decay=block_decay),
        grid=(bsz, seq // step),
        in_specs=[row(RET_QK), row(RET_QK), row(RET_V), row(RET_V), state,
                  const(dm.shape), const(xi.shape), const(zeta.shape), const((RET_V, D_MODEL))],
        out_specs=[row(D_MODEL), state],
        out_shape=[jax.ShapeDtypeStruct((bsz, seq, D_MODEL), BF16),
                   jax.ShapeDtypeStruct((bsz, RET_HEADS, RET_DK, RET_DV), F32)],
        scratch_shapes=[pltpu.VMEM((RET_HEADS, RET_DK, RET_DV), F32)],
        compiler_params=_cparams("parallel", "arbitrary"),
        name="retention",
    )(q, k, v, g, s0, dm, xi, zeta, w_ret_out_b)


def _to_row_tiles(val):
    n, w = val.shape
    return val.reshape(n * (w // LANE), LANE)


def _from_row_tiles(tiles, w):
    return tiles.reshape(tiles.shape[0] // (w // LANE), w)


def _merge_kernel(x_ref, ya_ref, yb_ref, ga_ref, gb_ref, gt_ref, sh_ref, sc_ref, g2_ref, wo_ref,
                  wr_ref, br_ref, x1_ref, n2_ref, re_ref, wc_ref, cnt_ref):
    tl = x_ref.shape[0]
    part = tl // MERGE_PARTS if tl % (MERGE_PARTS * 16) == 0 else tl
    nt_dot = lambda a, b: lax.dot_general(a, b, (((1,), (1,)), ((), ())), preferred_element_type=F32)
    logits = []
    for p in range(tl // part):
        rows = slice(p * part, (p + 1) * part)
        vec = lambda ref: ref[...] if ref.shape[0] == 1 else ref[rows, :]
        merged = _sigmoid(ga_ref[rows, :]) * ya_ref[rows, :] + _sigmoid(gb_ref[rows, :]) * yb_ref[rows, :]
        x1 = x_ref[rows, :] + vec(gt_ref) * _bdot(merged, wo_ref[...])
        x1_ref[rows, :] = x1.astype(BF16)
        n2 = x1 * lax.rsqrt(jnp.mean(x1 * x1, axis=-1, keepdims=True) + EPS) * g2_ref[...]
        n2 = n2 * (1.0 + vec(sc_ref)) + vec(sh_ref)
        for j in range(ROW_TILES):
            n2_ref[pl.ds(p * part * ROW_TILES + j, part, stride=ROW_TILES), :] = n2[:, j * LANE:(j + 1) * LANE]
        n2_hi = n2.astype(BF16)
        n2_lo = (n2 - n2_hi.astype(F32)).astype(BF16)
        logits.append(nt_dot(wr_ref[0], n2_hi) + nt_dot(wr_ref[0], n2_lo) + nt_dot(wr_ref[1], n2_hi))
    lt = jnp.concatenate(logits, axis=1) + br_ref[...]
    iota = lax.broadcasted_iota(jnp.int32, (8, tl), 0)
    gl = lt[0:8]
    gmax = jnp.max(gl, axis=0, keepdims=True)
    gi = jnp.min(jnp.where(gl == gmax, iota, 8), axis=0, keepdims=True)
    gw = 1.0 / jnp.sum(jnp.exp(gl - gmax), axis=0, keepdims=True)
    el = jnp.zeros((8, tl), F32)
    for g in range(MOE_GROUPS):
        el = jnp.where(gi == g, lt[8 * (g + 1):8 * (g + 2)], el)
    m1 = jnp.max(el, axis=0, keepdims=True)
    i1 = jnp.min(jnp.where(el == m1, iota, 8), axis=0, keepdims=True)
    el2 = jnp.where(iota == i1, -jnp.inf, el)
    m2 = jnp.max(el2, axis=0, keepdims=True)
    i2 = jnp.min(jnp.where(el2 == m2, iota, 8), axis=0, keepdims=True)
    e21 = jnp.exp(m2 - m1)
    w1 = gw / (1.0 + e21)
    w2 = w1 * e21
    lo = jnp.minimum(i1, i2)
    hi = jnp.maximum(i1, i2)
    pair = gi * PAIRS_PER_GROUP + lo * (MOE_EXPERTS - 1) - ((lo * (lo - 1)) >> 1) + (hi - lo - 1)
    re_ref[...] = jnp.where(iota == 0, pair, 0)
    rw = jnp.where(iota == 0, jnp.where(i1 < i2, w1, w2), jnp.where(iota == 1, jnp.where(i1 < i2, w2, w1), 0.0))
    eye = (lax.broadcasted_iota(jnp.int32, (8, LANE), 0)
           == lax.broadcasted_iota(jnp.int32, (8, LANE), 1)).astype(F32)
    wc_ref[...] = lax.dot_general(rw, eye, (((0,), (0,)), ((), ())),
                                  precision=lax.Precision.HIGHEST, preferred_element_type=F32)
    ids = lax.broadcasted_iota(jnp.int32, (N_PAIRS, tl), 0)
    hits = jnp.where(ids == pair, 1.0, 0.0)
    cnt_ref[...] = jnp.broadcast_to(jnp.sum(hits, axis=1, keepdims=True), (N_PAIRS, LANE)).astype(jnp.int32)


def _merge(x, ya, yb, ga, gb, gate1, shift2, scale2, g2, w_out_b, wr, br):
    bsz, seq, _ = x.shape
    tl = min(TOKEN_TILE, seq)
    row = pl.BlockSpec((None, tl, D_MODEL), lambda b, t: (b, t, 0))
    vec = _mod_spec(gate1, tl)
    const = lambda shape: pl.BlockSpec(shape, lambda b, t: (0,) * len(shape))
    route = pl.BlockSpec((None, 8, tl), lambda b, t: (b, 0, t))
    nt = seq // tl
    tiles = pl.BlockSpec((tl * ROW_TILES, LANE), lambda b, t: (b * nt + t, 0))
    return pl.pallas_call(
        _merge_kernel,
        grid=(bsz, nt),
        in_specs=[row, row, row, row, row, vec, vec, vec, const((1, D_MODEL)),
                  const((D_MODEL, D_MODEL)), const((2, ROUTE_ROWS, D_MODEL)), const((ROUTE_ROWS, 1))],
        out_specs=[row, tiles, route,
                   pl.BlockSpec((tl, LANE), lambda b, t: (b * nt + t, 0)),
                   pl.BlockSpec((None, N_PAIRS, LANE), lambda b, t: (b * nt + t, 0, 0))],
        out_shape=[jax.ShapeDtypeStruct((bsz, seq, D_MODEL), BF16),
                   jax.ShapeDtypeStruct((bsz * seq * ROW_TILES, LANE), F32),
                   jax.ShapeDtypeStruct((bsz, 8, seq), jnp.int32),
                   jax.ShapeDtypeStruct((bsz * seq, LANE), F32),
                   jax.ShapeDtypeStruct((bsz * nt, N_PAIRS, LANE), jnp.int32)],
        compiler_params=_cparams("parallel", "parallel"),
        name="merge",
    )(x, ya, yb, ga, gb, gate1, shift2, scale2, g2, w_out_b, wr, br)


def _router_weights(w_rg, b_rg, w_re, b_re):
    wr = jnp.zeros((ROUTE_ROWS, D_MODEL), F32)
    wr = wr.at[0:MOE_GROUPS].set(w_rg.astype(F32).T)
    wr = wr.at[8:].set(jnp.transpose(w_re.astype(F32), (0, 2, 1)).reshape(N_EXPERTS, D_MODEL))
    br = jnp.full((ROUTE_ROWS,), -1e30, F32)
    br = br.at[0:MOE_GROUPS].set(b_rg.astype(F32))
    br = br.at[8:].set(b_re.astype(F32).reshape(N_EXPERTS))
    wr_hi = wr.astype(BF16)
    wr_lo = (wr - wr_hi.astype(F32)).astype(BF16)
    return jnp.stack([wr_hi, wr_lo]), br.reshape(ROUTE_ROWS, 1)


def _row_gather(idx_ref, src_hbm, buf, sem, slot, n, span):
    def body(pair, carry):
        for queue in range(DMA_QUEUES):
            r = pair * DMA_QUEUES + queue
            src = pl.multiple_of(idx_ref[0, r] * span, span)
            dst = pl.multiple_of(r * span, span)
            pltpu.make_async_copy(src_hbm.at[pl.ds(src, span), :],
                                  buf.at[slot, pl.ds(dst, span), :], sem.at[slot]).start(priority=queue)
        return carry
    lax.fori_loop(0, n // DMA_QUEUES, body, 0, unroll=DMA_UNROLL // DMA_QUEUES)


def _row_gather_wait(src_hbm, buf, sem, slot, n, span):
    pltpu.make_async_copy(src_hbm.at[pl.ds(0, n * span), :], buf.at[slot], sem.at[slot]).wait()


def _plan_kernel(e_ref, base_ref, c0_ref, tri_ref, pos_ref, carry_ref):
    @pl.when(pl.program_id(0) == 0)
    def _():
        carry_ref[...] = c0_ref[...]

    tb = e_ref.shape[1]
    hit = lax.broadcasted_iota(jnp.int32, (N_PAIRS, tb), 0) == e_ref[0:1, :]
    incl = _bdot(jnp.where(hit, 1.0, 0.0).astype(BF16), tri_ref[...])
    row = jnp.where(hit, incl - 1.0 + carry_ref[...] + base_ref[...], 0.0)
    pos_ref[...] = jnp.sum(row, axis=0, keepdims=True).astype(jnp.int32)
    carry_ref[...] = carry_ref[...] + incl[:, tb - 1:tb]


def _plan(route_e, pad_start, placed):
    bsz, _, seq = route_e.shape
    tb = min(ROUTE_TILE, seq)
    nb = seq // tb
    r = jnp.arange(tb)
    tri = (r[:, None] <= r[None, :]).astype(BF16)
    col = pl.BlockSpec((N_PAIRS, 1), lambda i: (0, 0))
    return pl.pallas_call(
        _plan_kernel,
        grid=(bsz * nb,),
        in_specs=[pl.BlockSpec((None, 8, tb), lambda i: (i // nb, 0, i % nb)), col, col,
                  pl.BlockSpec((tb, tb), lambda i: (0, 0))],
        out_specs=pl.BlockSpec((None, 1, tb), lambda i: (i, 0, 0)),
        out_shape=jax.ShapeDtypeStruct((bsz * nb, 1, tb), jnp.int32),
        scratch_shapes=[pltpu.VMEM((N_PAIRS, 1), F32)],
        compiler_params=_cparams("arbitrary"),
        name="plan",
    )(route_e, pad_start.astype(F32).reshape(N_PAIRS, 1), placed.astype(F32).reshape(N_PAIRS, 1), tri)


ZERO_CHUNKS = tuple(2 ** b for b in range(8, -1, -1))


def _dispatch_kernel(lo_ref, hi_ref, nu_ref, posa_ref, xa_ref, posb_ref, xb_ref, xs_hbm, zero_ref, sem,
                     *, tb, nb_rows, tm, n_tiles):
    last = pl.num_programs(0) - 1

    def scatter(pos_ref, x_ref, n):
        def body(pair, carry):
            for queue in range(DMA_QUEUES):
                r = pair * DMA_QUEUES + queue
                src = pl.multiple_of(r * ROW_TILES, ROW_TILES)
                dst = pl.multiple_of(pos_ref[0, r] * ROW_TILES, ROW_TILES)
                pltpu.make_async_copy(x_ref.at[pl.ds(src, ROW_TILES), :],
                                      xs_hbm.at[pl.ds(dst, ROW_TILES), :], sem.at[0]).start(priority=queue)
            return carry
        lax.fori_loop(0, n // DMA_QUEUES, body, 0, unroll=DMA_UNROLL // DMA_QUEUES)

    def scatter_wait(x_ref, n):
        pltpu.make_async_copy(x_ref, xs_hbm.at[pl.ds(0, n * ROW_TILES), :], sem.at[0]).wait()

    def zero_copy(row, size):
        dst = pl.multiple_of(row * ROW_TILES, ROW_TILES)
        return pltpu.make_async_copy(zero_ref.at[pl.ds(0, size * ROW_TILES), :],
                                     xs_hbm.at[pl.ds(dst, size * ROW_TILES), :], sem.at[1])

    @pl.when(pl.program_id(0) < last)
    def _():
        scatter(posa_ref, xa_ref, tb)
        scatter_wait(xa_ref, tb)

    @pl.when(pl.program_id(0) == last)
    def _():
        scatter(posb_ref, xb_ref, nb_rows)
        zero_ref[...] = jnp.zeros_like(zero_ref)

        def per_pair(e, carry):
            lo = lo_ref[e]
            n = hi_ref[e] - lo
            for wait in (False, True):
                row = lo
                for size in ZERO_CHUNKS:
                    @pl.when((n & size) != 0)
                    def _(row=row, size=size, wait=wait):
                        zero_copy(row, size).wait() if wait else zero_copy(row, size).start()
                    row = row + (n & size)
            return carry
        lax.fori_loop(0, N_PAIRS, per_pair, 0)

        chunk = min(ZERO_CHUNKS[0], tm)
        for wait in (False, True):
            def per_tile(t, carry, wait=wait):
                for c in range(tm // chunk):
                    cp = zero_copy(t * tm + c * chunk, chunk)
                    cp.wait() if wait else cp.start()
                return carry
            lax.fori_loop(nu_ref[0], n_tiles, per_tile, 0)
        scatter_wait(xb_ref, nb_rows)


def _dispatch(pos_a, rows_a, pos_b, rows_b, pad_lo, pad_hi, n_used, n_rows, tm):
    tb = min(DISPATCH_TILE, pos_a.size)
    na = pos_a.size // tb
    pos_a = pos_a.reshape(na, 1, tb)
    nb_rows = rows_b.shape[0] // ROW_TILES
    pos_b = pos_b.reshape(1, 1, nb_rows)
    block_a = lambda i, lo, hi, nu: (jnp.minimum(i, na - 1), 0, 0)
    grid_spec = pltpu.PrefetchScalarGridSpec(
        num_scalar_prefetch=3,
        grid=(na + 1,),
        in_specs=[pl.BlockSpec((None, 1, tb), block_a, memory_space=pltpu.SMEM),
                  pl.BlockSpec((tb * ROW_TILES, LANE), lambda i, lo, hi, nu: (jnp.minimum(i, na - 1), 0)),
                  pl.BlockSpec((None, 1, nb_rows), lambda i, lo, hi, nu: (0, 0, 0), memory_space=pltpu.SMEM),
                  pl.BlockSpec((nb_rows * ROW_TILES, LANE), lambda i, lo, hi, nu: (0, 0))],
        out_specs=pl.BlockSpec(memory_space=pl.ANY),
        scratch_shapes=[pltpu.VMEM((ZERO_CHUNKS[0] * ROW_TILES, LANE), F32), pltpu.SemaphoreType.DMA((2,))],
    )
    return pl.pallas_call(
        functools.partial(_dispatch_kernel, tb=tb, nb_rows=nb_rows, tm=tm, n_tiles=n_rows // tm),
        grid_spec=grid_spec,
        out_shape=jax.ShapeDtypeStruct((n_rows * ROW_TILES, LANE), F32),
        compiler_params=_cparams("arbitrary"),
        name="dispatch",
    )(pad_lo, pad_hi, n_used, pos_a, rows_a, pos_b, rows_b)


def _expert_kernel(ta_ref, tb_ref, nu_ref, x_ref, w1a_ref, w3a_ref, w2a_ref, w1b_ref, w3b_ref, w2b_ref,
                   y_ref, *, tm):
    @pl.when(pl.program_id(0) < nu_ref[0])
    def _():
        xb = _from_row_tiles(x_ref[...], D_MODEL).astype(BF16)
        ys = []
        for w1_ref, w3_ref, w2_ref in ((w1a_ref, w3a_ref, w2a_ref), (w1b_ref, w3b_ref, w2b_ref)):
            a = _bdot(xb, w1_ref[...])
            h = a * _sigmoid(a) * _bdot(xb, w3_ref[...])
            ys.append(_bdot(h.astype(BF16), w2_ref[...]))
        y_ref[...] = _to_row_tiles(jnp.concatenate(ys, axis=1))

    @pl.when(pl.program_id(0) >= nu_ref[0])
    def _():
        y_ref[...] = jnp.zeros_like(y_ref)


def _experts(xs, tile_a, tile_b, n_used, w1b, w3b, w2b, tm):
    n_tiles = xs.shape[0] // (tm * ROW_TILES)
    tile = lambda i, ta, tb, nu: (jnp.maximum(jnp.minimum(i, nu[0] - 1), 0), 0)
    first = lambda i, ta, tb, nu: (ta[i], 0, 0)
    second = lambda i, ta, tb, nu: (tb[i], 0, 0)
    up = lambda which: pl.BlockSpec((None, D_MODEL, MOE_FF), which)
    down = lambda which: pl.BlockSpec((None, MOE_FF, D_MODEL), which)
    grid_spec = pltpu.PrefetchScalarGridSpec(
        num_scalar_prefetch=3,
        grid=(n_tiles,),
        in_specs=[pl.BlockSpec((tm * ROW_TILES, LANE), tile),
                  up(first), up(first), down(first), up(second), up(second), down(second)],
        out_specs=pl.BlockSpec((tm * 2 * ROW_TILES, LANE), lambda i, ta, tb, nu: (i, 0)),
    )
    return pl.pallas_call(
        functools.partial(_expert_kernel, tm=tm),
        grid_spec=grid_spec,
        out_shape=jax.ShapeDtypeStruct((2 * xs.shape[0], LANE), F32),
        compiler_params=_cparams("arbitrary"),
        name="experts",
    )(tile_a, tile_b, n_used, xs, w1b, w3b, w2b, w1b, w3b, w2b)


def _expert_layout(counts, tm, n_tok):
    n_tiles = n_tok // tm + MOE_GROUPS * PAIRS_PER_GROUP
    padded = ((counts + tm - 1) // tm) * tm
    pad_end = jnp.cumsum(padded)
    pad_start = pad_end - padded
    n_used = pad_end[-1] // tm
    tile_start = jnp.minimum(jnp.arange(n_tiles, dtype=jnp.int32), n_used - 1) * tm
    tile_pair = jnp.sum((pad_end[None, :] <= tile_start[:, None]).astype(jnp.int32), axis=1)
    tile_pair = jnp.minimum(tile_pair, MOE_GROUPS * PAIRS_PER_GROUP - 1)
    members = [(lo, hi) for lo in range(MOE_EXPERTS) for hi in range(lo + 1, MOE_EXPERTS)]
    lo_of = jnp.asarray([m[0] for m in members], jnp.int32)
    hi_of = jnp.asarray([m[1] for m in members], jnp.int32)
    group = tile_pair // PAIRS_PER_GROUP
    tile_a = group * MOE_EXPERTS + lo_of[tile_pair % PAIRS_PER_GROUP]
    tile_b = group * MOE_EXPERTS + hi_of[tile_pair % PAIRS_PER_GROUP]
    return (pad_start, pad_start + counts, pad_end, tile_a.astype(jnp.int32), tile_b.astype(jnp.int32),
            n_used.astype(jnp.int32).reshape(1), n_tiles * tm)


def _combine_kernel(cur_ref, nxt_ref, x1_ref, wc_ref, gt_ref, fn_ref, y_hbm, o_ref, ybuf, sem, *, tl):
    i = pl.program_id(0)
    slot = i % 2

    span = 2 * ROW_TILES

    @pl.when(i == 0)
    def _():
        _row_gather(cur_ref, y_hbm, ybuf, sem, 0, tl, span)

    @pl.when(i + 1 < pl.num_programs(0))
    def _():
        _row_gather(nxt_ref, y_hbm, ybuf, sem, 1 - slot, tl, span)

    _row_gather_wait(y_hbm, ybuf, sem, slot, tl, span)
    both = _from_row_tiles(ybuf[slot], 2 * D_MODEL)
    moe = wc_ref[:, 0:1] * both[:, :D_MODEL] + wc_ref[:, 1:2] * both[:, D_MODEL:]
    x2 = x1_ref[...].astype(F32) + gt_ref[...] * moe
    o_ref[...] = x2 * lax.rsqrt(jnp.mean(x2 * x2, axis=-1, keepdims=True) + EPS) * fn_ref[...]


def _combine(x1_flat, pos, y_rows, wcol, gate2, final_norm, seq):
    n_tok = x1_flat.shape[0]
    n_tiles, _, tl = pos.shape
    last = n_tiles - 1
    if gate2.shape[1] == 1:
        gate_spec = pl.BlockSpec((None, 1, D_MODEL), lambda i: ((i * tl) // seq, 0, 0))
    else:
        gate2 = gate2.reshape(n_tok, D_MODEL)
        gate_spec = pl.BlockSpec((tl, D_MODEL), lambda i: (i, 0))
    return pl.pallas_call(
        functools.partial(_combine_kernel, tl=tl),
        grid=(n_tiles,),
        in_specs=[pl.BlockSpec((None, 1, tl), lambda i: (i, 0, 0), memory_space=pltpu.SMEM),
                  pl.BlockSpec((None, 1, tl), lambda i: (jnp.minimum(i + 1, last), 0, 0),
                               memory_space=pltpu.SMEM),
                  pl.BlockSpec((tl, D_MODEL), lambda i: (i, 0)),
                  pl.BlockSpec((tl, LANE), lambda i: (i, 0)),
                  gate_spec,
                  pl.BlockSpec((1, D_MODEL), lambda i: (0, 0)),
                  pl.BlockSpec(memory_space=pl.ANY)],
        out_specs=pl.BlockSpec((tl, D_MODEL), lambda i: (i, 0)),
        out_shape=jax.ShapeDtypeStruct((n_tok, D_MODEL), F32),
        scratch_shapes=[pltpu.VMEM((2, 2 * tl * ROW_TILES, LANE), F32), pltpu.SemaphoreType.DMA((2,))],
        compiler_params=_cparams("arbitrary"),
        name="combine",
    )(pos, pos, x1_flat, wcol, gate2, final_norm, y_rows)


def _rope_tables(seq, offset):
    half = RET_DK // 2
    theta = 1.0 / (ROPE_BASE ** jnp.linspace(0.0, 1.0, half, dtype=F32))
    pos = offset + jnp.arange(seq)
    ang = pos.astype(F32)[:, None] * theta[None, :]
    cos = jnp.cos(ang)
    sin = jnp.sin(ang)
    return jnp.concatenate([cos, cos], axis=1), jnp.concatenate([-sin, sin], axis=1)


def _mixers(x, mod, h0, s0, offset, p):
    bsz, seq, _ = x.shape
    mods = [m.reshape(bsz, 1, D_MODEL) for m in jnp.split(mod, 6, axis=-1)]
    cos2, sin2 = _rope_tables(seq, offset)
    flat = seq < TOKEN_TILE and bsz * seq <= TOKEN_TILE
    sets, rows = (1, bsz * seq) if flat else (bsz, seq)
    tok = lambda a: a.reshape(sets, rows, a.shape[-1])
    if flat:
        mods = [jnp.broadcast_to(m, (bsz, seq, D_MODEL)).reshape(1, rows, D_MODEL) for m in mods]
        cos2, sin2 = jnp.tile(cos2, (bsz, 1)), jnp.tile(sin2, (bsz, 1))
    sh1, sc1, gt1, sh2, sc2, gt2 = mods
    seqs = lambda a: a.reshape(bsz, seq, a.shape[-1])
    u, q, k, v, g, ga, gb = _inproj(tok(x), sh1, sc1, p['norm1'], p['w_in'], cos2, sin2)
    ya, h_t = _s5(seqs(u), h0, p['a_lanes'], p['bm'], p['cm'], p['d'], p['w_glu'], p['b_glu'], p['w_s5_out'])
    yb, s_t = _retention(seqs(q), seqs(k), seqs(v), seqs(g), s0, p['w_ret_out'])
    x1, n2, route_e, wcol, cnt = _merge(tok(x), tok(ya), tok(yb), ga, gb, gt1, sh2, sc2, p['norm2'],
                                        p['w_out'], p['wr'], p['br'])
    return dict(x1=x1.reshape(bsz * seq, D_MODEL), n2=n2, route=route_e, wcol=wcol,
                counts=jnp.sum(cnt[:, :, 0], axis=0), gate2=gt2, seq=rows, h=h_t, s=s_t)


def _moe(a, b, p, final_norm):
    n_tok = a['x1'].shape[0] + b['x1'].shape[0]
    pad_start, pad_lo, pad_hi, tile_a, tile_b, n_used, n_rows = _expert_layout(
        a['counts'] + b['counts'], EXPERT_TILE, n_tok)
    pos_a = _plan(a['route'], pad_start, jnp.zeros_like(a['counts']))
    pos_b = _plan(b['route'], pad_start, a['counts'])
    xs = _dispatch(pos_a, a['n2'], pos_b, b['n2'], pad_lo, pad_hi, n_used, n_rows, EXPERT_TILE)
    y_rows = _experts(xs, tile_a, tile_b, n_used, p['w1'], p['w3'], p['w2'], EXPERT_TILE)
    return [_combine(m['x1'], pos, y_rows, m['wcol'], m['gate2'], final_norm, m['seq'])
            for m, pos in ((a, pos_a), (b, pos_b))]


def kernel(x_prompt, x_sample, state_s5_re, state_s5_im, state_ret, c_prompt, c_sample, w_ada, b_ada, norm1, norm2, w_in, s5_a_re, s5_a_im, s5_log_dt, s5_b_re, s5_b_im, s5_c_re, s5_c_im, s5_d, s5_w_glu, s5_b_glu, w_s5_out, w_ret_out, w_out, w_rg, b_rg, w_re, b_re, w1, w3, w2, final_norm):
    depth = w_ada.shape[0]
    assert depth == 1
    bp = x_prompt.shape[0]
    bs, seq_s, _ = x_sample.shape
    l = 0
    a_lanes, bm, cm = _s5_params(s5_a_re[l], s5_a_im[l], s5_log_dt[l], s5_b_re[l], s5_b_im[l],
                                 s5_c_re[l], s5_c_im[l])
    wr, br = _router_weights(w_rg[l], b_rg[l], w_re[l], b_re[l])
    p = dict(
        norm1=norm1[l].astype(F32).reshape(1, D_MODEL), norm2=norm2[l].astype(F32).reshape(1, D_MODEL),
        w_in=w_in[l].astype(BF16), a_lanes=a_lanes, bm=bm, cm=cm,
        d=s5_d[l].astype(F32).reshape(1, S5_WIDTH), w_glu=s5_w_glu[l].astype(BF16),
        b_glu=s5_b_glu[l].astype(F32).reshape(1, S5_WIDTH), w_s5_out=w_s5_out[l].astype(BF16),
        w_ret_out=w_ret_out[l].astype(BF16), w_out=w_out[l].astype(BF16), wr=wr, br=br,
        w1=w1[l].astype(BF16).reshape(N_EXPERTS, D_MODEL, MOE_FF),
        w3=w3[l].astype(BF16).reshape(N_EXPERTS, D_MODEL, MOE_FF),
        w2=w2[l].astype(BF16).reshape(N_EXPERTS, MOE_FF, D_MODEL))
    fn = final_norm.astype(F32).reshape(1, D_MODEL)
    mod = _mod(jnp.concatenate([c_prompt, c_sample], axis=0).astype(F32), w_ada[l], b_ada[l])

    h0_p = jnp.zeros((bp, S5_LANES), F32)
    s0_p = jnp.zeros((bp, RET_HEADS, RET_DK, RET_DV), F32)
    prompt = _mixers(x_prompt, mod[:bp], h0_p, s0_p, 0, p)
    h0_s = _s5_state_to_lanes(state_s5_re[l], state_s5_im[l])
    sample = _mixers(x_sample, mod[bp:], h0_s, state_ret[l].astype(F32), PAST_LEN, p)
    y_p, y_s = _moe(prompt, sample, p, fn)
    p_re, p_im = _s5_state_from_lanes(prompt['h'])
    s_re, s_im = _s5_state_from_lanes(sample['h'])
    return (y_p.reshape(x_prompt.shape), y_s.reshape(x_sample.shape), p_re[None], p_im[None],
            prompt['s'][None], s_re[None], s_im[None], sample['s'][None])
```

```python
import functools
import math

import jax
import jax.numpy as jnp
from jax import lax
from jax.experimental import pallas as pl
from jax.experimental.pallas import tpu as pltpu

F32 = jnp.float32
BF16 = jnp.bfloat16

D_MODEL = 1024
PAST_LEN = 2048
CHUNK = 64
S5_WIDTH = 512
S5_GROUP = 16
S5_GROUPS = 32
S5_STATE = 64
S5_LANES = 2 * S5_GROUPS * S5_STATE
S5_CHUNKS = 4
RET_HEADS = 4
RET_DK = 128
RET_DV = 256
RET_QK = RET_HEADS * RET_DK
RET_V = RET_HEADS * RET_DV
ROPE_BASE = 10000.0
MOE_GROUPS = 4
MOE_EXPERTS = 8
N_EXPERTS = MOE_GROUPS * MOE_EXPERTS
PAIRS_PER_GROUP = MOE_EXPERTS * (MOE_EXPERTS - 1) // 2
N_PAIRS = 128
MOE_FF = 256
EPS = 1e-6
IN_WIDTH = S5_WIDTH + 2 * RET_QK + 2 * RET_V + 2 * D_MODEL
ROUTE_ROWS = 8 * (1 + MOE_GROUPS)

BATCH_GROUP = 8
S5_GROUPS_PER_STEP = 2
TOKEN_TILE = 1024
MERGE_PARTS = 2
INPROJ_TILE = 512
S5_TIME_TILE = 64
RET_BLOCK = 256
RET_BLOCKS_PER_STEP = 4
EXPERT_TILE = 256
ROUTE_TILE = 512
DISPATCH_TILE = 2048
VMEM_LIMIT = 56 * 1024 * 1024
LANE = 128
SUBLANE = 8
ROW_TILES = D_MODEL // LANE
DMA_UNROLL = 8
DMA_QUEUES = 2


def _cparams(*sem):
    return pltpu.CompilerParams(dimension_semantics=sem, vmem_limit_bytes=VMEM_LIMIT)


def _bdot(a, b):
    return jnp.dot(a, b, preferred_element_type=F32)


def _sigmoid(x):
    return 0.5 * jnp.tanh(0.5 * x) + 0.5


def _mod_kernel(c_ref, w_ref, b_ref, o_ref):
    c = c_ref[...]
    a = (c * _sigmoid(c)).astype(BF16)
    o_ref[...] = _bdot(a, w_ref[...].astype(BF16)) + b_ref[...]


def _mod(c, w_ada, b_ada):
    n = c.shape[0]
    return pl.pallas_call(
        _mod_kernel,
        grid=(6,),
        in_specs=[pl.BlockSpec((n, D_MODEL), lambda j: (0, 0)),
                  pl.BlockSpec((D_MODEL, D_MODEL), lambda j: (0, j)),
                  pl.BlockSpec((1, D_MODEL), lambda j: (0, j))],
        out_specs=pl.BlockSpec((n, D_MODEL), lambda j: (0, j)),
        out_shape=jax.ShapeDtypeStruct((n, 6 * D_MODEL), F32),
        compiler_params=_cparams("parallel"),
        name="mod",
    )(c, w_ada, b_ada.reshape(1, -1))


def _rope(x, cos2, sin2):
    return x * cos2 + pltpu.roll(x, RET_DK // 2, 1) * sin2


def _inproj_kernel(x_ref, sh_ref, sc_ref, g_ref, w_ref, cos_ref, sin_ref,
                   u_ref, q_ref, k_ref, v_ref, gs_ref, ga_ref, gb_ref):
    x = x_ref[...]
    n = x * lax.rsqrt(jnp.mean(x * x, axis=-1, keepdims=True) + EPS) * g_ref[...]
    nb = (n * (1.0 + sc_ref[...]) + sh_ref[...]).astype(BF16)
    cos2 = cos_ref[...]
    sin2 = sin_ref[...]
    o = 0
    u_ref[...] = _bdot(nb, w_ref[:, o:o + S5_WIDTH]).astype(BF16)
    o += S5_WIDTH
    q = _bdot(nb, w_ref[:, o:o + RET_QK])
    for h in range(RET_HEADS):
        head = slice(h * RET_DK, (h + 1) * RET_DK)
        q_ref[:, head] = _rope(q[:, head], cos2, sin2).astype(BF16)
    o += RET_QK
    k = _bdot(nb, w_ref[:, o:o + RET_QK])
    for h in range(RET_HEADS):
        head = slice(h * RET_DK, (h + 1) * RET_DK)
        k_ref[:, head] = (_rope(k[:, head], cos2, sin2) * (RET_DK ** -0.5)).astype(BF16)
    o += RET_QK
    for ref in (v_ref, gs_ref, ga_ref, gb_ref):
        ref[...] = _bdot(nb, w_ref[:, o:o + D_MODEL]).astype(BF16)
        o += D_MODEL


def _mod_spec(vec, tl):
    if vec.shape[1] == 1:
        return pl.BlockSpec((None, 1, D_MODEL), lambda b, t: (b, 0, 0))
    return pl.BlockSpec((None, tl, D_MODEL), lambda b, t: (b, t, 0))


def _inproj(x, shift, scale, g1, w_in_b, cos2, sin2):
    bsz, seq, _ = x.shape
    tl = min(INPROJ_TILE, seq)
    row = lambda w: pl.BlockSpec((None, tl, w), lambda b, t: (b, t, 0))
    vec = _mod_spec(shift, tl)
    shapes = [S5_WIDTH, RET_QK, RET_QK, RET_V, RET_V, D_MODEL, D_MODEL]
    return pl.pallas_call(
        _inproj_kernel,
        grid=(bsz, seq // tl),
        in_specs=[row(D_MODEL), vec, vec,
                  pl.BlockSpec((1, D_MODEL), lambda b, t: (0, 0)),
                  pl.BlockSpec((D_MODEL, IN_WIDTH), lambda b, t: (0, 0), pipeline_mode=pl.Buffered(1)),
                  pl.BlockSpec((tl, RET_DK), lambda b, t: (t, 0)),
                  pl.BlockSpec((tl, RET_DK), lambda b, t: (t, 0))],
        out_specs=[row(w) for w in shapes],
        out_shape=[jax.ShapeDtypeStruct((bsz, seq, w), BF16) for w in shapes],
        compiler_params=_cparams("parallel", "parallel"),
        name="inproj",
    )(x, shift, scale, g1, w_in_b, cos2, sin2)


def _gelu_tanh(y):
    return 0.5 * y * (1.0 + jnp.tanh(math.sqrt(2.0 / math.pi) * (y + 0.044715 * (y * y * y))))


def _s5_kernel(u_ref, h0_ref, a_ref, pm_ref, pt_ref, bm_ref, cm_ref, d_ref, wg_ref, bg_ref, wo_ref,
               ya_ref, ht_ref, hs_ref, *bu_refs, tt):
    ti = pl.program_id(1)
    rows = BATCH_GROUP * tt
    half = S5_LANES // (2 * S5_CHUNKS)

    @pl.when(ti == 0)
    def _():
        hs_ref[...] = h0_ref[...]

    kc = S5_WIDTH // S5_CHUNKS
    n_grp = u_ref.shape[0] // BATCH_GROUP
    streams = lambda g: slice(g * BATCH_GROUP, (g + 1) * BATCH_GROUP)
    u2, ys = {}, {g: [] for g in range(n_grp)}

    def permute(g):
        u2[g] = _bdot(pm_ref[...], u_ref[streams(g)].reshape(rows, S5_WIDTH)).astype(BF16)

    def input_map(g, c):
        bu_refs[g * S5_CHUNKS + c][...] = _bdot(u2[g][:, c * kc:(c + 1) * kc], bm_ref[c])

    def recurrence(g, c):
        bu_ref = bu_refs[g * S5_CHUNKS + c]
        lre = slice(c * 2 * half, c * 2 * half + half)
        lim = slice(c * 2 * half + half, (c + 1) * 2 * half)
        are = a_ref[:, lre]
        aim = a_ref[:, lim]
        hre = hs_ref[streams(g), lre]
        him = hs_ref[streams(g), lim]
        for t in range(tt):
            rsel = slice(t * BATCH_GROUP, (t + 1) * BATCH_GROUP)
            hre, him = (are * hre - aim * him + bu_ref[rsel, :half],
                        are * him + aim * hre + bu_ref[rsel, half:])
            bu_ref[rsel, :half] = hre
            bu_ref[rsel, half:] = him
        hs_ref[streams(g), lre] = hre
        hs_ref[streams(g), lim] = him

    def output_map(g, c):
        ys[g].append(_bdot(bu_refs[g * S5_CHUNKS + c][...].astype(BF16), cm_ref[c]))

    def tail(g):
        y = jnp.concatenate(ys[g], axis=1) + d_ref[...] * u2[g].astype(F32)
        z = _gelu_tanh(y)
        gl = _bdot(z.astype(BF16), wg_ref[...]) + bg_ref[...]
        o = (z * _sigmoid(gl)).astype(BF16)
        ob = _bdot(pt_ref[...], o).astype(BF16)
        ya_ref[streams(g)] = _bdot(ob, wo_ref[...]).reshape(BATCH_GROUP, tt, D_MODEL).astype(BF16)

    permute(0)
    for c in range(S5_CHUNKS):
        input_map(0, c)
    for g in range(n_grp + 1):
        if g + 1 < n_grp:
            permute(g + 1)
        for c in range(S5_CHUNKS):
            if g >= 1:
                output_map(g - 1, c)
            if g + 1 < n_grp:
                input_map(g + 1, c)
            if g < n_grp:
                recurrence(g, c)
        if g >= 1:
            tail(g - 1)

    @pl.when(ti == pl.num_programs(1) - 1)
    def _():
        ht_ref[...] = hs_ref[...]


def _s5(u, h0, a_lanes, bm, cm, d, wg, bg, wo):
    bsz, seq, _ = u.shape
    tt = min(S5_TIME_TILE, seq)
    rows = BATCH_GROUP * tt
    groups = S5_GROUPS_PER_STEP if bsz % (BATCH_GROUP * S5_GROUPS_PER_STEP) == 0 else 1
    nb = BATCH_GROUP * groups
    const = lambda shape: pl.BlockSpec(shape, lambda b, t: (0,) * len(shape))
    r = jnp.arange(rows)
    perm = ((r[:, None] % BATCH_GROUP) * tt + r[:, None] // BATCH_GROUP == r[None, :]).astype(BF16)
    return pl.pallas_call(
        functools.partial(_s5_kernel, tt=tt),
        grid=(bsz // nb, seq // tt),
        in_specs=[pl.BlockSpec((nb, tt, S5_WIDTH), lambda b, t: (b, t, 0)),
                  pl.BlockSpec((nb, S5_LANES), lambda b, t: (b, 0)),
                  const((BATCH_GROUP, S5_LANES)), const((rows, rows)), const((rows, rows)),
                  const(bm.shape), const(cm.shape), const((1, S5_WIDTH)),
                  const((S5_WIDTH, S5_WIDTH)), const((1, S5_WIDTH)), const((S5_WIDTH, D_MODEL))],
        out_specs=[pl.BlockSpec((nb, tt, D_MODEL), lambda b, t: (b, t, 0)),
                   pl.BlockSpec((nb, S5_LANES), lambda b, t: (b, 0))],
        out_shape=[jax.ShapeDtypeStruct((bsz, seq, D_MODEL), BF16),
                   jax.ShapeDtypeStruct((bsz, S5_LANES), F32)],
        scratch_shapes=[pltpu.VMEM((nb, S5_LANES), F32)]
        + [pltpu.VMEM((rows, S5_LANES // S5_CHUNKS), F32)] * (S5_CHUNKS * groups),
        compiler_params=_cparams("parallel", "arbitrary"),
        name="s5",
    )(u, h0, a_lanes, perm, perm.T, bm, cm, d, wg, bg, wo)


def _s5_params(a_re, a_im, log_dt, b_re, b_im, c_re, c_im):
    a_re = a_re.astype(F32)
    a_im = a_im.astype(F32)
    dt = jnp.exp(log_dt.astype(F32))[:, None]
    mag = jnp.exp(a_re * dt)
    ang = a_im * dt
    ab_re = mag * jnp.cos(ang)
    ab_im = mag * jnp.sin(ang)
    den = a_re * a_re + a_im * a_im
    nr = ab_re - 1.0
    ni = ab_im
    f_re = (nr * a_re + ni * a_im) / den
    f_im = (ni * a_re - nr * a_im) / den
    b_re = b_re.astype(F32)
    b_im = b_im.astype(F32)
    bb_re = f_re[..., None] * b_re - f_im[..., None] * b_im
    bb_im = f_re[..., None] * b_im + f_im[..., None] * b_re
    gpc = S5_GROUPS // S5_CHUNKS
    eye = jnp.eye(gpc, dtype=F32)

    def lanes(x):
        return x.reshape(S5_CHUNKS, gpc * S5_STATE)

    a_lanes = jnp.concatenate([lanes(ab_re), lanes(ab_im)], axis=1).reshape(1, S5_LANES)
    a_lanes = jnp.broadcast_to(a_lanes, (BATCH_GROUP, S5_LANES))

    def in_blocks(bb):
        bb = bb.reshape(S5_CHUNKS, gpc, S5_STATE, S5_GROUP)
        return jnp.einsum('cgpj,gh->cgjhp', bb, eye).reshape(S5_CHUNKS, gpc * S5_GROUP, gpc * S5_STATE)

    bm = jnp.concatenate([in_blocks(bb_re), in_blocks(bb_im)], axis=2).astype(BF16)

    def out_blocks(cc):
        cc = cc.astype(F32).reshape(S5_CHUNKS, gpc, S5_GROUP, S5_STATE)
        return jnp.einsum('cgjp,gh->cgphj', cc, eye).reshape(S5_CHUNKS, gpc * S5_STATE, gpc * S5_GROUP)

    cm = jnp.concatenate([out_blocks(c_re), -out_blocks(c_im)], axis=1).astype(BF16)
    return a_lanes, bm, cm


def _s5_state_to_lanes(h_re, h_im):
    bsz = h_re.shape[0]
    re = h_re.astype(F32).reshape(bsz, S5_CHUNKS, -1)
    im = h_im.astype(F32).reshape(bsz, S5_CHUNKS, -1)
    return jnp.concatenate([re, im], axis=2).reshape(bsz, S5_LANES)


def _s5_state_from_lanes(h):
    bsz = h.shape[0]
    h = h.reshape(bsz, S5_CHUNKS, 2, S5_GROUPS // S5_CHUNKS, S5_STATE)
    return (h[:, :, 0].reshape(bsz, S5_GROUPS, S5_STATE), h[:, :, 1].reshape(bsz, S5_GROUPS, S5_STATE))


def _ret_kernel(q_ref, k_ref, v_ref, g_ref, *rest, block_decay, carried):
    s0_ref = rest[0] if carried else None
    dm_ref, xi_ref, zeta_ref, wo_ref, yb_ref, st_ref, s_ref = rest[-7:]
    si = pl.program_id(1)

    @pl.when(si == 0)
    def _():
        s_ref[...] = s0_ref[...] if carried else jnp.zeros_like(s_ref)

    blk = dm_ref.shape[1]
    states = [s_ref[h] for h in range(RET_HEADS)]
    gated = []
    for sub in range(q_ref.shape[0] // blk):
        rows = slice(sub * blk, (sub + 1) * blk)
        heads = []
        for h in range(RET_HEADS):
            qh = q_ref[rows, h * RET_DK:(h + 1) * RET_DK]
            kh = k_ref[rows, h * RET_DK:(h + 1) * RET_DK]
            vh = v_ref[rows, h * RET_DV:(h + 1) * RET_DV]
            scores = lax.dot_general(qh, kh, (((1,), (1,)), ((), ())), preferred_element_type=F32) * dm_ref[h]
            o = _bdot(scores.astype(BF16), vh) + _bdot(qh, states[h].astype(BF16)) * xi_ref[h]
            o = o * lax.rsqrt(jnp.mean(o * o, axis=-1, keepdims=True) + EPS)
            gh = g_ref[rows, h * RET_DV:(h + 1) * RET_DV].astype(F32)
            heads.append((o * (gh * _sigmoid(gh))).astype(BF16))
            kz = (kh.astype(F32) * zeta_ref[h]).astype(BF16)
            kv = lax.dot_general(kz, vh, (((0,), (0,)), ((), ())), preferred_element_type=F32)
            states[h] = block_decay[h] * states[h] + kv
        gated.append(jnp.concatenate(heads, axis=1))
    for h in range(RET_HEADS):
        s_ref[h] = states[h]
    yb_ref[...] = _bdot(jnp.concatenate(gated, axis=0), wo_ref[...]).astype(BF16)

    @pl.when(si == pl.num_programs(1) - 1)
    def _():
        st_ref[...] = s_ref[...]


def _ret_tables(seq):
    cl = min(CHUNK, seq)
    blk = min(RET_BLOCK, seq)
    log_g = jnp.log(1.0 - 2.0 ** (-5.0 - jnp.arange(RET_HEADS, dtype=F32)))
    idx = jnp.arange(blk, dtype=F32)
    diff = idx[:, None] - idx[None, :]
    cn = jnp.arange(blk)[:, None] // cl
    cm = jnp.arange(blk)[None, :] // cl
    expo = jnp.where(cm == cn, jnp.abs(diff), diff)
    dm = jnp.where(cm <= cn, jnp.exp(log_g[:, None, None] * expo[None]), 0.0)
    xi = jnp.exp(log_g[:, None] * (idx + 1.0)[None, :])[..., None]
    zeta = jnp.exp(log_g[:, None] * (blk - 1.0 - idx)[None, :])[..., None]
    block_decay = tuple(math.exp(math.log(1.0 - 2.0 ** (-5.0 - h)) * blk) for h in range(RET_HEADS))
    return blk, dm, xi, zeta, block_decay


def _retention(q, k, v, g, s0, w_ret_out_b):
    bsz, seq, _ = q.shape
    blk, dm, xi, zeta, block_decay = _ret_tables(seq)
    step = min(RET_BLOCKS_PER_STEP * blk, seq)
    row = lambda w: pl.BlockSpec((None, step, w), lambda b, s: (b, s, 0))
    const = lambda shape: pl.BlockSpec(shape, lambda b, s: (0,) * len(shape))
    state = pl.BlockSpec((None, RET_HEADS, RET_DK, RET_DV), lambda b, s: (b, 0, 0, 0))
    carried = s0 is not None
    return pl.pallas_call(
        functools.partial(_ret_kernel, block_decay=block_decay, carried=carried),
        grid=(bsz, seq // step),
        in_specs=[row(RET_QK), row(RET_QK), row(RET_V), row(RET_V)] + [state] * carried
        + [const(dm.shape), const(xi.shape), const(zeta.shape), const((RET_V, D_MODEL))],
        out_specs=[row(D_MODEL), state],
        out_shape=[jax.ShapeDtypeStruct((bsz, seq, D_MODEL), BF16),
                   jax.ShapeDtypeStruct((bsz, RET_HEADS, RET_DK, RET_DV), F32)],
        scratch_shapes=[pltpu.VMEM((RET_HEADS, RET_DK, RET_DV), F32)],
        compiler_params=_cparams("parallel", "arbitrary"),
        name="retention",
    )(q, k, v, g, *([s0] * carried), dm, xi, zeta, w_ret_out_b)


def _to_row_tiles(val):
    n, w = val.shape
    return val.reshape(n * (w // LANE), LANE)


def _from_row_tiles(tiles, w):
    return tiles.reshape(tiles.shape[0] // (w // LANE), w)


def _merge_kernel(x_ref, ya_ref, yb_ref, ga_ref, gb_ref, gt_ref, sh_ref, sc_ref, g2_ref, wo_ref,
                  wr_ref, br_ref, x1_ref, n2_ref, re_ref, wc_ref, cnt_ref):
    tl = x_ref.shape[0]
    part = tl // MERGE_PARTS if tl % (MERGE_PARTS * 16) == 0 else tl
    nt_dot = lambda a, b: lax.dot_general(a, b, (((1,), (1,)), ((), ())), preferred_element_type=F32)
    logits = []
    for p in range(tl // part):
        rows = slice(p * part, (p + 1) * part)
        vec = lambda ref: ref[...] if ref.shape[0] == 1 else ref[rows, :]
        merged = _sigmoid(ga_ref[rows, :]) * ya_ref[rows, :] + _sigmoid(gb_ref[rows, :]) * yb_ref[rows, :]
        x1 = x_ref[rows, :] + vec(gt_ref) * _bdot(merged, wo_ref[...])
        x1_ref[rows, :] = x1.astype(BF16)
        n2 = x1 * lax.rsqrt(jnp.mean(x1 * x1, axis=-1, keepdims=True) + EPS) * g2_ref[...]
        n2 = n2 * (1.0 + vec(sc_ref)) + vec(sh_ref)
        for j in range(ROW_TILES):
            n2_ref[pl.ds(p * part * ROW_TILES + j, part, stride=ROW_TILES), :] = n2[:, j * LANE:(j + 1) * LANE]
        n2_hi = n2.astype(BF16)
        n2_lo = (n2 - n2_hi.astype(F32)).astype(BF16)
        logits.append(nt_dot(wr_ref[0], n2_hi) + nt_dot(wr_ref[0], n2_lo) + nt_dot(wr_ref[1], n2_hi))
    lt = jnp.concatenate(logits, axis=1) + br_ref[...]
    iota = lax.broadcasted_iota(jnp.int32, (8, tl), 0)
    gl = lt[0:8]
    gmax = jnp.max(gl, axis=0, keepdims=True)
    gi = jnp.min(jnp.where(gl == gmax, iota, 8), axis=0, keepdims=True)
    gw = 1.0 / jnp.sum(jnp.exp(gl - gmax), axis=0, keepdims=True)
    el = jnp.zeros((8, tl), F32)
    for g in range(MOE_GROUPS):
        el = jnp.where(gi == g, lt[8 * (g + 1):8 * (g + 2)], el)
    m1 = jnp.max(el, axis=0, keepdims=True)
    i1 = jnp.min(jnp.where(el == m1, iota, 8), axis=0, keepdims=True)
    el2 = jnp.where(iota == i1, -jnp.inf, el)
    m2 = jnp.max(el2, axis=0, keepdims=True)
    i2 = jnp.min(jnp.where(el2 == m2, iota, 8), axis=0, keepdims=True)
    e21 = jnp.exp(m2 - m1)
    w1 = gw / (1.0 + e21)
    w2 = w1 * e21
    lo = jnp.minimum(i1, i2)
    hi = jnp.maximum(i1, i2)
    pair = gi * PAIRS_PER_GROUP + lo * (MOE_EXPERTS - 1) - ((lo * (lo - 1)) >> 1) + (hi - lo - 1)
    re_ref[...] = jnp.where(iota == 0, pair, 0)
    rw = jnp.where(iota == 0, jnp.where(i1 < i2, w1, w2), jnp.where(iota == 1, jnp.where(i1 < i2, w2, w1), 0.0))
    eye = (lax.broadcasted_iota(jnp.int32, (8, LANE), 0)
           == lax.broadcasted_iota(jnp.int32, (8, LANE), 1)).astype(F32)
    wc_ref[...] = lax.dot_general(rw, eye, (((0,), (0,)), ((), ())),
                                  precision=lax.Precision.HIGHEST, preferred_element_type=F32)
    ids = lax.broadcasted_iota(jnp.int32, (N_PAIRS, tl), 0)
    hits = jnp.where(ids == pair, 1.0, 0.0)
    cnt_ref[...] = jnp.broadcast_to(jnp.sum(hits, axis=1, keepdims=True), (N_PAIRS, LANE)).astype(jnp.int32)


def _merge(x, ya, yb, ga, gb, gate1, shift2, scale2, g2, w_out_b, wr, br):
    bsz, seq, _ = x.shape
    tl = min(TOKEN_TILE, seq)
    row = pl.BlockSpec((None, tl, D_MODEL), lambda b, t: (b, t, 0))
    vec = _mod_spec(gate1, tl)
    const = lambda shape: pl.BlockSpec(shape, lambda b, t: (0,) * len(shape))
    route = pl.BlockSpec((None, 8, tl), lambda b, t: (b, 0, t))
    nt = seq // tl
    tiles = pl.BlockSpec((tl * ROW_TILES, LANE), lambda b, t: (b * nt + t, 0))
    return pl.pallas_call(
        _merge_kernel,
        grid=(bsz, nt),
        in_specs=[row, row, row, row, row, vec, vec, vec, const((1, D_MODEL)),
                  const((D_MODEL, D_MODEL)), const((2, ROUTE_ROWS, D_MODEL)), const((ROUTE_ROWS, 1))],
        out_specs=[row, tiles, route,
                   pl.BlockSpec((tl, LANE), lambda b, t: (b * nt + t, 0)),
                   pl.BlockSpec((None, N_PAIRS, LANE), lambda b, t: (b * nt + t, 0, 0))],
        out_shape=[jax.ShapeDtypeStruct((bsz, seq, D_MODEL), BF16),
                   jax.ShapeDtypeStruct((bsz * seq * ROW_TILES, LANE), F32),
                   jax.ShapeDtypeStruct((bsz, 8, seq), jnp.int32),
                   jax.ShapeDtypeStruct((bsz * seq, LANE), F32),
                   jax.ShapeDtypeStruct((bsz * nt, N_PAIRS, LANE), jnp.int32)],
        compiler_params=_cparams("parallel", "parallel"),
        name="merge",
    )(x, ya, yb, ga, gb, gate1, shift2, scale2, g2, w_out_b, wr, br)


def _router_weights(w_rg, b_rg, w_re, b_re):
    wr = jnp.zeros((ROUTE_ROWS, D_MODEL), F32)
    wr = wr.at[0:MOE_GROUPS].set(w_rg.astype(F32).T)
    wr = wr.at[8:].set(jnp.transpose(w_re.astype(F32), (0, 2, 1)).reshape(N_EXPERTS, D_MODEL))
    br = jnp.full((ROUTE_ROWS,), -1e30, F32)
    br = br.at[0:MOE_GROUPS].set(b_rg.astype(F32))
    br = br.at[8:].set(b_re.astype(F32).reshape(N_EXPERTS))
    wr_hi = wr.astype(BF16)
    wr_lo = (wr - wr_hi.astype(F32)).astype(BF16)
    return jnp.stack([wr_hi, wr_lo]), br.reshape(ROUTE_ROWS, 1)


def _row_gather(idx_ref, src_hbm, buf, sem, slot, n, span):
    def body(pair, carry):
        for queue in range(DMA_QUEUES):
            r = pair * DMA_QUEUES + queue
            src = pl.multiple_of(idx_ref[0, r] * span, span)
            dst = pl.multiple_of(r * span, span)
            pltpu.make_async_copy(src_hbm.at[pl.ds(src, span), :],
                                  buf.at[slot, pl.ds(dst, span), :], sem.at[slot]).start(priority=queue)
        return carry
    lax.fori_loop(0, n // DMA_QUEUES, body, 0, unroll=DMA_UNROLL // DMA_QUEUES)


def _row_gather_wait(src_hbm, buf, sem, slot, n, span):
    pltpu.make_async_copy(src_hbm.at[pl.ds(0, n * span), :], buf.at[slot], sem.at[slot]).wait()


def _plan_kernel(e_ref, base_ref, c0_ref, tri_ref, pos_ref, carry_ref):
    @pl.when(pl.program_id(0) == 0)
    def _():
        carry_ref[...] = c0_ref[...]

    tb = e_ref.shape[1]
    hit = lax.broadcasted_iota(jnp.int32, (N_PAIRS, tb), 0) == e_ref[0:1, :]
    incl = _bdot(jnp.where(hit, 1.0, 0.0).astype(BF16), tri_ref[...])
    row = jnp.where(hit, incl - 1.0 + carry_ref[...] + base_ref[...], 0.0)
    pos_ref[...] = jnp.sum(row, axis=0, keepdims=True).astype(jnp.int32)
    carry_ref[...] = carry_ref[...] + incl[:, tb - 1:tb]


def _plan(route_e, pad_start, placed):
    bsz, _, seq = route_e.shape
    tb = min(ROUTE_TILE, seq)
    nb = seq // tb
    r = jnp.arange(tb)
    tri = (r[:, None] <= r[None, :]).astype(BF16)
    col = pl.BlockSpec((N_PAIRS, 1), lambda i: (0, 0))
    return pl.pallas_call(
        _plan_kernel,
        grid=(bsz * nb,),
        in_specs=[pl.BlockSpec((None, 8, tb), lambda i: (i // nb, 0, i % nb)), col, col,
                  pl.BlockSpec((tb, tb), lambda i: (0, 0))],
        out_specs=pl.BlockSpec((None, 1, tb), lambda i: (i, 0, 0)),
        out_shape=jax.ShapeDtypeStruct((bsz * nb, 1, tb), jnp.int32),
        scratch_shapes=[pltpu.VMEM((N_PAIRS, 1), F32)],
        compiler_params=_cparams("arbitrary"),
        name="plan",
    )(route_e, pad_start.astype(F32).reshape(N_PAIRS, 1), placed.astype(F32).reshape(N_PAIRS, 1), tri)


ZERO_CHUNKS = tuple(2 ** b for b in range(8, -1, -1))


def _dispatch_kernel(lo_ref, hi_ref, nu_ref, posa_ref, xa_ref, posb_ref, xb_ref, xs_hbm, zero_ref, sem,
                     *, tb, nb_rows, tm, n_tiles):
    last = pl.num_programs(0) - 1

    def scatter(pos_ref, x_ref, n):
        def body(pair, carry):
            for queue in range(DMA_QUEUES):
                r = pair * DMA_QUEUES + queue
                src = pl.multiple_of(r * ROW_TILES, ROW_TILES)
                dst = pl.multiple_of(pos_ref[0, r] * ROW_TILES, ROW_TILES)
                pltpu.make_async_copy(x_ref.at[pl.ds(src, ROW_TILES), :],
                                      xs_hbm.at[pl.ds(dst, ROW_TILES), :], sem.at[0]).start(priority=queue)
            return carry
        lax.fori_loop(0, n // DMA_QUEUES, body, 0, unroll=DMA_UNROLL // DMA_QUEUES)

    def scatter_wait(x_ref, n):
        pltpu.make_async_copy(x_ref, xs_hbm.at[pl.ds(0, n * ROW_TILES), :], sem.at[0]).wait()

    def zero_copy(row, size):
        dst = pl.multiple_of(row * ROW_TILES, ROW_TILES)
        return pltpu.make_async_copy(zero_ref.at[pl.ds(0, size * ROW_TILES), :],
                                     xs_hbm.at[pl.ds(dst, size * ROW_TILES), :], sem.at[1])

    @pl.when(pl.program_id(0) < last)
    def _():
        scatter(posa_ref, xa_ref, tb)
        scatter_wait(xa_ref, tb)

    @pl.when(pl.program_id(0) == last)
    def _():
        scatter(posb_ref, xb_ref, nb_rows)
        zero_ref[...] = jnp.zeros_like(zero_ref)

        def per_pair(e, carry):
            lo = lo_ref[e]
            n = hi_ref[e] - lo
            for wait in (False, True):
                row = lo
                for size in ZERO_CHUNKS:
                    @pl.when((n & size) != 0)
                    def _(row=row, size=size, wait=wait):
                        zero_copy(row, size).wait() if wait else zero_copy(row, size).start()
                    row = row + (n & size)
            return carry
        lax.fori_loop(0, N_PAIRS, per_pair, 0)

        chunk = min(ZERO_CHUNKS[0], tm)
        for wait in (False, True):
            def per_tile(t, carry, wait=wait):
                for c in range(tm // chunk):
                    cp = zero_copy(t * tm + c * chunk, chunk)
                    cp.wait() if wait else cp.start()
                return carry
            lax.fori_loop(nu_ref[0], n_tiles, per_tile, 0)
        scatter_wait(xb_ref, nb_rows)


def _dispatch(pos_a, rows_a, pos_b, rows_b, pad_lo, pad_hi, n_used, n_rows, tm):
    tb = min(DISPATCH_TILE, pos_a.size)
    na = pos_a.size // tb
    pos_a = pos_a.reshape(na, 1, tb)
    nb_rows = rows_b.shape[0] // ROW_TILES
    pos_b = pos_b.reshape(1, 1, nb_rows)
    block_a = lambda i, lo, hi, nu: (jnp.minimum(i, na - 1), 0, 0)
    grid_spec = pltpu.PrefetchScalarGridSpec(
        num_scalar_prefetch=3,
        grid=(na + 1,),
        in_specs=[pl.BlockSpec((None, 1, tb), block_a, memory_space=pltpu.SMEM),
                  pl.BlockSpec((tb * ROW_TILES, LANE), lambda i, lo, hi, nu: (jnp.minimum(i, na - 1), 0)),
                  pl.BlockSpec((None, 1, nb_rows), lambda i, lo, hi, nu: (0, 0, 0), memory_space=pltpu.SMEM),
                  pl.BlockSpec((nb_rows * ROW_TILES, LANE), lambda i, lo, hi, nu: (0, 0))],
        out_specs=pl.BlockSpec(memory_space=pl.ANY),
        scratch_shapes=[pltpu.VMEM((ZERO_CHUNKS[0] * ROW_TILES, LANE), F32), pltpu.SemaphoreType.DMA((2,))],
    )
    return pl.pallas_call(
        functools.partial(_dispatch_kernel, tb=tb, nb_rows=nb_rows, tm=tm, n_tiles=n_rows // tm),
        grid_spec=grid_spec,
        out_shape=jax.ShapeDtypeStruct((n_rows * ROW_TILES, LANE), F32),
        compiler_params=_cparams("arbitrary"),
        name="dispatch",
    )(pad_lo, pad_hi, n_used, pos_a, rows_a, pos_b, rows_b)


def _expert_kernel(ta_ref, tb_ref, nu_ref, x_ref, w1a_ref, w3a_ref, w2a_ref, w1b_ref, w3b_ref, w2b_ref,
                   y_ref, *, tm):
    @pl.when(pl.program_id(0) < nu_ref[0])
    def _():
        xb = _from_row_tiles(x_ref[...], D_MODEL).astype(BF16)
        ys = []
        for w1_ref, w3_ref, w2_ref in ((w1a_ref, w3a_ref, w2a_ref), (w1b_ref, w3b_ref, w2b_ref)):
            a = _bdot(xb, w1_ref[...])
            h = a * _sigmoid(a) * _bdot(xb, w3_ref[...])
            ys.append(_bdot(h.astype(BF16), w2_ref[...]))
        y_ref[...] = _to_row_tiles(jnp.concatenate(ys, axis=1))

    @pl.when(pl.program_id(0) >= nu_ref[0])
    def _():
        y_ref[...] = jnp.zeros_like(y_ref)


def _experts(xs, tile_a, tile_b, n_used, w1b, w3b, w2b, tm):
    n_tiles = xs.shape[0] // (tm * ROW_TILES)
    tile = lambda i, ta, tb, nu: (jnp.maximum(jnp.minimum(i, nu[0] - 1), 0), 0)
    first = lambda i, ta, tb, nu: (ta[i], 0, 0)
    second = lambda i, ta, tb, nu: (tb[i], 0, 0)
    up = lambda which: pl.BlockSpec((None, D_MODEL, MOE_FF), which)
    down = lambda which: pl.BlockSpec((None, MOE_FF, D_MODEL), which)
    grid_spec = pltpu.PrefetchScalarGridSpec(
        num_scalar_prefetch=3,
        grid=(n_tiles,),
        in_specs=[pl.BlockSpec((tm * ROW_TILES, LANE), tile),
                  up(first), up(first), down(first), up(second), up(second), down(second)],
        out_specs=pl.BlockSpec((tm * 2 * ROW_TILES, LANE), lambda i, ta, tb, nu: (i, 0)),
    )
    return pl.pallas_call(
        functools.partial(_expert_kernel, tm=tm),
        grid_spec=grid_spec,
        out_shape=jax.ShapeDtypeStruct((2 * xs.shape[0], LANE), F32),
        compiler_params=_cparams("arbitrary"),
        name="experts",
    )(tile_a, tile_b, n_used, xs, w1b, w3b, w2b, w1b, w3b, w2b)


def _expert_layout(counts, tm, n_tok):
    n_tiles = n_tok // tm + MOE_GROUPS * PAIRS_PER_GROUP
    padded = ((counts + tm - 1) // tm) * tm
    pad_end = jnp.cumsum(padded)
    pad_start = pad_end - padded
    n_used = pad_end[-1] // tm
    tile_start = jnp.minimum(jnp.arange(n_tiles, dtype=jnp.int32), n_used - 1) * tm
    tile_pair = jnp.sum((pad_end[None, :] <= tile_start[:, None]).astype(jnp.int32), axis=1)
    tile_pair = jnp.minimum(tile_pair, MOE_GROUPS * PAIRS_PER_GROUP - 1)
    members = [(lo, hi) for lo in range(MOE_EXPERTS) for hi in range(lo + 1, MOE_EXPERTS)]
    lo_of = jnp.asarray([m[0] for m in members], jnp.int32)
    hi_of = jnp.asarray([m[1] for m in members], jnp.int32)
    group = tile_pair // PAIRS_PER_GROUP
    tile_a = group * MOE_EXPERTS + lo_of[tile_pair % PAIRS_PER_GROUP]
    tile_b = group * MOE_EXPERTS + hi_of[tile_pair % PAIRS_PER_GROUP]
    return (pad_start, pad_start + counts, pad_end, tile_a.astype(jnp.int32), tile_b.astype(jnp.int32),
            n_used.astype(jnp.int32).reshape(1), n_tiles * tm)


def _combine_kernel(cur_ref, nxt_ref, x1_ref, wc_ref, gt_ref, fn_ref, y_hbm, o_ref, ybuf, sem, *, tl):
    i = pl.program_id(0)
    slot = i % 2

    span = 2 * ROW_TILES

    @pl.when(i == 0)
    def _():
        _row_gather(cur_ref, y_hbm, ybuf, sem, 0, tl, span)

    @pl.when(i + 1 < pl.num_programs(0))
    def _():
        _row_gather(nxt_ref, y_hbm, ybuf, sem, 1 - slot, tl, span)

    _row_gather_wait(y_hbm, ybuf, sem, slot, tl, span)
    both = _from_row_tiles(ybuf[slot], 2 * D_MODEL)
    moe = wc_ref[:, 0:1] * both[:, :D_MODEL] + wc_ref[:, 1:2] * both[:, D_MODEL:]
    x2 = x1_ref[...].astype(F32) + gt_ref[...] * moe
    o_ref[...] = x2 * lax.rsqrt(jnp.mean(x2 * x2, axis=-1, keepdims=True) + EPS) * fn_ref[...]


def _combine(x1_flat, pos, y_rows, wcol, gate2, final_norm, seq):
    n_tok = x1_flat.shape[0]
    n_tiles, _, tl = pos.shape
    last = n_tiles - 1
    if gate2.shape[1] == 1:
        gate_spec = pl.BlockSpec((None, 1, D_MODEL), lambda i: ((i * tl) // seq, 0, 0))
    else:
        gate2 = gate2.reshape(n_tok, D_MODEL)
        gate_spec = pl.BlockSpec((tl, D_MODEL), lambda i: (i, 0))
    return pl.pallas_call(
        functools.partial(_combine_kernel, tl=tl),
        grid=(n_tiles,),
        in_specs=[pl.BlockSpec((None, 1, tl), lambda i: (i, 0, 0), memory_space=pltpu.SMEM),
                  pl.BlockSpec((None, 1, tl), lambda i: (jnp.minimum(i + 1, last), 0, 0),
                               memory_space=pltpu.SMEM),
                  pl.BlockSpec((tl, D_MODEL), lambda i: (i, 0)),
                  pl.BlockSpec((tl, LANE), lambda i: (i, 0)),
                  gate_spec,
                  pl.BlockSpec((1, D_MODEL), lambda i: (0, 0)),
                  pl.BlockSpec(memory_space=pl.ANY)],
        out_specs=pl.BlockSpec((tl, D_MODEL), lambda i: (i, 0)),
        out_shape=jax.ShapeDtypeStruct((n_tok, D_MODEL), F32),
        scratch_shapes=[pltpu.VMEM((2, 2 * tl * ROW_TILES, LANE), F32), pltpu.SemaphoreType.DMA((2,))],
        compiler_params=_cparams("arbitrary"),
        name="combine",
    )(pos, pos, x1_flat, wcol, gate2, final_norm, y_rows)


def _rope_tables(seq, offset):
    half = RET_DK // 2
    theta = 1.0 / (ROPE_BASE ** jnp.linspace(0.0, 1.0, half, dtype=F32))
    pos = offset + jnp.arange(seq)
    ang = pos.astype(F32)[:, None] * theta[None, :]
    cos = jnp.cos(ang)
    sin = jnp.sin(ang)
    return jnp.concatenate([cos, cos], axis=1), jnp.concatenate([-sin, sin], axis=1)


def _mixers(x, mod, h0, s0, offset, p):
    bsz, seq, _ = x.shape
    mods = [m.reshape(bsz, 1, D_MODEL) for m in jnp.split(mod, 6, axis=-1)]
    cos2, sin2 = _rope_tables(seq, offset)
    flat = seq < TOKEN_TILE and bsz * seq <= TOKEN_TILE
    sets, rows = (1, bsz * seq) if flat else (bsz, seq)
    tok = lambda a: a.reshape(sets, rows, a.shape[-1])
    if flat:
        mods = [jnp.broadcast_to(m, (bsz, seq, D_MODEL)).reshape(1, rows, D_MODEL) for m in mods]
        cos2, sin2 = jnp.tile(cos2, (bsz, 1)), jnp.tile(sin2, (bsz, 1))
    sh1, sc1, gt1, sh2, sc2, gt2 = mods
    seqs = lambda a: a.reshape(bsz, seq, a.shape[-1])
    u, q, k, v, g, ga, gb = _inproj(tok(x), sh1, sc1, p['norm1'], p['w_in'], cos2, sin2)
    ya, h_t = _s5(seqs(u), h0, p['a_lanes'], p['bm'], p['cm'], p['d'], p['w_glu'], p['b_glu'], p['w_s5_out'])
    yb, s_t = _retention(seqs(q), seqs(k), seqs(v), seqs(g), s0, p['w_ret_out'])
    x1, n2, route_e, wcol, cnt = _merge(tok(x), tok(ya), tok(yb), ga, gb, gt1, sh2, sc2, p['norm2'],
                                        p['w_out'], p['wr'], p['br'])
    return dict(x1=x1.reshape(bsz * seq, D_MODEL), n2=n2, route=route_e, wcol=wcol,
                counts=jnp.sum(cnt[:, :, 0], axis=0), gate2=gt2, seq=rows, h=h_t, s=s_t)


def _moe(a, b, p, final_norm):
    n_tok = a['x1'].shape[0] + b['x1'].shape[0]
    pad_start, pad_lo, pad_hi, tile_a, tile_b, n_used, n_rows = _expert_layout(
        a['counts'] + b['counts'], EXPERT_TILE, n_tok)
    pos_a = _plan(a['route'], pad_start, jnp.zeros_like(a['counts']))
    pos_b = _plan(b['route'], pad_start, a['counts'])
    xs = _dispatch(pos_a, a['n2'], pos_b, b['n2'], pad_lo, pad_hi, n_used, n_rows, EXPERT_TILE)
    y_rows = _experts(xs, tile_a, tile_b, n_used, p['w1'], p['w3'], p['w2'], EXPERT_TILE)
    return [_combine(m['x1'], pos, y_rows, m['wcol'], m['gate2'], final_norm, m['seq'])
            for m, pos in ((a, pos_a), (b, pos_b))]


def kernel(x_prompt, x_sample, state_s5_re, state_s5_im, state_ret, c_prompt, c_sample, w_ada, b_ada, norm1, norm2, w_in, s5_a_re, s5_a_im, s5_log_dt, s5_b_re, s5_b_im, s5_c_re, s5_c_im, s5_d, s5_w_glu, s5_b_glu, w_s5_out, w_ret_out, w_out, w_rg, b_rg, w_re, b_re, w1, w3, w2, final_norm):
    depth = w_ada.shape[0]
    assert depth == 1
    bp = x_prompt.shape[0]
    bs, seq_s, _ = x_sample.shape
    l = 0
    a_lanes, bm, cm = _s5_params(s5_a_re[l], s5_a_im[l], s5_log_dt[l], s5_b_re[l], s5_b_im[l],
                                 s5_c_re[l], s5_c_im[l])
    wr, br = _router_weights(w_rg[l], b_rg[l], w_re[l], b_re[l])
    p = dict(
        norm1=norm1[l].astype(F32).reshape(1, D_MODEL), norm2=norm2[l].astype(F32).reshape(1, D_MODEL),
        w_in=w_in[l].astype(BF16), a_lanes=a_lanes, bm=bm, cm=cm,
        d=s5_d[l].astype(F32).reshape(1, S5_WIDTH), w_glu=s5_w_glu[l].astype(BF16),
        b_glu=s5_b_glu[l].astype(F32).reshape(1, S5_WIDTH), w_s5_out=w_s5_out[l].astype(BF16),
        w_ret_out=w_ret_out[l].astype(BF16), w_out=w_out[l].astype(BF16), wr=wr, br=br,
        w1=w1[l].astype(BF16).reshape(N_EXPERTS, D_MODEL, MOE_FF),
        w3=w3[l].astype(BF16).reshape(N_EXPERTS, D_MODEL, MOE_FF),
        w2=w2[l].astype(BF16).reshape(N_EXPERTS, MOE_FF, D_MODEL))
    fn = final_norm.astype(F32).reshape(1, D_MODEL)
    mod = _mod(jnp.concatenate([c_prompt, c_sample], axis=0).astype(F32), w_ada[l], b_ada[l])

    h0_p = jnp.zeros((bp, S5_LANES), F32)
    prompt = _mixers(x_prompt, mod[:bp], h0_p, None, 0, p)
    h0_s = _s5_state_to_lanes(state_s5_re[l], state_s5_im[l])
    sample = _mixers(x_sample, mod[bp:], h0_s, state_ret[l].astype(F32), PAST_LEN, p)
    y_p, y_s = _moe(prompt, sample, p, fn)
    p_re, p_im = _s5_state_from_lanes(prompt['h'])
    s_re, s_im = _s5_state_from_lanes(sample['h'])
    return (y_p.reshape(x_prompt.shape), y_s.reshape(x_sample.shape), p_re[None], p_im[None],
            prompt['s'][None], s_re[None], s_im[None], sample['s'][None])
```

```python
import functools
import math

import jax
import jax.numpy as jnp
from jax import lax
from jax.experimental import pallas as pl
from jax.experimental.pallas import tpu as pltpu

F32 = jnp.float32
BF16 = jnp.bfloat16

D_MODEL = 1024
PAST_LEN = 2048
CHUNK = 64
S5_WIDTH = 512
S5_GROUP = 16
S5_GROUPS = 32
S5_STATE = 64
S5_LANES = 2 * S5_GROUPS * S5_STATE
S5_CHUNKS = 4
RET_HEADS = 4
RET_DK = 128
RET_DV = 256
RET_QK = RET_HEADS * RET_DK
RET_V = RET_HEADS * RET_DV
ROPE_BASE = 10000.0
MOE_GROUPS = 4
MOE_EXPERTS = 8
N_EXPERTS = MOE_GROUPS * MOE_EXPERTS
PAIRS_PER_GROUP = MOE_EXPERTS * (MOE_EXPERTS - 1) // 2
N_PAIRS = 128
MOE_FF = 256
EPS = 1e-6
IN_WIDTH = S5_WIDTH + 2 * RET_QK + 2 * RET_V + 2 * D_MODEL
ROUTE_ROWS = 8 * (1 + MOE_GROUPS)

BATCH_GROUP = 8
S5_GROUPS_PER_STEP = 2
TOKEN_TILE = 1024
MERGE_PARTS = 2
INPROJ_TILE = 512
S5_TIME_TILE = 64
RET_BLOCK = 256
RET_BLOCKS_PER_STEP = 4
EXPERT_TILE = 256
ROUTE_TILE = 512
DISPATCH_TILE = 2048
VMEM_LIMIT = 56 * 1024 * 1024
LANE = 128
SUBLANE = 8
ROW_TILES = D_MODEL // LANE
DMA_UNROLL = 8
DMA_QUEUES = 2


def _cparams(*sem):
    return pltpu.CompilerParams(dimension_semantics=sem, vmem_limit_bytes=VMEM_LIMIT)


def _bdot(a, b):
    return jnp.dot(a, b, preferred_element_type=F32)


def _sigmoid(x):
    return 0.5 * jnp.tanh(0.5 * x) + 0.5


def _mod_kernel(c_ref, w_ref, b_ref, o_ref):
    c = c_ref[...]
    a = (c * _sigmoid(c)).astype(BF16)
    o_ref[...] = _bdot(a, w_ref[...].astype(BF16)) + b_ref[...]


def _mod(c, w_ada, b_ada):
    n = c.shape[0]
    return pl.pallas_call(
        _mod_kernel,
        grid=(6,),
        in_specs=[pl.BlockSpec((n, D_MODEL), lambda j: (0, 0)),
                  pl.BlockSpec((D_MODEL, D_MODEL), lambda j: (0, j)),
                  pl.BlockSpec((1, D_MODEL), lambda j: (0, j))],
        out_specs=pl.BlockSpec((n, D_MODEL), lambda j: (0, j)),
        out_shape=jax.ShapeDtypeStruct((n, 6 * D_MODEL), F32),
        compiler_params=_cparams("parallel"),
        name="mod",
    )(c, w_ada, b_ada.reshape(1, -1))


def _rope(x, cos2, sin2):
    return x * cos2 + pltpu.roll(x, RET_DK // 2, 1) * sin2


def _inproj_kernel(x_ref, sh_ref, sc_ref, g_ref, w_ref, cos_ref, sin_ref,
                   u_ref, q_ref, k_ref, v_ref, gs_ref, ga_ref, gb_ref):
    x = x_ref[...]
    n = x * lax.rsqrt(jnp.mean(x * x, axis=-1, keepdims=True) + EPS) * g_ref[...]
    nb = (n * (1.0 + sc_ref[...]) + sh_ref[...]).astype(BF16)
    cos2 = cos_ref[...]
    sin2 = sin_ref[...]
    o = 0
    u_ref[...] = _bdot(nb, w_ref[:, o:o + S5_WIDTH]).astype(BF16)
    o += S5_WIDTH
    q = _bdot(nb, w_ref[:, o:o + RET_QK])
    for h in range(RET_HEADS):
        head = slice(h * RET_DK, (h + 1) * RET_DK)
        q_ref[:, head] = _rope(q[:, head], cos2, sin2).astype(BF16)
    o += RET_QK
    k = _bdot(nb, w_ref[:, o:o + RET_QK])
    for h in range(RET_HEADS):
        head = slice(h * RET_DK, (h + 1) * RET_DK)
        k_ref[:, head] = (_rope(k[:, head], cos2, sin2) * (RET_DK ** -0.5)).astype(BF16)
    o += RET_QK
    for ref in (v_ref, gs_ref, ga_ref, gb_ref):
        ref[...] = _bdot(nb, w_ref[:, o:o + D_MODEL]).astype(BF16)
        o += D_MODEL


def _mod_spec(vec, tl):
    if vec.shape[1] == 1:
        return pl.BlockSpec((None, 1, D_MODEL), lambda b, t: (b, 0, 0))
    return pl.BlockSpec((None, tl, D_MODEL), lambda b, t: (b, t, 0))


def _inproj(x, shift, scale, g1, w_in_b, cos2, sin2):
    bsz, seq, _ = x.shape
    tl = min(INPROJ_TILE, seq)
    row = lambda w: pl.BlockSpec((None, tl, w), lambda b, t: (b, t, 0))
    vec = _mod_spec(shift, tl)
    shapes = [S5_WIDTH, RET_QK, RET_QK, RET_V, RET_V, D_MODEL, D_MODEL]
    return pl.pallas_call(
        _inproj_kernel,
        grid=(bsz, seq // tl),
        in_specs=[row(D_MODEL), vec, vec,
                  pl.BlockSpec((1, D_MODEL), lambda b, t: (0, 0)),
                  pl.BlockSpec((D_MODEL, IN_WIDTH), lambda b, t: (0, 0), pipeline_mode=pl.Buffered(1)),
                  pl.BlockSpec((tl, RET_DK), lambda b, t: (t, 0)),
                  pl.BlockSpec((tl, RET_DK), lambda b, t: (t, 0))],
        out_specs=[row(w) for w in shapes],
        out_shape=[jax.ShapeDtypeStruct((bsz, seq, w), BF16) for w in shapes],
        compiler_params=_cparams("parallel", "parallel"),
        name="inproj",
    )(x, shift, scale, g1, w_in_b, cos2, sin2)


def _gelu_tanh(y):
    return 0.5 * y * (1.0 + jnp.tanh(math.sqrt(2.0 / math.pi) * (y + 0.044715 * (y * y * y))))


def _s5_kernel(u_ref, h0_ref, a_ref, pm_ref, pt_ref, bm_ref, cm_ref, d_ref, wg_ref, bg_ref, wo_ref,
               ya_ref, ht_ref, hs_ref, *bu_refs, tt):
    ti = pl.program_id(1)
    rows = BATCH_GROUP * tt
    half = S5_LANES // (2 * S5_CHUNKS)

    @pl.when(ti == 0)
    def _():
        hs_ref[...] = h0_ref[...]

    kc = S5_WIDTH // S5_CHUNKS
    n_grp = u_ref.shape[0] // BATCH_GROUP
    streams = lambda g: slice(g * BATCH_GROUP, (g + 1) * BATCH_GROUP)
    u2, ys = {}, {g: [] for g in range(n_grp)}

    def permute(g):
        u2[g] = _bdot(pm_ref[...], u_ref[streams(g)].reshape(rows, S5_WIDTH)).astype(BF16)

    def input_map(g, c):
        bu_refs[g * S5_CHUNKS + c][...] = _bdot(u2[g][:, c * kc:(c + 1) * kc], bm_ref[c])

    def recurrence(g, c):
        bu_ref = bu_refs[g * S5_CHUNKS + c]
        lre = slice(c * 2 * half, c * 2 * half + half)
        lim = slice(c * 2 * half + half, (c + 1) * 2 * half)
        are = a_ref[:, lre]
        aim = a_ref[:, lim]
        hre = hs_ref[streams(g), lre]
        him = hs_ref[streams(g), lim]
        for t in range(tt):
            rsel = slice(t * BATCH_GROUP, (t + 1) * BATCH_GROUP)
            hre, him = (are * hre - aim * him + bu_ref[rsel, :half],
                        are * him + aim * hre + bu_ref[rsel, half:])
            bu_ref[rsel, :half] = hre
            bu_ref[rsel, half:] = him
        hs_ref[streams(g), lre] = hre
        hs_ref[streams(g), lim] = him

    def output_map(g, c):
        ys[g].append(_bdot(bu_refs[g * S5_CHUNKS + c][...].astype(BF16), cm_ref[c]))

    def tail(g):
        y = jnp.concatenate(ys[g], axis=1) + d_ref[...] * u2[g].astype(F32)
        z = _gelu_tanh(y)
        gl = _bdot(z.astype(BF16), wg_ref[...]) + bg_ref[...]
        o = (z * _sigmoid(gl)).astype(BF16)
        ob = _bdot(pt_ref[...], o).astype(BF16)
        ya_ref[streams(g)] = _bdot(ob, wo_ref[...]).reshape(BATCH_GROUP, tt, D_MODEL).astype(BF16)

    permute(0)
    for c in range(S5_CHUNKS):
        input_map(0, c)
    for g in range(n_grp + 1):
        if g + 1 < n_grp:
            permute(g + 1)
        for c in range(S5_CHUNKS):
            if g >= 1:
                output_map(g - 1, c)
            if g + 1 < n_grp:
                input_map(g + 1, c)
            if g < n_grp:
                recurrence(g, c)
        if g >= 1:
            tail(g - 1)

    @pl.when(ti == pl.num_programs(1) - 1)
    def _():
        ht_ref[...] = hs_ref[...]


def _s5(u, h0, a_lanes, bm, cm, d, wg, bg, wo):
    bsz, seq, _ = u.shape
    tt = min(S5_TIME_TILE, seq)
    rows = BATCH_GROUP * tt
    groups = S5_GROUPS_PER_STEP if bsz % (BATCH_GROUP * S5_GROUPS_PER_STEP) == 0 else 1
    nb = BATCH_GROUP * groups
    const = lambda shape: pl.BlockSpec(shape, lambda b, t: (0,) * len(shape))
    r = jnp.arange(rows)
    perm = ((r[:, None] % BATCH_GROUP) * tt + r[:, None] // BATCH_GROUP == r[None, :]).astype(BF16)
    return pl.pallas_call(
        functools.partial(_s5_kernel, tt=tt),
        grid=(bsz // nb, seq // tt),
        in_specs=[pl.BlockSpec((nb, tt, S5_WIDTH), lambda b, t: (b, t, 0)),
                  pl.BlockSpec((nb, S5_LANES), lambda b, t: (b, 0)),
                  const((BATCH_GROUP, S5_LANES)), const((rows, rows)), const((rows, rows)),
                  const(bm.shape), const(cm.shape), const((1, S5_WIDTH)),
                  const((S5_WIDTH, S5_WIDTH)), const((1, S5_WIDTH)), const((S5_WIDTH, D_MODEL))],
        out_specs=[pl.BlockSpec((nb, tt, D_MODEL), lambda b, t: (b, t, 0)),
                   pl.BlockSpec((nb, S5_LANES), lambda b, t: (b, 0))],
        out_shape=[jax.ShapeDtypeStruct((bsz, seq, D_MODEL), BF16),
                   jax.ShapeDtypeStruct((bsz, S5_LANES), F32)],
        scratch_shapes=[pltpu.VMEM((nb, S5_LANES), F32)]
        + [pltpu.VMEM((rows, S5_LANES // S5_CHUNKS), F32)] * (S5_CHUNKS * groups),
        compiler_params=_cparams("parallel", "arbitrary"),
        name="s5",
    )(u, h0, a_lanes, perm, perm.T, bm, cm, d, wg, bg, wo)


def _s5_params(a_re, a_im, log_dt, b_re, b_im, c_re, c_im):
    a_re = a_re.astype(F32)
    a_im = a_im.astype(F32)
    dt = jnp.exp(log_dt.astype(F32))[:, None]
    mag = jnp.exp(a_re * dt)
    ang = a_im * dt
    ab_re = mag * jnp.cos(ang)
    ab_im = mag * jnp.sin(ang)
    den = a_re * a_re + a_im * a_im
    nr = ab_re - 1.0
    ni = ab_im
    f_re = (nr * a_re + ni * a_im) / den
    f_im = (ni * a_re - nr * a_im) / den
    b_re = b_re.astype(F32)
    b_im = b_im.astype(F32)
    bb_re = f_re[..., None] * b_re - f_im[..., None] * b_im
    bb_im = f_re[..., None] * b_im + f_im[..., None] * b_re
    gpc = S5_GROUPS // S5_CHUNKS
    eye = jnp.eye(gpc, dtype=F32)

    def lanes(x):
        return x.reshape(S5_CHUNKS, gpc * S5_STATE)

    a_lanes = jnp.concatenate([lanes(ab_re), lanes(ab_im)], axis=1).reshape(1, S5_LANES)
    a_lanes = jnp.broadcast_to(a_lanes, (BATCH_GROUP, S5_LANES))

    def in_blocks(bb):
        bb = bb.reshape(S5_CHUNKS, gpc, S5_STATE, S5_GROUP)
        return jnp.einsum('cgpj,gh->cgjhp', bb, eye).reshape(S5_CHUNKS, gpc * S5_GROUP, gpc * S5_STATE)

    bm = jnp.concatenate([in_blocks(bb_re), in_blocks(bb_im)], axis=2).astype(BF16)

    def out_blocks(cc):
        cc = cc.astype(F32).reshape(S5_CHUNKS, gpc, S5_GROUP, S5_STATE)
        return jnp.einsum('cgjp,gh->cgphj', cc, eye).reshape(S5_CHUNKS, gpc * S5_STATE, gpc * S5_GROUP)

    cm = jnp.concatenate([out_blocks(c_re), -out_blocks(c_im)], axis=1).astype(BF16)
    return a_lanes, bm, cm


def _s5_state_to_lanes(h_re, h_im):
    bsz = h_re.shape[0]
    re = h_re.astype(F32).reshape(bsz, S5_CHUNKS, -1)
    im = h_im.astype(F32).reshape(bsz, S5_CHUNKS, -1)
    return jnp.concatenate([re, im], axis=2).reshape(bsz, S5_LANES)


def _s5_state_from_lanes(h):
    bsz = h.shape[0]
    h = h.reshape(bsz, S5_CHUNKS, 2, S5_GROUPS // S5_CHUNKS, S5_STATE)
    return (h[:, :, 0].reshape(bsz, S5_GROUPS, S5_STATE), h[:, :, 1].reshape(bsz, S5_GROUPS, S5_STATE))


def _ret_kernel(q_ref, k_ref, v_ref, g_ref, *rest, block_decay, carried):
    s0_ref = rest[0] if carried else None
    dm_ref, xi_ref, zeta_ref, wo_ref, yb_ref, st_ref, s_ref = rest[-7:]
    si = pl.program_id(1)

    @pl.when(si == 0)
    def _():
        s_ref[...] = s0_ref[...] if carried else jnp.zeros_like(s_ref)

    blk = dm_ref.shape[1]
    states = [s_ref[h] for h in range(RET_HEADS)]
    gated = []
    for sub in range(q_ref.shape[0] // blk):
        rows = slice(sub * blk, (sub + 1) * blk)
        heads = []
        for h in range(RET_HEADS):
            qh = q_ref[rows, h * RET_DK:(h + 1) * RET_DK]
            kh = k_ref[rows, h * RET_DK:(h + 1) * RET_DK]
            vh = v_ref[rows, h * RET_DV:(h + 1) * RET_DV]
            scores = lax.dot_general(qh, kh, (((1,), (1,)), ((), ())), preferred_element_type=F32) * dm_ref[h]
            o = _bdot(scores.astype(BF16), vh) + _bdot(qh, states[h].astype(BF16)) * xi_ref[h]
            o = o * lax.rsqrt(jnp.mean(o * o, axis=-1, keepdims=True) + EPS)
            gh = g_ref[rows, h * RET_DV:(h + 1) * RET_DV].astype(F32)
            heads.append((o * (gh * _sigmoid(gh))).astype(BF16))
            kz = (kh.astype(F32) * zeta_ref[h]).astype(BF16)
            kv = lax.dot_general(kz, vh, (((0,), (0,)), ((), ())), preferred_element_type=F32)
            states[h] = block_decay[h] * states[h] + kv
        gated.append(jnp.concatenate(heads, axis=1))
    for h in range(RET_HEADS):
        s_ref[h] = states[h]
    yb_ref[...] = _bdot(jnp.concatenate(gated, axis=0), wo_ref[...]).astype(BF16)

    @pl.when(si == pl.num_programs(1) - 1)
    def _():
        st_ref[...] = s_ref[...]


def _ret_tables(seq):
    cl = min(CHUNK, seq)
    blk = min(RET_BLOCK, seq)
    log_g = jnp.log(1.0 - 2.0 ** (-5.0 - jnp.arange(RET_HEADS, dtype=F32)))
    idx = jnp.arange(blk, dtype=F32)
    diff = idx[:, None] - idx[None, :]
    cn = jnp.arange(blk)[:, None] // cl
    cm = jnp.arange(blk)[None, :] // cl
    expo = jnp.where(cm == cn, jnp.abs(diff), diff)
    dm = jnp.where(cm <= cn, jnp.exp(log_g[:, None, None] * expo[None]), 0.0)
    xi = jnp.exp(log_g[:, None] * (idx + 1.0)[None, :])[..., None]
    zeta = jnp.exp(log_g[:, None] * (blk - 1.0 - idx)[None, :])[..., None]
    block_decay = tuple(math.exp(math.log(1.0 - 2.0 ** (-5.0 - h)) * blk) for h in range(RET_HEADS))
    return blk, dm, xi, zeta, block_decay


def _retention(q, k, v, g, s0, w_ret_out_b):
    bsz, seq, _ = q.shape
    blk, dm, xi, zeta, block_decay = _ret_tables(seq)
    step = min(RET_BLOCKS_PER_STEP * blk, seq)
    row = lambda w: pl.BlockSpec((None, step, w), lambda b, s: (b, s, 0))
    const = lambda shape: pl.BlockSpec(shape, lambda b, s: (0,) * len(shape))
    state = pl.BlockSpec((None, RET_HEADS, RET_DK, RET_DV), lambda b, s: (b, 0, 0, 0))
    carried = s0 is not None
    return pl.pallas_call(
        functools.partial(_ret_kernel, block_decay=block_decay, carried=carried),
        grid=(bsz, seq // step),
        in_specs=[row(RET_QK), row(RET_QK), row(RET_V), row(RET_V)] + [state] * carried
        + [const(dm.shape), const(xi.shape), const(zeta.shape), const((RET_V, D_MODEL))],
        out_specs=[row(D_MODEL), state],
        out_shape=[jax.ShapeDtypeStruct((bsz, seq, D_MODEL), BF16),
                   jax.ShapeDtypeStruct((bsz, RET_HEADS, RET_DK, RET_DV), F32)],
        scratch_shapes=[pltpu.VMEM((RET_HEADS, RET_DK, RET_DV), F32)],
        compiler_params=_cparams("parallel", "arbitrary"),
        name="retention",
    )(q, k, v, g, *([s0] * carried), dm, xi, zeta, w_ret_out_b)


def _to_row_tiles(val):
    n, w = val.shape
    return val.reshape(n * (w // LANE), LANE)


def _from_row_tiles(tiles, w):
    return tiles.reshape(tiles.shape[0] // (w // LANE), w)


def _merge_kernel(x_ref, ya_ref, yb_ref, ga_ref, gb_ref, gt_ref, sh_ref, sc_ref, g2_ref, wo_ref,
                  wr_ref, br_ref, x1_ref, n2_ref, re_ref, wc_ref, cnt_ref):
    tl = x_ref.shape[0]
    part = tl // MERGE_PARTS if tl % (MERGE_PARTS * 16) == 0 else tl
    nt_dot = lambda a, b: lax.dot_general(a, b, (((1,), (1,)), ((), ())), preferred_element_type=F32)
    logits = []
    for p in range(tl // part):
        rows = slice(p * part, (p + 1) * part)
        vec = lambda ref: ref[...] if ref.shape[0] == 1 else ref[rows, :]
        merged = _sigmoid(ga_ref[rows, :]) * ya_ref[rows, :] + _sigmoid(gb_ref[rows, :]) * yb_ref[rows, :]
        x1 = x_ref[rows, :] + vec(gt_ref) * _bdot(merged, wo_ref[...])
        x1_ref[rows, :] = x1.astype(BF16)
        n2 = x1 * lax.rsqrt(jnp.mean(x1 * x1, axis=-1, keepdims=True) + EPS) * g2_ref[...]
        n2 = n2 * (1.0 + vec(sc_ref)) + vec(sh_ref)
        for j in range(ROW_TILES):
            n2_ref[pl.ds(p * part * ROW_TILES + j, part, stride=ROW_TILES), :] = n2[:, j * LANE:(j + 1) * LANE]
        n2_hi = n2.astype(BF16)
        n2_lo = (n2 - n2_hi.astype(F32)).astype(BF16)
        logits.append(nt_dot(wr_ref[0], n2_hi) + nt_dot(wr_ref[0], n2_lo) + nt_dot(wr_ref[1], n2_hi))
    lt = jnp.concatenate(logits, axis=1) + br_ref[...]
    iota = lax.broadcasted_iota(jnp.int32, (8, tl), 0)
    gl = lt[0:8]
    gmax = jnp.max(gl, axis=0, keepdims=True)
    gi = jnp.min(jnp.where(gl == gmax, iota, 8), axis=0, keepdims=True)
    gw = 1.0 / jnp.sum(jnp.exp(gl - gmax), axis=0, keepdims=True)
    el = jnp.zeros((8, tl), F32)
    for g in range(MOE_GROUPS):
        el = jnp.where(gi == g, lt[8 * (g + 1):8 * (g + 2)], el)
    m1 = jnp.max(el, axis=0, keepdims=True)
    i1 = jnp.min(jnp.where(el == m1, iota, 8), axis=0, keepdims=True)
    el2 = jnp.where(iota == i1, -jnp.inf, el)
    m2 = jnp.max(el2, axis=0, keepdims=True)
    i2 = jnp.min(jnp.where(el2 == m2, iota, 8), axis=0, keepdims=True)
    e21 = jnp.exp(m2 - m1)
    w1 = gw / (1.0 + e21)
    w2 = w1 * e21
    lo = jnp.minimum(i1, i2)
    hi = jnp.maximum(i1, i2)
    pair = gi * PAIRS_PER_GROUP + lo * (MOE_EXPERTS - 1) - ((lo * (lo - 1)) >> 1) + (hi - lo - 1)
    re_ref[...] = jnp.where(iota == 0, pair, 0)
    rw = jnp.where(iota == 0, jnp.where(i1 < i2, w1, w2), jnp.where(iota == 1, jnp.where(i1 < i2, w2, w1), 0.0))
    eye = (lax.broadcasted_iota(jnp.int32, (8, LANE), 0)
           == lax.broadcasted_iota(jnp.int32, (8, LANE), 1)).astype(F32)
    wc_ref[...] = lax.dot_general(rw, eye, (((0,), (0,)), ((), ())),
                                  precision=lax.Precision.HIGHEST, preferred_element_type=F32)
    ids = lax.broadcasted_iota(jnp.int32, (N_PAIRS, tl), 0)
    hits = jnp.where(ids == pair, 1.0, 0.0)
    cnt_ref[...] = jnp.broadcast_to(jnp.sum(hits, axis=1, keepdims=True), (N_PAIRS, LANE)).astype(jnp.int32)


def _merge(x, ya, yb, ga, gb, gate1, shift2, scale2, g2, w_out_b, wr, br):
    bsz, seq, _ = x.shape
    tl = min(TOKEN_TILE, seq)
    row = pl.BlockSpec((None, tl, D_MODEL), lambda b, t: (b, t, 0))
    vec = _mod_spec(gate1, tl)
    const = lambda shape: pl.BlockSpec(shape, lambda b, t: (0,) * len(shape))
    route = pl.BlockSpec((None, 8, tl), lambda b, t: (b, 0, t))
    nt = seq // tl
    tiles = pl.BlockSpec((tl * ROW_TILES, LANE), lambda b, t: (b * nt + t, 0))
    return pl.pallas_call(
        _merge_kernel,
        grid=(bsz, nt),
        in_specs=[row, row, row, row, row, vec, vec, vec, const((1, D_MODEL)),
                  const((D_MODEL, D_MODEL)), const((2, ROUTE_ROWS, D_MODEL)), const((ROUTE_ROWS, 1))],
        out_specs=[row, tiles, route,
                   pl.BlockSpec((tl, LANE), lambda b, t: (b * nt + t, 0)),
                   pl.BlockSpec((None, N_PAIRS, LANE), lambda b, t: (b * nt + t, 0, 0))],
        out_shape=[jax.ShapeDtypeStruct((bsz, seq, D_MODEL), BF16),
                   jax.ShapeDtypeStruct((bsz * seq * ROW_TILES, LANE), F32),
                   jax.ShapeDtypeStruct((bsz, 8, seq), jnp.int32),
                   jax.ShapeDtypeStruct((bsz * seq, LANE), F32),
                   jax.ShapeDtypeStruct((bsz * nt, N_PAIRS, LANE), jnp.int32)],
        compiler_params=_cparams("parallel", "parallel"),
        name="merge",
    )(x, ya, yb, ga, gb, gate1, shift2, scale2, g2, w_out_b, wr, br)


def _router_weights(w_rg, b_rg, w_re, b_re):
    wr = jnp.zeros((ROUTE_ROWS, D_MODEL), F32)
    wr = wr.at[0:MOE_GROUPS].set(w_rg.astype(F32).T)
    wr = wr.at[8:].set(jnp.transpose(w_re.astype(F32), (0, 2, 1)).reshape(N_EXPERTS, D_MODEL))
    br = jnp.full((ROUTE_ROWS,), -1e30, F32)
    br = br.at[0:MOE_GROUPS].set(b_rg.astype(F32))
    br = br.at[8:].set(b_re.astype(F32).reshape(N_EXPERTS))
    wr_hi = wr.astype(BF16)
    wr_lo = (wr - wr_hi.astype(F32)).astype(BF16)
    return jnp.stack([wr_hi, wr_lo]), br.reshape(ROUTE_ROWS, 1)


def _plan_kernel(e_ref, base_ref, c0_ref, tri_ref, pos_ref, carry_ref):
    @pl.when(pl.program_id(0) == 0)
    def _():
        carry_ref[...] = c0_ref[...]

    tb = e_ref.shape[1]
    hit = lax.broadcasted_iota(jnp.int32, (N_PAIRS, tb), 0) == e_ref[0:1, :]
    incl = _bdot(jnp.where(hit, 1.0, 0.0).astype(BF16), tri_ref[...])
    row = jnp.where(hit, incl - 1.0 + carry_ref[...] + base_ref[...], 0.0)
    pos_ref[...] = jnp.sum(row, axis=0, keepdims=True).astype(jnp.int32)
    carry_ref[...] = carry_ref[...] + incl[:, tb - 1:tb]


def _plan(route_e, pad_start, placed):
    bsz, _, seq = route_e.shape
    tb = min(ROUTE_TILE, seq)
    nb = seq // tb
    r = jnp.arange(tb)
    tri = (r[:, None] <= r[None, :]).astype(BF16)
    col = pl.BlockSpec((N_PAIRS, 1), lambda i: (0, 0))
    return pl.pallas_call(
        _plan_kernel,
        grid=(bsz * nb,),
        in_specs=[pl.BlockSpec((None, 8, tb), lambda i: (i // nb, 0, i % nb)), col, col,
                  pl.BlockSpec((tb, tb), lambda i: (0, 0))],
        out_specs=pl.BlockSpec((None, 1, tb), lambda i: (i, 0, 0)),
        out_shape=jax.ShapeDtypeStruct((bsz * nb, 1, tb), jnp.int32),
        scratch_shapes=[pltpu.VMEM((N_PAIRS, 1), F32)],
        compiler_params=_cparams("arbitrary"),
        name="plan",
    )(route_e, pad_start.astype(F32).reshape(N_PAIRS, 1), placed.astype(F32).reshape(N_PAIRS, 1), tri)


ZERO_CHUNKS = tuple(2 ** b for b in range(8, -1, -1))


def _dispatch_kernel(lo_ref, hi_ref, nu_ref, posa_ref, xa_ref, posb_ref, xb_ref, xs_hbm, zero_ref, sem,
                     *, tb, nb_rows, tm, n_tiles):
    last = pl.num_programs(0) - 1

    def scatter(pos_ref, x_ref, n):
        def body(pair, carry):
            for queue in range(DMA_QUEUES):
                r = pair * DMA_QUEUES + queue
                src = pl.multiple_of(r * ROW_TILES, ROW_TILES)
                dst = pl.multiple_of(pos_ref[0, r] * ROW_TILES, ROW_TILES)
                pltpu.make_async_copy(x_ref.at[pl.ds(src, ROW_TILES), :],
                                      xs_hbm.at[pl.ds(dst, ROW_TILES), :], sem.at[0]).start(priority=queue)
            return carry
        lax.fori_loop(0, n // DMA_QUEUES, body, 0, unroll=DMA_UNROLL // DMA_QUEUES)

    def scatter_wait(x_ref, n):
        pltpu.make_async_copy(x_ref, xs_hbm.at[pl.ds(0, n * ROW_TILES), :], sem.at[0]).wait()

    def zero_copy(row, size):
        dst = pl.multiple_of(row * ROW_TILES, ROW_TILES)
        return pltpu.make_async_copy(zero_ref.at[pl.ds(0, size * ROW_TILES), :],
                                     xs_hbm.at[pl.ds(dst, size * ROW_TILES), :], sem.at[1])

    @pl.when(pl.program_id(0) < last)
    def _():
        scatter(posa_ref, xa_ref, tb)
        scatter_wait(xa_ref, tb)

    @pl.when(pl.program_id(0) == last)
    def _():
        scatter(posb_ref, xb_ref, nb_rows)
        zero_ref[...] = jnp.zeros_like(zero_ref)

        def per_pair(e, carry):
            lo = lo_ref[e]
            n = hi_ref[e] - lo
            for wait in (False, True):
                row = lo
                for size in ZERO_CHUNKS:
                    @pl.when((n & size) != 0)
                    def _(row=row, size=size, wait=wait):
                        zero_copy(row, size).wait() if wait else zero_copy(row, size).start()
                    row = row + (n & size)
            return carry
        lax.fori_loop(0, N_PAIRS, per_pair, 0)

        chunk = min(ZERO_CHUNKS[0], tm)
        for wait in (False, True):
            def per_tile(t, carry, wait=wait):
                for c in range(tm // chunk):
                    cp = zero_copy(t * tm + c * chunk, chunk)
                    cp.wait() if wait else cp.start()
                return carry
            lax.fori_loop(nu_ref[0], n_tiles, per_tile, 0)
        scatter_wait(xb_ref, nb_rows)


def _dispatch(pos_a, rows_a, pos_b, rows_b, pad_lo, pad_hi, n_used, n_rows, tm):
    tb = min(DISPATCH_TILE, pos_a.size)
    na = pos_a.size // tb
    pos_a = pos_a.reshape(na, 1, tb)
    nb_rows = rows_b.shape[0] // ROW_TILES
    pos_b = pos_b.reshape(1, 1, nb_rows)
    block_a = lambda i, lo, hi, nu: (jnp.minimum(i, na - 1), 0, 0)
    grid_spec = pltpu.PrefetchScalarGridSpec(
        num_scalar_prefetch=3,
        grid=(na + 1,),
        in_specs=[pl.BlockSpec((None, 1, tb), block_a, memory_space=pltpu.SMEM),
                  pl.BlockSpec((tb * ROW_TILES, LANE), lambda i, lo, hi, nu: (jnp.minimum(i, na - 1), 0)),
                  pl.BlockSpec((None, 1, nb_rows), lambda i, lo, hi, nu: (0, 0, 0), memory_space=pltpu.SMEM),
                  pl.BlockSpec((nb_rows * ROW_TILES, LANE), lambda i, lo, hi, nu: (0, 0))],
        out_specs=pl.BlockSpec(memory_space=pl.ANY),
        scratch_shapes=[pltpu.VMEM((ZERO_CHUNKS[0] * ROW_TILES, LANE), F32), pltpu.SemaphoreType.DMA((2,))],
    )
    return pl.pallas_call(
        functools.partial(_dispatch_kernel, tb=tb, nb_rows=nb_rows, tm=tm, n_tiles=n_rows // tm),
        grid_spec=grid_spec,
        out_shape=jax.ShapeDtypeStruct((n_rows * ROW_TILES, LANE), F32),
        compiler_params=_cparams("arbitrary"),
        name="dispatch",
    )(pad_lo, pad_hi, n_used, pos_a, rows_a, pos_b, rows_b)


def _expert_kernel(ta_ref, tb_ref, nu_ref, x_ref, w1a_ref, w3a_ref, w2a_ref, w1b_ref, w3b_ref, w2b_ref,
                   y_ref, *, tm):
    @pl.when(pl.program_id(0) < nu_ref[0])
    def _():
        xb = _from_row_tiles(x_ref[...], D_MODEL).astype(BF16)
        ys = []
        for w1_ref, w3_ref, w2_ref in ((w1a_ref, w3a_ref, w2a_ref), (w1b_ref, w3b_ref, w2b_ref)):
            a = _bdot(xb, w1_ref[...])
            h = a * _sigmoid(a) * _bdot(xb, w3_ref[...])
            ys.append(_bdot(h.astype(BF16), w2_ref[...]))
        y_ref[...] = _to_row_tiles(jnp.concatenate(ys, axis=1))

    @pl.when(pl.program_id(0) >= nu_ref[0])
    def _():
        y_ref[...] = jnp.zeros_like(y_ref)


def _experts(xs, tile_a, tile_b, n_used, w1b, w3b, w2b, tm):
    n_tiles = xs.shape[0] // (tm * ROW_TILES)
    tile = lambda i, ta, tb, nu: (jnp.maximum(jnp.minimum(i, nu[0] - 1), 0), 0)
    first = lambda i, ta, tb, nu: (ta[i], 0, 0)
    second = lambda i, ta, tb, nu: (tb[i], 0, 0)
    up = lambda which: pl.BlockSpec((None, D_MODEL, MOE_FF), which)
    down = lambda which: pl.BlockSpec((None, MOE_FF, D_MODEL), which)
    grid_spec = pltpu.PrefetchScalarGridSpec(
        num_scalar_prefetch=3,
        grid=(n_tiles,),
        in_specs=[pl.BlockSpec((tm * ROW_TILES, LANE), tile),
                  up(first), up(first), down(first), up(second), up(second), down(second)],
        out_specs=pl.BlockSpec((tm * 2 * ROW_TILES, LANE), lambda i, ta, tb, nu: (i, 0)),
    )
    return pl.pallas_call(
        functools.partial(_expert_kernel, tm=tm),
        grid_spec=grid_spec,
        out_shape=jax.ShapeDtypeStruct((2 * xs.shape[0], LANE), F32),
        compiler_params=_cparams("arbitrary"),
        name="experts",
    )(tile_a, tile_b, n_used, xs, w1b, w3b, w2b, w1b, w3b, w2b)


def _expert_layout(counts, tm, n_tok):
    n_tiles = n_tok // tm + MOE_GROUPS * PAIRS_PER_GROUP
    padded = ((counts + tm - 1) // tm) * tm
    pad_end = jnp.cumsum(padded)
    pad_start = pad_end - padded
    n_used = pad_end[-1] // tm
    tile_start = jnp.minimum(jnp.arange(n_tiles, dtype=jnp.int32), n_used - 1) * tm
    tile_pair = jnp.sum((pad_end[None, :] <= tile_start[:, None]).astype(jnp.int32), axis=1)
    tile_pair = jnp.minimum(tile_pair, MOE_GROUPS * PAIRS_PER_GROUP - 1)
    members = [(lo, hi) for lo in range(MOE_EXPERTS) for hi in range(lo + 1, MOE_EXPERTS)]
    lo_of = jnp.asarray([m[0] for m in members], jnp.int32)
    hi_of = jnp.asarray([m[1] for m in members], jnp.int32)
    group = tile_pair // PAIRS_PER_GROUP
    tile_a = group * MOE_EXPERTS + lo_of[tile_pair % PAIRS_PER_GROUP]
    tile_b = group * MOE_EXPERTS + hi_of[tile_pair % PAIRS_PER_GROUP]
    return (pad_start, pad_start + counts, pad_end, tile_a.astype(jnp.int32), tile_b.astype(jnp.int32),
            n_used.astype(jnp.int32).reshape(1), n_tiles * tm)


def _combine_kernel(cur_ref, nxt_ref, x1_ref, wc_ref, gt_ref, fn_ref, y_hbm, o_ref, buf0, buf1, sem, *, tl):
    step = pl.program_id(0)
    span = 2 * ROW_TILES

    def gather(idx_ref, first, buf, which):
        for r in range(tl):
            src = pl.multiple_of(idx_ref[0, first + r] * span, span)
            pltpu.make_async_copy(y_hbm.at[pl.ds(src, span), :], buf.at[pl.ds(r * span, span), :],
                                  sem.at[which]).start(priority=r % DMA_QUEUES)

    def gather_wait(buf, which):
        pltpu.make_async_copy(y_hbm.at[pl.ds(0, tl * span), :], buf, sem.at[which]).wait()

    def finish(buf, half):
        rows = slice(half * tl, (half + 1) * tl)
        both = _from_row_tiles(buf[...], 2 * D_MODEL)
        moe = wc_ref[rows, 0:1] * both[:, :D_MODEL] + wc_ref[rows, 1:2] * both[:, D_MODEL:]
        gate = gt_ref[...] if gt_ref.shape[0] == 1 else gt_ref[rows, :]
        x2 = x1_ref[rows, :].astype(F32) + gate * moe
        o_ref[rows, :] = x2 * lax.rsqrt(jnp.mean(x2 * x2, axis=-1, keepdims=True) + EPS) * fn_ref[...]

    @pl.when(step == 0)
    def _():
        gather(cur_ref, 0, buf0, 0)

    gather_wait(buf0, 0)
    gather(cur_ref, tl, buf1, 1)
    finish(buf0, 0)
    gather_wait(buf1, 1)
    gather(nxt_ref, 0, buf0, 0)
    finish(buf1, 1)

    @pl.when(step == pl.num_programs(0) - 1)
    def _():
        gather_wait(buf0, 0)


def _combine(x1_flat, pos, y_rows, wcol, gate2, final_norm, seq):
    n_tok = x1_flat.shape[0]
    tl = pos.shape[2] if pos.shape[0] % 2 == 0 else pos.shape[2] // 2
    n_steps = n_tok // (2 * tl)
    pos = pos.reshape(n_steps, 1, 2 * tl)
    last = n_steps - 1
    if gate2.shape[1] == 1:
        gate_spec = pl.BlockSpec((None, 1, D_MODEL), lambda i: ((i * 2 * tl) // seq, 0, 0))
    else:
        gate2 = gate2.reshape(n_tok, D_MODEL)
        gate_spec = pl.BlockSpec((2 * tl, D_MODEL), lambda i: (i, 0))
    return pl.pallas_call(
        functools.partial(_combine_kernel, tl=tl),
        grid=(n_steps,),
        in_specs=[pl.BlockSpec((None, 1, 2 * tl), lambda i: (i, 0, 0), memory_space=pltpu.SMEM),
                  pl.BlockSpec((None, 1, 2 * tl), lambda i: (jnp.minimum(i + 1, last), 0, 0),
                               memory_space=pltpu.SMEM),
                  pl.BlockSpec((2 * tl, D_MODEL), lambda i: (i, 0)),
                  pl.BlockSpec((2 * tl, LANE), lambda i: (i, 0)),
                  gate_spec,
                  pl.BlockSpec((1, D_MODEL), lambda i: (0, 0)),
                  pl.BlockSpec(memory_space=pl.ANY)],
        out_specs=pl.BlockSpec((2 * tl, D_MODEL), lambda i: (i, 0)),
        out_shape=jax.ShapeDtypeStruct((n_tok, D_MODEL), F32),
        scratch_shapes=[pltpu.VMEM((2 * tl * ROW_TILES, LANE), F32)] * 2 + [pltpu.SemaphoreType.DMA((2,))],
        compiler_params=_cparams("arbitrary"),
        name="combine",
    )(pos, pos, x1_flat, wcol, gate2, final_norm, y_rows)


def _rope_tables(seq, offset):
    half = RET_DK // 2
    theta = 1.0 / (ROPE_BASE ** jnp.linspace(0.0, 1.0, half, dtype=F32))
    pos = offset + jnp.arange(seq)
    ang = pos.astype(F32)[:, None] * theta[None, :]
    cos = jnp.cos(ang)
    sin = jnp.sin(ang)
    return jnp.concatenate([cos, cos], axis=1), jnp.concatenate([-sin, sin], axis=1)


def _mixers(x, mod, h0, s0, offset, p):
    bsz, seq, _ = x.shape
    mods = [m.reshape(bsz, 1, D_MODEL) for m in jnp.split(mod, 6, axis=-1)]
    cos2, sin2 = _rope_tables(seq, offset)
    flat = seq < TOKEN_TILE and bsz * seq <= TOKEN_TILE
    sets, rows = (1, bsz * seq) if flat else (bsz, seq)
    tok = lambda a: a.reshape(sets, rows, a.shape[-1])
    if flat:
        mods = [jnp.broadcast_to(m, (bsz, seq, D_MODEL)).reshape(1, rows, D_MODEL) for m in mods]
        cos2, sin2 = jnp.tile(cos2, (bsz, 1)), jnp.tile(sin2, (bsz, 1))
    sh1, sc1, gt1, sh2, sc2, gt2 = mods
    seqs = lambda a: a.reshape(bsz, seq, a.shape[-1])
    u, q, k, v, g, ga, gb = _inproj(tok(x), sh1, sc1, p['norm1'], p['w_in'], cos2, sin2)
    ya, h_t = _s5(seqs(u), h0, p['a_lanes'], p['bm'], p['cm'], p['d'], p['w_glu'], p['b_glu'], p['w_s5_out'])
    yb, s_t = _retention(seqs(q), seqs(k), seqs(v), seqs(g), s0, p['w_ret_out'])
    x1, n2, route_e, wcol, cnt = _merge(tok(x), tok(ya), tok(yb), ga, gb, gt1, sh2, sc2, p['norm2'],
                                        p['w_out'], p['wr'], p['br'])
    return dict(x1=x1.reshape(bsz * seq, D_MODEL), n2=n2, route=route_e, wcol=wcol,
                counts=jnp.sum(cnt[:, :, 0], axis=0), gate2=gt2, seq=rows, h=h_t, s=s_t)


def _moe(a, b, p, final_norm):
    n_tok = a['x1'].shape[0] + b['x1'].shape[0]
    pad_start, pad_lo, pad_hi, tile_a, tile_b, n_used, n_rows = _expert_layout(
        a['counts'] + b['counts'], EXPERT_TILE, n_tok)
    pos_a = _plan(a['route'], pad_start, jnp.zeros_like(a['counts']))
    pos_b = _plan(b['route'], pad_start, a['counts'])
    xs = _dispatch(pos_a, a['n2'], pos_b, b['n2'], pad_lo, pad_hi, n_used, n_rows, EXPERT_TILE)
    y_rows = _experts(xs, tile_a, tile_b, n_used, p['w1'], p['w3'], p['w2'], EXPERT_TILE)
    return [_combine(m['x1'], pos, y_rows, m['wcol'], m['gate2'], final_norm, m['seq'])
            for m, pos in ((a, pos_a), (b, pos_b))]


def kernel(x_prompt, x_sample, state_s5_re, state_s5_im, state_ret, c_prompt, c_sample, w_ada, b_ada, norm1, norm2, w_in, s5_a_re, s5_a_im, s5_log_dt, s5_b_re, s5_b_im, s5_c_re, s5_c_im, s5_d, s5_w_glu, s5_b_glu, w_s5_out, w_ret_out, w_out, w_rg, b_rg, w_re, b_re, w1, w3, w2, final_norm):
    depth = w_ada.shape[0]
    assert depth == 1
    bp = x_prompt.shape[0]
    bs, seq_s, _ = x_sample.shape
    l = 0
    a_lanes, bm, cm = _s5_params(s5_a_re[l], s5_a_im[l], s5_log_dt[l], s5_b_re[l], s5_b_im[l],
                                 s5_c_re[l], s5_c_im[l])
    wr, br = _router_weights(w_rg[l], b_rg[l], w_re[l], b_re[l])
    p = dict(
        norm1=norm1[l].astype(F32).reshape(1, D_MODEL), norm2=norm2[l].astype(F32).reshape(1, D_MODEL),
        w_in=w_in[l].astype(BF16), a_lanes=a_lanes, bm=bm, cm=cm,
        d=s5_d[l].astype(F32).reshape(1, S5_WIDTH), w_glu=s5_w_glu[l].astype(BF16),
        b_glu=s5_b_glu[l].astype(F32).reshape(1, S5_WIDTH), w_s5_out=w_s5_out[l].astype(BF16),
        w_ret_out=w_ret_out[l].astype(BF16), w_out=w_out[l].astype(BF16), wr=wr, br=br,
        w1=w1[l].astype(BF16).reshape(N_EXPERTS, D_MODEL, MOE_FF),
        w3=w3[l].astype(BF16).reshape(N_EXPERTS, D_MODEL, MOE_FF),
        w2=w2[l].astype(BF16).reshape(N_EXPERTS, MOE_FF, D_MODEL))
    fn = final_norm.astype(F32).reshape(1, D_MODEL)
    mod = _mod(jnp.concatenate([c_prompt, c_sample], axis=0).astype(F32), w_ada[l], b_ada[l])

    h0_p = jnp.zeros((bp, S5_LANES), F32)
    prompt = _mixers(x_prompt, mod[:bp], h0_p, None, 0, p)
    h0_s = _s5_state_to_lanes(state_s5_re[l], state_s5_im[l])
    sample = _mixers(x_sample, mod[bp:], h0_s, state_ret[l].astype(F32), PAST_LEN, p)
    y_p, y_s = _moe(prompt, sample, p, fn)
    p_re, p_im = _s5_state_from_lanes(prompt['h'])
    s_re, s_im = _s5_state_from_lanes(sample['h'])
    return (y_p.reshape(x_prompt.shape), y_s.reshape(x_sample.shape), p_re[None], p_im[None],
            prompt['s'][None], s_re[None], s_im[None], sample['s'][None])
```

```python
import functools
import math

import jax
import jax.numpy as jnp
from jax import lax
from jax.experimental import pallas as pl
from jax.experimental.pallas import tpu as pltpu

F32 = jnp.float32
BF16 = jnp.bfloat16

D_MODEL = 1024
PAST_LEN = 2048
CHUNK = 64
S5_WIDTH = 512
S5_GROUP = 16
S5_GROUPS = 32
S5_STATE = 64
S5_LANES = 2 * S5_GROUPS * S5_STATE
S5_CHUNKS = 4
RET_HEADS = 4
RET_DK = 128
RET_DV = 256
RET_QK = RET_HEADS * RET_DK
RET_V = RET_HEADS * RET_DV
ROPE_BASE = 10000.0
MOE_GROUPS = 4
MOE_EXPERTS = 8
N_EXPERTS = MOE_GROUPS * MOE_EXPERTS
PAIRS_PER_GROUP = MOE_EXPERTS * (MOE_EXPERTS - 1) // 2
N_PAIRS = 128
RANK_BITS = 17
RANK_SPAN = 1 << RANK_BITS
MOE_FF = 256
EPS = 1e-6
IN_WIDTH = S5_WIDTH + 2 * RET_QK + 2 * RET_V + 2 * D_MODEL
ROUTE_ROWS = 8 * (1 + MOE_GROUPS)

BATCH_GROUP = 8
S5_GROUPS_PER_STEP = 2
TOKEN_TILE = 1024
MERGE_PARTS = 2
INPROJ_TILE = 512
S5_TIME_TILE = 64
RET_BLOCK = 256
RET_BLOCKS_PER_STEP = 4
EXPERT_TILE = 256
ROUTE_TILE = 512
DISPATCH_TILE = 2048
VMEM_LIMIT = 56 * 1024 * 1024
LANE = 128
SUBLANE = 8
ROW_TILES = D_MODEL // LANE
DMA_UNROLL = 8
DMA_QUEUES = 2


def _cparams(*sem):
    return pltpu.CompilerParams(dimension_semantics=sem, vmem_limit_bytes=VMEM_LIMIT)


def _bdot(a, b):
    return jnp.dot(a, b, preferred_element_type=F32)


def _sigmoid(x):
    return 0.5 * jnp.tanh(0.5 * x) + 0.5


def _mod_kernel(c_ref, w_ref, b_ref, o_ref):
    c = c_ref[...]
    a = (c * _sigmoid(c)).astype(BF16)
    o_ref[...] = _bdot(a, w_ref[...].astype(BF16)) + b_ref[...]


def _mod(c, w_ada, b_ada):
    n = c.shape[0]
    return pl.pallas_call(
        _mod_kernel,
        grid=(6,),
        in_specs=[pl.BlockSpec((n, D_MODEL), lambda j: (0, 0)),
                  pl.BlockSpec((D_MODEL, D_MODEL), lambda j: (0, j)),
                  pl.BlockSpec((1, D_MODEL), lambda j: (0, j))],
        out_specs=pl.BlockSpec((n, D_MODEL), lambda j: (0, j)),
        out_shape=jax.ShapeDtypeStruct((n, 6 * D_MODEL), F32),
        compiler_params=_cparams("parallel"),
        name="mod",
    )(c, w_ada, b_ada.reshape(1, -1))


def _rope(x, cos2, sin2):
    return x * cos2 + pltpu.roll(x, RET_DK // 2, 1) * sin2


def _inproj_kernel(x_ref, sh_ref, sc_ref, g_ref, w_ref, cos_ref, sin_ref,
                   u_ref, q_ref, k_ref, v_ref, gs_ref, ga_ref, gb_ref):
    x = x_ref[...]
    n = x * lax.rsqrt(jnp.mean(x * x, axis=-1, keepdims=True) + EPS) * g_ref[...]
    nb = (n * (1.0 + sc_ref[...]) + sh_ref[...]).astype(BF16)
    cos2 = cos_ref[...]
    sin2 = sin_ref[...]
    o = 0
    u_ref[...] = _bdot(nb, w_ref[:, o:o + S5_WIDTH]).astype(BF16)
    o += S5_WIDTH
    q = _bdot(nb, w_ref[:, o:o + RET_QK])
    for h in range(RET_HEADS):
        head = slice(h * RET_DK, (h + 1) * RET_DK)
        q_ref[:, head] = _rope(q[:, head], cos2, sin2).astype(BF16)
    o += RET_QK
    k = _bdot(nb, w_ref[:, o:o + RET_QK])
    for h in range(RET_HEADS):
        head = slice(h * RET_DK, (h + 1) * RET_DK)
        k_ref[:, head] = (_rope(k[:, head], cos2, sin2) * (RET_DK ** -0.5)).astype(BF16)
    o += RET_QK
    for ref in (v_ref, gs_ref, ga_ref, gb_ref):
        ref[...] = _bdot(nb, w_ref[:, o:o + D_MODEL]).astype(BF16)
        o += D_MODEL


def _mod_spec(vec, tl):
    if vec.shape[1] == 1:
        return pl.BlockSpec((None, 1, D_MODEL), lambda b, t: (b, 0, 0))
    return pl.BlockSpec((None, tl, D_MODEL), lambda b, t: (b, t, 0))


def _inproj(x, shift, scale, g1, w_in_b, cos2, sin2):
    bsz, seq, _ = x.shape
    tl = min(INPROJ_TILE, seq)
    row = lambda w: pl.BlockSpec((None, tl, w), lambda b, t: (b, t, 0))
    vec = _mod_spec(shift, tl)
    shapes = [S5_WIDTH, RET_QK, RET_QK, RET_V, RET_V, D_MODEL, D_MODEL]
    return pl.pallas_call(
        _inproj_kernel,
        grid=(bsz, seq // tl),
        in_specs=[row(D_MODEL), vec, vec,
                  pl.BlockSpec((1, D_MODEL), lambda b, t: (0, 0)),
                  pl.BlockSpec((D_MODEL, IN_WIDTH), lambda b, t: (0, 0), pipeline_mode=pl.Buffered(1)),
                  pl.BlockSpec((tl, RET_DK), lambda b, t: (t, 0)),
                  pl.BlockSpec((tl, RET_DK), lambda b, t: (t, 0))],
        out_specs=[row(w) for w in shapes],
        out_shape=[jax.ShapeDtypeStruct((bsz, seq, w), BF16) for w in shapes],
        compiler_params=_cparams("parallel", "parallel"),
        name="inproj",
    )(x, shift, scale, g1, w_in_b, cos2, sin2)


def _gelu_tanh(y):
    return 0.5 * y * (1.0 + jnp.tanh(math.sqrt(2.0 / math.pi) * (y + 0.044715 * (y * y * y))))


def _s5_kernel(u_ref, h0_ref, a_ref, pm_ref, pt_ref, bm_ref, cm_ref, d_ref, wg_ref, bg_ref, wo_ref,
               ya_ref, ht_ref, hs_ref, *bu_refs, tt):
    ti = pl.program_id(1)
    rows = BATCH_GROUP * tt
    half = S5_LANES // (2 * S5_CHUNKS)

    @pl.when(ti == 0)
    def _():
        hs_ref[...] = h0_ref[...]

    kc = S5_WIDTH // S5_CHUNKS
    n_grp = u_ref.shape[0] // BATCH_GROUP
    streams = lambda g: slice(g * BATCH_GROUP, (g + 1) * BATCH_GROUP)
    u2, ys = {}, {g: [] for g in range(n_grp)}

    def permute(g):
        u2[g] = _bdot(pm_ref[...], u_ref[streams(g)].reshape(rows, S5_WIDTH)).astype(BF16)

    def input_map(g, c):
        bu_refs[g * S5_CHUNKS + c][...] = _bdot(u2[g][:, c * kc:(c + 1) * kc], bm_ref[c])

    def recurrence(g, c):
        bu_ref = bu_refs[g * S5_CHUNKS + c]
        lre = slice(c * 2 * half, c * 2 * half + half)
        lim = slice(c * 2 * half + half, (c + 1) * 2 * half)
        are = a_ref[:, lre]
        aim = a_ref[:, lim]
        hre = hs_ref[streams(g), lre]
        him = hs_ref[streams(g), lim]
        for t in range(tt):
            rsel = slice(t * BATCH_GROUP, (t + 1) * BATCH_GROUP)
            hre, him = (are * hre - aim * him + bu_ref[rsel, :half],
                        are * him + aim * hre + bu_ref[rsel, half:])
            bu_ref[rsel, :half] = hre
            bu_ref[rsel, half:] = him
        hs_ref[streams(g), lre] = hre
        hs_ref[streams(g), lim] = him

    def output_map(g, c):
        ys[g].append(_bdot(bu_refs[g * S5_CHUNKS + c][...].astype(BF16), cm_ref[c]))

    def tail(g):
        y = jnp.concatenate(ys[g], axis=1) + d_ref[...] * u2[g].astype(F32)
        z = _gelu_tanh(y)
        gl = _bdot(z.astype(BF16), wg_ref[...]) + bg_ref[...]
        o = (z * _sigmoid(gl)).astype(BF16)
        ob = _bdot(pt_ref[...], o).astype(BF16)
        ya_ref[streams(g)] = _bdot(ob, wo_ref[...]).reshape(BATCH_GROUP, tt, D_MODEL).astype(BF16)

    permute(0)
    for c in range(S5_CHUNKS):
        input_map(0, c)
    for g in range(n_grp + 1):
        if g + 1 < n_grp:
            permute(g + 1)
        for c in range(S5_CHUNKS):
            if g >= 1:
                output_map(g - 1, c)
            if g + 1 < n_grp:
                input_map(g + 1, c)
            if g < n_grp:
                recurrence(g, c)
        if g >= 1:
            tail(g - 1)

    @pl.when(ti == pl.num_programs(1) - 1)
    def _():
        ht_ref[...] = hs_ref[...]


def _s5(u, h0, a_lanes, bm, cm, d, wg, bg, wo):
    bsz, seq, _ = u.shape
    tt = min(S5_TIME_TILE, seq)
    rows = BATCH_GROUP * tt
    groups = S5_GROUPS_PER_STEP if bsz % (BATCH_GROUP * S5_GROUPS_PER_STEP) == 0 else 1
    nb = BATCH_GROUP * groups
    const = lambda shape: pl.BlockSpec(shape, lambda b, t: (0,) * len(shape))
    r = jnp.arange(rows)
    perm = ((r[:, None] % BATCH_GROUP) * tt + r[:, None] // BATCH_GROUP == r[None, :]).astype(BF16)
    return pl.pallas_call(
        functools.partial(_s5_kernel, tt=tt),
        grid=(bsz // nb, seq // tt),
        in_specs=[pl.BlockSpec((nb, tt, S5_WIDTH), lambda b, t: (b, t, 0)),
                  pl.BlockSpec((nb, S5_LANES), lambda b, t: (b, 0)),
                  const((BATCH_GROUP, S5_LANES)), const((rows, rows)), const((rows, rows)),
                  const(bm.shape), const(cm.shape), const((1, S5_WIDTH)),
                  const((S5_WIDTH, S5_WIDTH)), const((1, S5_WIDTH)), const((S5_WIDTH, D_MODEL))],
        out_specs=[pl.BlockSpec((nb, tt, D_MODEL), lambda b, t: (b, t, 0)),
                   pl.BlockSpec((nb, S5_LANES), lambda b, t: (b, 0))],
        out_shape=[jax.ShapeDtypeStruct((bsz, seq, D_MODEL), BF16),
                   jax.ShapeDtypeStruct((bsz, S5_LANES), F32)],
        scratch_shapes=[pltpu.VMEM((nb, S5_LANES), F32)]
        + [pltpu.VMEM((rows, S5_LANES // S5_CHUNKS), F32)] * (S5_CHUNKS * groups),
        compiler_params=_cparams("parallel", "arbitrary"),
        name="s5",
    )(u, h0, a_lanes, perm, perm.T, bm, cm, d, wg, bg, wo)


def _s5_params(a_re, a_im, log_dt, b_re, b_im, c_re, c_im):
    a_re = a_re.astype(F32)
    a_im = a_im.astype(F32)
    dt = jnp.exp(log_dt.astype(F32))[:, None]
    mag = jnp.exp(a_re * dt)
    ang = a_im * dt
    ab_re = mag * jnp.cos(ang)
    ab_im = mag * jnp.sin(ang)
    den = a_re * a_re + a_im * a_im
    nr = ab_re - 1.0
    ni = ab_im
    f_re = (nr * a_re + ni * a_im) / den
    f_im = (ni * a_re - nr * a_im) / den
    b_re = b_re.astype(F32)
    b_im = b_im.astype(F32)
    bb_re = f_re[..., None] * b_re - f_im[..., None] * b_im
    bb_im = f_re[..., None] * b_im + f_im[..., None] * b_re
    gpc = S5_GROUPS // S5_CHUNKS
    eye = jnp.eye(gpc, dtype=F32)

    def lanes(x):
        return x.reshape(S5_CHUNKS, gpc * S5_STATE)

    a_lanes = jnp.concatenate([lanes(ab_re), lanes(ab_im)], axis=1).reshape(1, S5_LANES)
    a_lanes = jnp.broadcast_to(a_lanes, (BATCH_GROUP, S5_LANES))

    def in_blocks(bb):
        bb = bb.reshape(S5_CHUNKS, gpc, S5_STATE, S5_GROUP)
        return jnp.einsum('cgpj,gh->cgjhp', bb, eye).reshape(S5_CHUNKS, gpc * S5_GROUP, gpc * S5_STATE)

    bm = jnp.concatenate([in_blocks(bb_re), in_blocks(bb_im)], axis=2).astype(BF16)

    def out_blocks(cc):
        cc = cc.astype(F32).reshape(S5_CHUNKS, gpc, S5_GROUP, S5_STATE)
        return jnp.einsum('cgjp,gh->cgphj', cc, eye).reshape(S5_CHUNKS, gpc * S5_STATE, gpc * S5_GROUP)

    cm = jnp.concatenate([out_blocks(c_re), -out_blocks(c_im)], axis=1).astype(BF16)
    return a_lanes, bm, cm


def _s5_state_to_lanes(h_re, h_im):
    bsz = h_re.shape[0]
    re = h_re.astype(F32).reshape(bsz, S5_CHUNKS, -1)
    im = h_im.astype(F32).reshape(bsz, S5_CHUNKS, -1)
    return jnp.concatenate([re, im], axis=2).reshape(bsz, S5_LANES)


def _s5_state_from_lanes(h):
    bsz = h.shape[0]
    h = h.reshape(bsz, S5_CHUNKS, 2, S5_GROUPS // S5_CHUNKS, S5_STATE)
    return (h[:, :, 0].reshape(bsz, S5_GROUPS, S5_STATE), h[:, :, 1].reshape(bsz, S5_GROUPS, S5_STATE))


def _ret_kernel(q_ref, k_ref, v_ref, g_ref, *rest, block_decay, carried):
    s0_ref = rest[0] if carried else None
    dm_ref, xi_ref, zeta_ref, wo_ref, yb_ref, st_ref, s_ref = rest[-7:]
    si = pl.program_id(1)

    @pl.when(si == 0)
    def _():
        s_ref[...] = s0_ref[...] if carried else jnp.zeros_like(s_ref)

    blk = dm_ref.shape[1]
    states = [s_ref[h] for h in range(RET_HEADS)]
    gated = []
    for sub in range(q_ref.shape[0] // blk):
        rows = slice(sub * blk, (sub + 1) * blk)
        heads = []
        for h in range(RET_HEADS):
            qh = q_ref[rows, h * RET_DK:(h + 1) * RET_DK]
            kh = k_ref[rows, h * RET_DK:(h + 1) * RET_DK]
            vh = v_ref[rows, h * RET_DV:(h + 1) * RET_DV]
            scores = lax.dot_general(qh, kh, (((1,), (1,)), ((), ())), preferred_element_type=F32) * dm_ref[h]
            o = _bdot(scores.astype(BF16), vh) + _bdot(qh, states[h].astype(BF16)) * xi_ref[h]
            o = o * lax.rsqrt(jnp.mean(o * o, axis=-1, keepdims=True) + EPS)
            gh = g_ref[rows, h * RET_DV:(h + 1) * RET_DV].astype(F32)
            heads.append((o * (gh * _sigmoid(gh))).astype(BF16))
            kz = (kh.astype(F32) * zeta_ref[h]).astype(BF16)
            kv = lax.dot_general(kz, vh, (((0,), (0,)), ((), ())), preferred_element_type=F32)
            states[h] = block_decay[h] * states[h] + kv
        gated.append(jnp.concatenate(heads, axis=1))
    for h in range(RET_HEADS):
        s_ref[h] = states[h]
    yb_ref[...] = _bdot(jnp.concatenate(gated, axis=0), wo_ref[...]).astype(BF16)

    @pl.when(si == pl.num_programs(1) - 1)
    def _():
        st_ref[...] = s_ref[...]


def _ret_tables(seq):
    cl = min(CHUNK, seq)
    blk = min(RET_BLOCK, seq)
    log_g = jnp.log(1.0 - 2.0 ** (-5.0 - jnp.arange(RET_HEADS, dtype=F32)))
    idx = jnp.arange(blk, dtype=F32)
    diff = idx[:, None] - idx[None, :]
    cn = jnp.arange(blk)[:, None] // cl
    cm = jnp.arange(blk)[None, :] // cl
    expo = jnp.where(cm == cn, jnp.abs(diff), diff)
    dm = jnp.where(cm <= cn, jnp.exp(log_g[:, None, None] * expo[None]), 0.0)
    xi = jnp.exp(log_g[:, None] * (idx + 1.0)[None, :])[..., None]
    zeta = jnp.exp(log_g[:, None] * (blk - 1.0 - idx)[None, :])[..., None]
    block_decay = tuple(math.exp(math.log(1.0 - 2.0 ** (-5.0 - h)) * blk) for h in range(RET_HEADS))
    return blk, dm, xi, zeta, block_decay


def _retention(q, k, v, g, s0, w_ret_out_b):
    bsz, seq, _ = q.shape
    blk, dm, xi, zeta, block_decay = _ret_tables(seq)
    step = min(RET_BLOCKS_PER_STEP * blk, seq)
    row = lambda w: pl.BlockSpec((None, step, w), lambda b, s: (b, s, 0))
    const = lambda shape: pl.BlockSpec(shape, lambda b, s: (0,) * len(shape))
    state = pl.BlockSpec((None, RET_HEADS, RET_DK, RET_DV), lambda b, s: (b, 0, 0, 0))
    carried = s0 is not None
    return pl.pallas_call(
        functools.partial(_ret_kernel, block_decay=block_decay, carried=carried),
        grid=(bsz, seq // step),
        in_specs=[row(RET_QK), row(RET_QK), row(RET_V), row(RET_V)] + [state] * carried
        + [const(dm.shape), const(xi.shape), const(zeta.shape), const((RET_V, D_MODEL))],
        out_specs=[row(D_MODEL), state],
        out_shape=[jax.ShapeDtypeStruct((bsz, seq, D_MODEL), BF16),
                   jax.ShapeDtypeStruct((bsz, RET_HEADS, RET_DK, RET_DV), F32)],
        scratch_shapes=[pltpu.VMEM((RET_HEADS, RET_DK, RET_DV), F32)],
        compiler_params=_cparams("parallel", "arbitrary"),
        name="retention",
    )(q, k, v, g, *([s0] * carried), dm, xi, zeta, w_ret_out_b)


def _to_row_tiles(val):
    n, w = val.shape
    return val.reshape(n * (w // LANE), LANE)


def _from_row_tiles(tiles, w):
    return tiles.reshape(tiles.shape[0] // (w // LANE), w)


def _merge_kernel(x_ref, ya_ref, yb_ref, ga_ref, gb_ref, gt_ref, sh_ref, sc_ref, g2_ref, wo_ref,
                  wr_ref, br_ref, placed_ref, tri_ref, x1_ref, n2_ref, re_ref, wc_ref, cnt_ref, carry_ref):
    tl = x_ref.shape[0]
    part = tl // MERGE_PARTS if tl % (MERGE_PARTS * 16) == 0 else tl
    nt_dot = lambda a, b: lax.dot_general(a, b, (((1,), (1,)), ((), ())), preferred_element_type=F32)
    logits = []
    for p in range(tl // part):
        rows = slice(p * part, (p + 1) * part)
        vec = lambda ref: ref[...] if ref.shape[0] == 1 else ref[rows, :]
        merged = _sigmoid(ga_ref[rows, :]) * ya_ref[rows, :] + _sigmoid(gb_ref[rows, :]) * yb_ref[rows, :]
        x1 = x_ref[rows, :] + vec(gt_ref) * _bdot(merged, wo_ref[...])
        x1_ref[rows, :] = x1.astype(BF16)
        n2 = x1 * lax.rsqrt(jnp.mean(x1 * x1, axis=-1, keepdims=True) + EPS) * g2_ref[...]
        n2 = n2 * (1.0 + vec(sc_ref)) + vec(sh_ref)
        for j in range(ROW_TILES):
            n2_ref[pl.ds(p * part * ROW_TILES + j, part, stride=ROW_TILES), :] = n2[:, j * LANE:(j + 1) * LANE]
        n2_hi = n2.astype(BF16)
        n2_lo = (n2 - n2_hi.astype(F32)).astype(BF16)
        logits.append(nt_dot(wr_ref[0], n2_hi) + nt_dot(wr_ref[0], n2_lo) + nt_dot(wr_ref[1], n2_hi))
    lt = jnp.concatenate(logits, axis=1) + br_ref[...]
    iota = lax.broadcasted_iota(jnp.int32, (8, tl), 0)
    gl = lt[0:8]
    gmax = jnp.max(gl, axis=0, keepdims=True)
    gi = jnp.min(jnp.where(gl == gmax, iota, 8), axis=0, keepdims=True)
    gw = 1.0 / jnp.sum(jnp.exp(gl - gmax), axis=0, keepdims=True)
    el = jnp.zeros((8, tl), F32)
    for g in range(MOE_GROUPS):
        el = jnp.where(gi == g, lt[8 * (g + 1):8 * (g + 2)], el)
    m1 = jnp.max(el, axis=0, keepdims=True)
    i1 = jnp.min(jnp.where(el == m1, iota, 8), axis=0, keepdims=True)
    el2 = jnp.where(iota == i1, -jnp.inf, el)
    m2 = jnp.max(el2, axis=0, keepdims=True)
    i2 = jnp.min(jnp.where(el2 == m2, iota, 8), axis=0, keepdims=True)
    e21 = jnp.exp(m2 - m1)
    w1 = gw / (1.0 + e21)
    w2 = w1 * e21
    lo = jnp.minimum(i1, i2)
    hi = jnp.maximum(i1, i2)
    pair = gi * PAIRS_PER_GROUP + lo * (MOE_EXPERTS - 1) - ((lo * (lo - 1)) >> 1) + (hi - lo - 1)
    @pl.when(jnp.logical_and(pl.program_id(0) == 0, pl.program_id(1) == 0))
    def _():
        carry_ref[...] = placed_ref[...]

    ids = lax.broadcasted_iota(jnp.int32, (N_PAIRS, tl), 0)
    hit = ids == pair
    ranks = []
    for p in range(tl // tri_ref.shape[0]):
        lanes = slice(p * tri_ref.shape[0], (p + 1) * tri_ref.shape[0])
        incl = _bdot(jnp.where(hit[:, lanes], 1.0, 0.0).astype(BF16), tri_ref[...])
        ranks.append(jnp.sum(jnp.where(hit[:, lanes], incl - 1.0 + carry_ref[...], 0.0), axis=0, keepdims=True))
        carry_ref[...] = carry_ref[...] + incl[:, -1:]
    rank = jnp.concatenate(ranks, axis=1).astype(jnp.int32)
    re_ref[...] = jnp.where(iota == 0, pair * RANK_SPAN + rank, 0)
    rw = jnp.where(iota == 0, jnp.where(i1 < i2, w1, w2), jnp.where(iota == 1, jnp.where(i1 < i2, w2, w1), 0.0))
    eye = (lax.broadcasted_iota(jnp.int32, (8, LANE), 0)
           == lax.broadcasted_iota(jnp.int32, (8, LANE), 1)).astype(F32)
    wc_ref[...] = lax.dot_general(rw, eye, (((0,), (0,)), ((), ())),
                                  precision=lax.Precision.HIGHEST, preferred_element_type=F32)
    hits = jnp.where(hit, 1.0, 0.0)
    cnt_ref[...] = jnp.broadcast_to(jnp.sum(hits, axis=1, keepdims=True), (N_PAIRS, LANE)).astype(jnp.int32)


def _merge(x, ya, yb, ga, gb, gate1, shift2, scale2, g2, w_out_b, wr, br, placed):
    bsz, seq, _ = x.shape
    tl = min(TOKEN_TILE, seq)
    tp = min(ROUTE_TILE, tl)
    r = jnp.arange(tp)
    tri = (r[:, None] <= r[None, :]).astype(BF16)
    row = pl.BlockSpec((None, tl, D_MODEL), lambda b, t: (b, t, 0))
    vec = _mod_spec(gate1, tl)
    const = lambda shape: pl.BlockSpec(shape, lambda b, t: (0,) * len(shape))
    route = pl.BlockSpec((None, 8, tl), lambda b, t: (b, 0, t))
    nt = seq // tl
    tiles = pl.BlockSpec((tl * ROW_TILES, LANE), lambda b, t: (b * nt + t, 0))
    return pl.pallas_call(
        _merge_kernel,
        grid=(bsz, nt),
        in_specs=[row, row, row, row, row, vec, vec, vec, const((1, D_MODEL)),
                  const((D_MODEL, D_MODEL)), const((2, ROUTE_ROWS, D_MODEL)), const((ROUTE_ROWS, 1)),
                  const((N_PAIRS, 1)), const((tp, tp))],
        out_specs=[row, tiles, route,
                   pl.BlockSpec((tl, LANE), lambda b, t: (b * nt + t, 0)),
                   pl.BlockSpec((None, N_PAIRS, LANE), lambda b, t: (b * nt + t, 0, 0))],
        out_shape=[jax.ShapeDtypeStruct((bsz, seq, D_MODEL), BF16),
                   jax.ShapeDtypeStruct((bsz * seq * ROW_TILES, LANE), F32),
                   jax.ShapeDtypeStruct((bsz, 8, seq), jnp.int32),
                   jax.ShapeDtypeStruct((bsz * seq, LANE), F32),
                   jax.ShapeDtypeStruct((bsz * nt, N_PAIRS, LANE), jnp.int32)],
        scratch_shapes=[pltpu.VMEM((N_PAIRS, 1), F32)],
        compiler_params=_cparams("arbitrary", "arbitrary"),
        name="merge",
    )(x, ya, yb, ga, gb, gate1, shift2, scale2, g2, w_out_b, wr, br,
      placed.astype(F32).reshape(N_PAIRS, 1), tri)


def _router_weights(w_rg, b_rg, w_re, b_re):
    wr = jnp.zeros((ROUTE_ROWS, D_MODEL), F32)
    wr = wr.at[0:MOE_GROUPS].set(w_rg.astype(F32).T)
    wr = wr.at[8:].set(jnp.transpose(w_re.astype(F32), (0, 2, 1)).reshape(N_EXPERTS, D_MODEL))
    br = jnp.full((ROUTE_ROWS,), -1e30, F32)
    br = br.at[0:MOE_GROUPS].set(b_rg.astype(F32))
    br = br.at[8:].set(b_re.astype(F32).reshape(N_EXPERTS))
    wr_hi = wr.astype(BF16)
    wr_lo = (wr - wr_hi.astype(F32)).astype(BF16)
    return jnp.stack([wr_hi, wr_lo]), br.reshape(ROUTE_ROWS, 1)


def _route_row(base_ref, code):
    return base_ref[code >> RANK_BITS] + (code & (RANK_SPAN - 1))


def _row_gather(base_ref, idx_ref, src_hbm, buf, sem, slot, n, span):
    def body(pair, carry):
        for queue in range(DMA_QUEUES):
            r = pair * DMA_QUEUES + queue
            src = pl.multiple_of(_route_row(base_ref, idx_ref[0, r]) * span, span)
            dst = pl.multiple_of(r * span, span)
            pltpu.make_async_copy(src_hbm.at[pl.ds(src, span), :],
                                  buf.at[slot, pl.ds(dst, span), :], sem.at[slot]).start(priority=queue)
        return carry
    lax.fori_loop(0, n // DMA_QUEUES, body, 0, unroll=DMA_UNROLL // DMA_QUEUES)


def _row_gather_wait(src_hbm, buf, sem, slot, n, span):
    pltpu.make_async_copy(src_hbm.at[pl.ds(0, n * span), :], buf.at[slot], sem.at[slot]).wait()


ZERO_CHUNKS = tuple(2 ** b for b in range(8, -1, -1))


def _dispatch_kernel(base_ref, lo_ref, hi_ref, nu_ref, posa_ref, xa_ref, posb_ref, xb_ref, xs_hbm, zero_ref,
                     sem, *, tb, nb_rows, tm, n_tiles):
    last = pl.num_programs(0) - 1

    def scatter(pos_ref, x_ref, n):
        def body(pair, carry):
            for queue in range(DMA_QUEUES):
                r = pair * DMA_QUEUES + queue
                src = pl.multiple_of(r * ROW_TILES, ROW_TILES)
                dst = pl.multiple_of(_route_row(base_ref, pos_ref[0, r]) * ROW_TILES, ROW_TILES)
                pltpu.make_async_copy(x_ref.at[pl.ds(src, ROW_TILES), :],
                                      xs_hbm.at[pl.ds(dst, ROW_TILES), :], sem.at[0]).start(priority=queue)
            return carry
        lax.fori_loop(0, n // DMA_QUEUES, body, 0, unroll=DMA_UNROLL // DMA_QUEUES)

    def scatter_wait(x_ref, n):
        pltpu.make_async_copy(x_ref, xs_hbm.at[pl.ds(0, n * ROW_TILES), :], sem.at[0]).wait()

    def zero_copy(row, size):
        dst = pl.multiple_of(row * ROW_TILES, ROW_TILES)
        return pltpu.make_async_copy(zero_ref.at[pl.ds(0, size * ROW_TILES), :],
                                     xs_hbm.at[pl.ds(dst, size * ROW_TILES), :], sem.at[1])

    @pl.when(pl.program_id(0) < last)
    def _():
        scatter(posa_ref, xa_ref, tb)
        scatter_wait(xa_ref, tb)

    @pl.when(pl.program_id(0) == last)
    def _():
        scatter(posb_ref, xb_ref, nb_rows)
        zero_ref[...] = jnp.zeros_like(zero_ref)

        def per_pair(e, carry):
            lo = lo_ref[e]
            n = hi_ref[e] - lo
            for wait in (False, True):
                row = lo
                for size in ZERO_CHUNKS:
                    @pl.when((n & size) != 0)
                    def _(row=row, size=size, wait=wait):
                        zero_copy(row, size).wait() if wait else zero_copy(row, size).start()
                    row = row + (n & size)
            return carry
        lax.fori_loop(0, N_PAIRS, per_pair, 0)

        chunk = min(ZERO_CHUNKS[0], tm)
        for wait in (False, True):
            def per_tile(t, carry, wait=wait):
                for c in range(tm // chunk):
                    cp = zero_copy(t * tm + c * chunk, chunk)
                    cp.wait() if wait else cp.start()
                return carry
            lax.fori_loop(nu_ref[0], n_tiles, per_tile, 0)
        scatter_wait(xb_ref, nb_rows)


def _dispatch(pos_a, rows_a, pos_b, rows_b, pad_start, pad_lo, pad_hi, n_used, n_rows, tm):
    tb = min(DISPATCH_TILE, pos_a.size)
    na = pos_a.size // tb
    pos_a = pos_a.reshape(na, 1, tb)
    nb_rows = rows_b.shape[0] // ROW_TILES
    pos_b = pos_b.reshape(1, 1, nb_rows)
    block_a = lambda i, *_: (jnp.minimum(i, na - 1), 0, 0)
    grid_spec = pltpu.PrefetchScalarGridSpec(
        num_scalar_prefetch=4,
        grid=(na + 1,),
        in_specs=[pl.BlockSpec((None, 1, tb), block_a, memory_space=pltpu.SMEM),
                  pl.BlockSpec((tb * ROW_TILES, LANE), lambda i, *_: (jnp.minimum(i, na - 1), 0)),
                  pl.BlockSpec((None, 1, nb_rows), lambda i, *_: (0, 0, 0), memory_space=pltpu.SMEM),
                  pl.BlockSpec((nb_rows * ROW_TILES, LANE), lambda i, *_: (0, 0))],
        out_specs=pl.BlockSpec(memory_space=pl.ANY),
        scratch_shapes=[pltpu.VMEM((ZERO_CHUNKS[0] * ROW_TILES, LANE), F32), pltpu.SemaphoreType.DMA((2,))],
    )
    return pl.pallas_call(
        functools.partial(_dispatch_kernel, tb=tb, nb_rows=nb_rows, tm=tm, n_tiles=n_rows // tm),
        grid_spec=grid_spec,
        out_shape=jax.ShapeDtypeStruct((n_rows * ROW_TILES, LANE), F32),
        compiler_params=_cparams("arbitrary"),
        name="dispatch",
    )(pad_start, pad_lo, pad_hi, n_used, pos_a, rows_a, pos_b, rows_b)


def _expert_kernel(ta_ref, tb_ref, nu_ref, x_ref, w1a_ref, w3a_ref, w2a_ref, w1b_ref, w3b_ref, w2b_ref,
                   y_ref, *, tm):
    @pl.when(pl.program_id(0) < nu_ref[0])
    def _():
        xb = _from_row_tiles(x_ref[...], D_MODEL).astype(BF16)
        ys = []
        for w1_ref, w3_ref, w2_ref in ((w1a_ref, w3a_ref, w2a_ref), (w1b_ref, w3b_ref, w2b_ref)):
            a = _bdot(xb, w1_ref[...])
            h = a * _sigmoid(a) * _bdot(xb, w3_ref[...])
            ys.append(_bdot(h.astype(BF16), w2_ref[...]))
        y_ref[...] = _to_row_tiles(jnp.concatenate(ys, axis=1))

    @pl.when(pl.program_id(0) >= nu_ref[0])
    def _():
        y_ref[...] = jnp.zeros_like(y_ref)


def _experts(xs, tile_a, tile_b, n_used, w1b, w3b, w2b, tm):
    n_tiles = xs.shape[0] // (tm * ROW_TILES)
    tile = lambda i, ta, tb, nu: (jnp.maximum(jnp.minimum(i, nu[0] - 1), 0), 0)
    first = lambda i, ta, tb, nu: (ta[i], 0, 0)
    second = lambda i, ta, tb, nu: (tb[i], 0, 0)
    up = lambda which: pl.BlockSpec((None, D_MODEL, MOE_FF), which)
    down = lambda which: pl.BlockSpec((None, MOE_FF, D_MODEL), which)
    grid_spec = pltpu.PrefetchScalarGridSpec(
        num_scalar_prefetch=3,
        grid=(n_tiles,),
        in_specs=[pl.BlockSpec((tm * ROW_TILES, LANE), tile),
                  up(first), up(first), down(first), up(second), up(second), down(second)],
        out_specs=pl.BlockSpec((tm * 2 * ROW_TILES, LANE), lambda i, ta, tb, nu: (i, 0)),
    )
    return pl.pallas_call(
        functools.partial(_expert_kernel, tm=tm),
        grid_spec=grid_spec,
        out_shape=jax.ShapeDtypeStruct((2 * xs.shape[0], LANE), F32),
        compiler_params=_cparams("arbitrary"),
        name="experts",
    )(tile_a, tile_b, n_used, xs, w1b, w3b, w2b, w1b, w3b, w2b)


def _expert_layout(counts, tm, n_tok):
    n_tiles = n_tok // tm + MOE_GROUPS * PAIRS_PER_GROUP
    padded = ((counts + tm - 1) // tm) * tm
    pad_end = jnp.cumsum(padded)
    pad_start = pad_end - padded
    n_used = pad_end[-1] // tm
    tile_start = jnp.minimum(jnp.arange(n_tiles, dtype=jnp.int32), n_used - 1) * tm
    tile_pair = jnp.sum((pad_end[None, :] <= tile_start[:, None]).astype(jnp.int32), axis=1)
    tile_pair = jnp.minimum(tile_pair, MOE_GROUPS * PAIRS_PER_GROUP - 1)
    members = [(lo, hi) for lo in range(MOE_EXPERTS) for hi in range(lo + 1, MOE_EXPERTS)]
    lo_of = jnp.asarray([m[0] for m in members], jnp.int32)
    hi_of = jnp.asarray([m[1] for m in members], jnp.int32)
    group = tile_pair // PAIRS_PER_GROUP
    tile_a = group * MOE_EXPERTS + lo_of[tile_pair % PAIRS_PER_GROUP]
    tile_b = group * MOE_EXPERTS + hi_of[tile_pair % PAIRS_PER_GROUP]
    return (pad_start, pad_start + counts, pad_end, tile_a.astype(jnp.int32), tile_b.astype(jnp.int32),
            n_used.astype(jnp.int32).reshape(1), n_tiles * tm)


def _combine_kernel(base_ref, cur_ref, nxt_ref, x1_ref, wc_ref, gt_ref, fn_ref, y_hbm, o_ref, ybuf, sem,
                    *, tl):
    i = pl.program_id(0)
    slot = i % 2

    span = 2 * ROW_TILES

    @pl.when(i == 0)
    def _():
        _row_gather(base_ref, cur_ref, y_hbm, ybuf, sem, 0, tl, span)

    @pl.when(i + 1 < pl.num_programs(0))
    def _():
        _row_gather(base_ref, nxt_ref, y_hbm, ybuf, sem, 1 - slot, tl, span)

    _row_gather_wait(y_hbm, ybuf, sem, slot, tl, span)
    both = _from_row_tiles(ybuf[slot], 2 * D_MODEL)
    moe = wc_ref[:, 0:1] * both[:, :D_MODEL] + wc_ref[:, 1:2] * both[:, D_MODEL:]
    x2 = x1_ref[...].astype(F32) + gt_ref[...] * moe
    o_ref[...] = x2 * lax.rsqrt(jnp.mean(x2 * x2, axis=-1, keepdims=True) + EPS) * fn_ref[...]


def _combine(x1_flat, pos, pad_start, y_rows, wcol, gate2, final_norm, seq):
    n_tok = x1_flat.shape[0]
    n_tiles, _, tl = pos.shape
    last = n_tiles - 1
    if gate2.shape[1] == 1:
        gate_spec = pl.BlockSpec((None, 1, D_MODEL), lambda i, base: ((i * tl) // seq, 0, 0))
    else:
        gate2 = gate2.reshape(n_tok, D_MODEL)
        gate_spec = pl.BlockSpec((tl, D_MODEL), lambda i, base: (i, 0))
    grid_spec = pltpu.PrefetchScalarGridSpec(
        num_scalar_prefetch=1,
        grid=(n_tiles,),
        in_specs=[pl.BlockSpec((None, 1, tl), lambda i, base: (i, 0, 0), memory_space=pltpu.SMEM),
                  pl.BlockSpec((None, 1, tl), lambda i, base: (jnp.minimum(i + 1, last), 0, 0),
                               memory_space=pltpu.SMEM),
                  pl.BlockSpec((tl, D_MODEL), lambda i, base: (i, 0)),
                  pl.BlockSpec((tl, LANE), lambda i, base: (i, 0)),
                  gate_spec,
                  pl.BlockSpec((1, D_MODEL), lambda i, base: (0, 0)),
                  pl.BlockSpec(memory_space=pl.ANY)],
        out_specs=pl.BlockSpec((tl, D_MODEL), lambda i, base: (i, 0)),
        scratch_shapes=[pltpu.VMEM((2, 2 * tl * ROW_TILES, LANE), F32), pltpu.SemaphoreType.DMA((2,))],
    )
    return pl.pallas_call(
        functools.partial(_combine_kernel, tl=tl),
        grid_spec=grid_spec,
        out_shape=jax.ShapeDtypeStruct((n_tok, D_MODEL), F32),
        compiler_params=_cparams("arbitrary"),
        name="combine",
    )(pad_start, pos, pos, x1_flat, wcol, gate2, final_norm, y_rows)


def _rope_tables(seq, offset):
    half = RET_DK // 2
    theta = 1.0 / (ROPE_BASE ** jnp.linspace(0.0, 1.0, half, dtype=F32))
    pos = offset + jnp.arange(seq)
    ang = pos.astype(F32)[:, None] * theta[None, :]
    cos = jnp.cos(ang)
    sin = jnp.sin(ang)
    return jnp.concatenate([cos, cos], axis=1), jnp.concatenate([-sin, sin], axis=1)


def _mixers(x, mod, h0, s0, offset, p, placed):
    bsz, seq, _ = x.shape
    mods = [m.reshape(bsz, 1, D_MODEL) for m in jnp.split(mod, 6, axis=-1)]
    cos2, sin2 = _rope_tables(seq, offset)
    flat = seq < TOKEN_TILE and bsz * seq <= TOKEN_TILE
    sets, rows = (1, bsz * seq) if flat else (bsz, seq)
    tok = lambda a: a.reshape(sets, rows, a.shape[-1])
    if flat:
        mods = [jnp.broadcast_to(m, (bsz, seq, D_MODEL)).reshape(1, rows, D_MODEL) for m in mods]
        cos2, sin2 = jnp.tile(cos2, (bsz, 1)), jnp.tile(sin2, (bsz, 1))
    sh1, sc1, gt1, sh2, sc2, gt2 = mods
    seqs = lambda a: a.reshape(bsz, seq, a.shape[-1])
    u, q, k, v, g, ga, gb = _inproj(tok(x), sh1, sc1, p['norm1'], p['w_in'], cos2, sin2)
    ya, h_t = _s5(seqs(u), h0, p['a_lanes'], p['bm'], p['cm'], p['d'], p['w_glu'], p['b_glu'], p['w_s5_out'])
    yb, s_t = _retention(seqs(q), seqs(k), seqs(v), seqs(g), s0, p['w_ret_out'])
    x1, n2, route_e, wcol, cnt = _merge(tok(x), tok(ya), tok(yb), ga, gb, gt1, sh2, sc2, p['norm2'],
                                        p['w_out'], p['wr'], p['br'], placed)
    tb = min(ROUTE_TILE, rows)
    codes = route_e[:, 0, :].reshape(bsz * seq // tb, 1, tb)
    return dict(x1=x1.reshape(bsz * seq, D_MODEL), n2=n2, codes=codes, wcol=wcol,
                counts=jnp.sum(cnt[:, :, 0], axis=0), gate2=gt2, seq=rows, h=h_t, s=s_t)


def _moe(a, b, p, final_norm):
    n_tok = a['x1'].shape[0] + b['x1'].shape[0]
    pad_start, pad_lo, pad_hi, tile_a, tile_b, n_used, n_rows = _expert_layout(
        a['counts'] + b['counts'], EXPERT_TILE, n_tok)
    xs = _dispatch(a['codes'], a['n2'], b['codes'], b['n2'], pad_start, pad_lo, pad_hi, n_used, n_rows,
                   EXPERT_TILE)
    y_rows = _experts(xs, tile_a, tile_b, n_used, p['w1'], p['w3'], p['w2'], EXPERT_TILE)
    return [_combine(m['x1'], m['codes'], pad_start, y_rows, m['wcol'], m['gate2'], final_norm, m['seq'])
            for m in (a, b)]


def kernel(x_prompt, x_sample, state_s5_re, state_s5_im, state_ret, c_prompt, c_sample, w_ada, b_ada, norm1, norm2, w_in, s5_a_re, s5_a_im, s5_log_dt, s5_b_re, s5_b_im, s5_c_re, s5_c_im, s5_d, s5_w_glu, s5_b_glu, w_s5_out, w_ret_out, w_out, w_rg, b_rg, w_re, b_re, w1, w3, w2, final_norm):
    depth = w_ada.shape[0]
    assert depth == 1
    bp = x_prompt.shape[0]
    bs, seq_s, _ = x_sample.shape
    l = 0
    a_lanes, bm, cm = _s5_params(s5_a_re[l], s5_a_im[l], s5_log_dt[l], s5_b_re[l], s5_b_im[l],
                                 s5_c_re[l], s5_c_im[l])
    wr, br = _router_weights(w_rg[l], b_rg[l], w_re[l], b_re[l])
    p = dict(
        norm1=norm1[l].astype(F32).reshape(1, D_MODEL), norm2=norm2[l].astype(F32).reshape(1, D_MODEL),
        w_in=w_in[l].astype(BF16), a_lanes=a_lanes, bm=bm, cm=cm,
        d=s5_d[l].astype(F32).reshape(1, S5_WIDTH), w_glu=s5_w_glu[l].astype(BF16),
        b_glu=s5_b_glu[l].astype(F32).reshape(1, S5_WIDTH), w_s5_out=w_s5_out[l].astype(BF16),
        w_ret_out=w_ret_out[l].astype(BF16), w_out=w_out[l].astype(BF16), wr=wr, br=br,
        w1=w1[l].astype(BF16).reshape(N_EXPERTS, D_MODEL, MOE_FF),
        w3=w3[l].astype(BF16).reshape(N_EXPERTS, D_MODEL, MOE_FF),
        w2=w2[l].astype(BF16).reshape(N_EXPERTS, MOE_FF, D_MODEL))
    fn = final_norm.astype(F32).reshape(1, D_MODEL)
    mod = _mod(jnp.concatenate([c_prompt, c_sample], axis=0).astype(F32), w_ada[l], b_ada[l])

    h0_p = jnp.zeros((bp, S5_LANES), F32)
    prompt = _mixers(x_prompt, mod[:bp], h0_p, None, 0, p, jnp.zeros((N_PAIRS,), jnp.int32))
    h0_s = _s5_state_to_lanes(state_s5_re[l], state_s5_im[l])
    sample = _mixers(x_sample, mod[bp:], h0_s, state_ret[l].astype(F32), PAST_LEN, p, prompt['counts'])
    y_p, y_s = _moe(prompt, sample, p, fn)
    p_re, p_im = _s5_state_from_lanes(prompt['h'])
    s_re, s_im = _s5_state_from_lanes(sample['h'])
    return (y_p.reshape(x_prompt.shape), y_s.reshape(x_sample.shape), p_re[None], p_im[None],
            prompt['s'][None], s_re[None], s_im[None], sample['s'][None])
```
